```python
import math
import jax, jax.numpy as jnp
from jax import lax
import numpy as np


D_MODEL = 4096
BATCH = 4
SEQ = 2048
DEPTH = 2

CHUNK = 64
Q_BLOCK = 128
HEAD_DIM = 128
N_BRANCHES = 4
HEADS_PER_MIXER = D_MODEL // (N_BRANCHES * HEAD_DIM)
BRANCH_WIDTH = HEADS_PER_MIXER * HEAD_DIM
DIFF_DIM = HEAD_DIM // 2
IDX_HEADS = 8
IDX_DIM = 64
TOPK_MAX = 256
ROPE_THETA = 500000.0
ROPE_FRACTION = 4
N_EXPERTS = 32
TOP_K = 4
EXPERT_FF = 512
SWIGLU_LIMIT = 7.0
SWIGLU_ALPHA = 1.702
RMS_EPS = 1e-5
N_MOD = 6
MAX_POS_OFFSET = 8192

IN_SPLITS = (
    BRANCH_WIDTH, BRANCH_WIDTH, BRANCH_WIDTH,
    BRANCH_WIDTH, BRANCH_WIDTH, BRANCH_WIDTH, HEADS_PER_MIXER,
    BRANCH_WIDTH, BRANCH_WIDTH, BRANCH_WIDTH,
    BRANCH_WIDTH, HEAD_DIM, HEAD_DIM,
    IDX_HEADS * IDX_DIM, IDX_DIM, IDX_HEADS,
)
N_IN = sum(IN_SPLITS)

kernel_name = 'hybrid_chunk_causal_gated_mixers_moe'


def rms_norm(x, g):
    xf = x.astype(jnp.float32)
    y = xf * lax.rsqrt(jnp.mean(xf * xf, axis=-1, keepdims=True) + RMS_EPS)
    return y.astype(x.dtype) * g


def partial_rope(x, positions):
    d = x.shape[-1]
    rot = d // ROPE_FRACTION
    half = rot // 2
    inv_freq = jnp.float32(ROPE_THETA) ** (-jnp.arange(half, dtype=jnp.float32) / half)
    ang = positions.astype(jnp.float32)[:, :, None, None] * inv_freq
    cos, sin = jnp.cos(ang), jnp.sin(ang)
    xr = x[..., :rot].astype(jnp.float32)
    x1, x2 = xr[..., :half], xr[..., half:]
    rotated = jnp.concatenate([x1 * cos - x2 * sin, x2 * cos + x1 * sin], axis=-1).astype(x.dtype)
    return jnp.concatenate([rotated, x[..., rot:]], axis=-1)


def diff_attention_block(q, k, v, mask, lam, lam_init, sub_g):
    scale = DIFF_DIM ** -0.5

    def attn_map(qm, km):
        s = jnp.einsum('bqhd,bkhd->bhqk', qm, km).astype(jnp.float32) * scale
        return jax.nn.softmax(jnp.where(mask, s, -jnp.inf), axis=-1)

    a = attn_map(q[..., :DIFF_DIM], k[..., :DIFF_DIM]) - lam * attn_map(q[..., DIFF_DIM:], k[..., DIFF_DIM:])
    o = jnp.einsum('bhqk,bkhd->bqhd', a.astype(v.dtype), v)
    return rms_norm(o, sub_g) * (1.0 - lam_init)


def forgetting_attention_block(q, k, v, dcum_q, dcum_k, mask):
    s = jnp.einsum('bqhd,bkhd->bhqk', q, k).astype(jnp.float32) * HEAD_DIM ** -0.5
    s = s + dcum_q[..., :, None] - dcum_k[..., None, :]
    p = jax.nn.softmax(jnp.where(mask, s, -jnp.inf), axis=-1)
    return jnp.einsum('bhqk,bkhd->bqhd', p.astype(v.dtype), v)


def stick_breaking_block(q, k, v, mask):
    z = jnp.einsum('bqhd,bkhd->bhqk', q, k).astype(jnp.float32) * HEAD_DIM ** -0.5
    log_beta = jax.nn.log_sigmoid(z)
    log_keep = jnp.where(mask, jax.nn.log_sigmoid(-z), 0.0)
    after = lax.cumsum(log_keep, axis=3, reverse=True) - log_keep
    w = jnp.where(mask, jnp.exp(log_beta + after), 0.0)
    return jnp.einsum('bhqk,bkhd->bqhd', w.astype(v.dtype), v)


def sparse_indexed_block(q, k_sh, v_sh, iq, ik, iw, mask, chunk_end, topk):
    isc = jax.nn.relu(jnp.einsum('bqhd,bsd->bqhs', iq, ik).astype(jnp.float32))
    score = jnp.einsum('bqh,bqhs->bqs', iw.astype(jnp.float32), isc)
    score = jnp.where(mask[None], score, -jnp.inf)
    _, idx = lax.top_k(score, topk)
    gather = jax.vmap(lambda tb, ib: tb[ib])
    kg, vg = gather(k_sh, idx), gather(v_sh, idx)
    valid = idx < chunk_end[None, :, None]
    s = jnp.einsum('bqhd,bqkd->bhqk', q, kg).astype(jnp.float32) * HEAD_DIM ** -0.5
    p = jax.nn.softmax(jnp.where(valid[:, None], s, -jnp.inf), axis=-1)
    return jnp.einsum('bhqk,bqkd->bqhd', p.astype(vg.dtype), vg)


def token_mixers(h, positions, w_in, b_forget, diff_lambda, diff_norm_g, lam_init):
    B, S, _ = h.shape
    H = HEADS_PER_MIXER
    offsets = np.cumsum(IN_SPLITS)[:-1].tolist()
    (a_q, a_k, a_v, f_q, f_k, f_v, f_logit, s_q, s_k, s_v,
     d_q, d_k, d_v, i_q, i_k, i_w) = jnp.split(h @ w_in, offsets, axis=-1)
    heads = lambda t, n: t.reshape(B, S, n, -1)

    a_q = partial_rope(heads(a_q, 2 * H), positions).reshape(B, S, H, HEAD_DIM)
    a_k = partial_rope(heads(a_k, 2 * H), positions).reshape(B, S, H, HEAD_DIM)
    a_v = heads(a_v, H)
    lq1, lk1, lq2, lk2 = [diff_lambda[i].astype(jnp.float32) for i in range(4)]
    lam = jnp.exp(jnp.sum(lq1 * lk1)) - jnp.exp(jnp.sum(lq2 * lk2)) + lam_init

    f_q, f_k, f_v = heads(f_q, H), heads(f_k, H), heads(f_v, H)
    log_f = jax.nn.log_sigmoid((f_logit + b_forget).astype(jnp.float32))
    dcum = jnp.cumsum(log_f, axis=1).transpose(0, 2, 1)

    s_q, s_k, s_v = heads(s_q, H), heads(s_k, H), heads(s_v, H)

    d_q = partial_rope(heads(d_q, H), positions)
    d_k = partial_rope(d_k[:, :, None], positions)[:, :, 0]
    i_q = partial_rope(heads(i_q, IDX_HEADS), positions)
    i_k = partial_rope(i_k[:, :, None], positions)[:, :, 0]
    topk = min(TOPK_MAX, S // 4)

    kpos = jnp.arange(S)

    def block(i):
        q0 = i * Q_BLOCK
        qpos = q0 + jnp.arange(Q_BLOCK)
        sl = lambda t: lax.dynamic_slice_in_dim(t, q0, Q_BLOCK, axis=1)
        chunk_end = (qpos // CHUNK + 1) * CHUNK
        chunk_vis = kpos[None, :] < chunk_end[:, None]
        incl = kpos[None, :] <= qpos[:, None]
        strict = kpos[None, :] < qpos[:, None]
        o_a = diff_attention_block(sl(a_q), a_k, a_v, chunk_vis, lam, lam_init, diff_norm_g)
        o_b = forgetting_attention_block(sl(f_q), f_k, f_v,
                                         lax.dynamic_slice_in_dim(dcum, q0, Q_BLOCK, axis=2), dcum, incl)
        o_c = stick_breaking_block(sl(s_q), s_k, s_v, strict)
        o_d = sparse_indexed_block(sl(d_q), d_k, d_v, sl(i_q), i_k, sl(i_w), chunk_vis, chunk_end, topk)
        return jnp.stack([o.reshape(B, Q_BLOCK, BRANCH_WIDTH) for o in (o_a, o_b, o_c, o_d)], axis=0)

    outs = lax.map(block, jnp.arange(S // Q_BLOCK))
    return outs.transpose(1, 2, 0, 3, 4).reshape(N_BRANCHES, B, S, BRANCH_WIDTH)


def gated_merge(h, outs, w_branch, w_gate, b_gate, w_out):
    merged = None
    for i in range(N_BRANCHES):
        g = jax.nn.sigmoid((h @ w_gate[i] + b_gate[i]).astype(jnp.float32)).astype(h.dtype)
        term = g * (outs[i] @ w_branch[i])
        merged = term if merged is None else merged + term
    return merged @ w_out


def expert_ffn(xt, w_gu, b_gu, w_dn, b_dn):
    gu = xt @ w_gu + b_gu
    glu = jnp.minimum(gu[..., ::2], SWIGLU_LIMIT)
    lin = jnp.clip(gu[..., 1::2], -SWIGLU_LIMIT, SWIGLU_LIMIT)
    act = glu * jax.nn.sigmoid(SWIGLU_ALPHA * glu) * (lin + 1.0)
    return act @ w_dn + b_dn


def moe(h, w_router, b_router, w_gu, b_gu, w_dn, b_dn):
    B, S, D = h.shape
    xt = h.reshape(B * S, D)
    logits = (xt @ w_router + b_router).astype(jnp.float32)
    top_vals, top_idx = lax.top_k(logits, TOP_K)
    top_w = jax.nn.softmax(top_vals, axis=-1)
    combine = jnp.einsum('tk,tke->te', top_w,
                         jax.nn.one_hot(top_idx, N_EXPERTS, dtype=jnp.float32)).astype(h.dtype)
    y = jnp.zeros_like(xt)
    for e in range(N_EXPERTS):
        y = y + combine[:, e:e + 1] * expert_ffn(xt, w_gu[e], b_gu[e], w_dn[e], b_dn[e])
    return y.reshape(B, S, D)


def setup_inputs(seed: int = 0) -> dict:
    key = jax.random.key(seed)
    ks = jax.random.split(key, 24)
    f32 = jnp.float32
    D = D_MODEL
    nrm = lambda k, shape, scale: jax.random.normal(k, shape, f32) * scale
    x = nrm(ks[0], (BATCH, SEQ, D), 1.0)
    c = nrm(ks[1], (BATCH, D), 1.0)
    offset = jax.random.randint(ks[2], (BATCH, 1), 0, MAX_POS_OFFSET, dtype=jnp.int32)
    positions = offset + jnp.arange(SEQ, dtype=jnp.int32)[None, :]
    return {
        'x': x,
        'c': c,
        'positions': positions,
        'w_mod': nrm(ks[3], (D, N_MOD * D), 0.5 * D ** -0.5),
        'b_mod': nrm(ks[4], (N_MOD * D,), 0.01),
        'ada_table': nrm(ks[5], (DEPTH, N_MOD, D), 0.1),
        'norm1_g': 1.0 + nrm(ks[6], (DEPTH, D), 0.01),
        'w_in': nrm(ks[7], (DEPTH, D, N_IN), D ** -0.5),
        'b_forget': 2.0 + nrm(ks[8], (DEPTH, HEADS_PER_MIXER), 0.5),
        'diff_lambda': nrm(ks[9], (DEPTH, 4, DIFF_DIM), 0.1),
        'diff_norm_g': 1.0 + nrm(ks[10], (DEPTH, HEAD_DIM), 0.01),
        'w_branch': nrm(ks[11], (DEPTH, N_BRANCHES, BRANCH_WIDTH, D), BRANCH_WIDTH ** -0.5),
        'w_gate': nrm(ks[12], (DEPTH, N_BRANCHES, D, D), D ** -0.5),
        'b_gate': nrm(ks[13], (DEPTH, N_BRANCHES, D), 0.01),
        'w_out': nrm(ks[14], (DEPTH, D, D), D ** -0.5),
        'norm2_g': 1.0 + nrm(ks[15], (DEPTH, D), 0.01),
        'w_router': nrm(ks[16], (DEPTH, D, N_EXPERTS), D ** -0.5),
        'b_router': nrm(ks[17], (DEPTH, N_EXPERTS), 0.01),
        'w_gu': nrm(ks[18], (DEPTH, N_EXPERTS, D, 2 * EXPERT_FF), D ** -0.5),
        'b_gu': nrm(ks[19], (DEPTH, N_EXPERTS, 2 * EXPERT_FF), 0.01),
        'w_dn': nrm(ks[20], (DEPTH, N_EXPERTS, EXPERT_FF, D), EXPERT_FF ** -0.5),
        'b_dn': nrm(ks[21], (DEPTH, N_EXPERTS, D), 0.01),
        'final_norm_g': 1.0 + nrm(ks[22], (D,), 0.01),
    }


def reference(x, c, positions, w_mod, b_mod, ada_table, norm1_g, w_in, b_forget, diff_lambda,
              diff_norm_g, w_branch, w_gate, b_gate, w_out, norm2_g, w_router, b_router,
              w_gu, b_gu, w_dn, b_dn, final_norm_g):
    B = x.shape[0]
    mod = (jax.nn.silu(c) @ w_mod + b_mod).reshape(B, N_MOD, D_MODEL)
    for l in range(DEPTH):
        m = mod + ada_table[l][None]
        sh1, sc1, g1, sh2, sc2, g2 = [m[:, i, None, :] for i in range(N_MOD)]
        lam_init = 0.8 - 0.6 * math.exp(-0.3 * l)

        h = rms_norm(x, norm1_g[l]) * (1.0 + sc1) + sh1
        outs = token_mixers(h, positions, w_in[l], b_forget[l], diff_lambda[l], diff_norm_g[l], lam_init)
        x = x + g1 * gated_merge(h, outs, w_branch[l], w_gate[l], b_gate[l], w_out[l])

        h = rms_norm(x, norm2_g[l]) * (1.0 + sc2) + sh2
        x = x + g2 * moe(h, w_router[l], b_router[l], w_gu[l], b_gu[l], w_dn[l], b_dn[l])
    return rms_norm(x, final_norm_g)
```

```python
import functools

import numpy as np
import jax
import jax.numpy as jnp
from jax import lax
from jax.experimental import pallas as pl
from jax.experimental.pallas import tpu as pltpu

F32 = jnp.float32
BF16 = jnp.bfloat16
I32 = jnp.int32

LANES = 128
HEAD_DIM = 128
CHUNK = 64
IDX_HEADS = 8
IDX_DIM = 64
TOPK_MAX = 256
ROPE_THETA = 500000.0
ROPE_FRACTION = 4
TOP_K = 4
SWIGLU_LIMIT = 7.0
SWIGLU_ALPHA = 1.702
RMS_EPS = 1e-5
N_MOD = 6
NEG = -1e30
INT_MIN = -(2 ** 31)
VMEM_LIMIT = 56 * 1024 * 1024


def _params(n_axes, vmem=VMEM_LIMIT):
    return pltpu.CompilerParams(dimension_semantics=("arbitrary",) * n_axes,
                                vmem_limit_bytes=vmem)


def _tile(n, pref):
    t = min(n, pref)
    while n % t:
        t //= 2
    return t


def _norm_kernel(x_ref, g_ref, sc_ref, sh_ref, *o_refs):
    x = x_ref[...]
    var = jnp.mean(x * x, axis=-1, keepdims=True)
    y = x * lax.rsqrt(var + RMS_EPS) * g_ref[...]
    h = y * (1.0 + sc_ref[...]) + sh_ref[...]
    for o in o_refs:
        o[...] = h.astype(o.dtype)


def _ada_norm(x, g, m4, sc_idx, sh_idx, seq, out_dtypes):
    t, d = x.shape
    tm = _tile(seq, 256)
    per_b = seq // tm
    return pl.pallas_call(
        _norm_kernel,
        grid=(t // tm,),
        in_specs=[
            pl.BlockSpec((tm, d), lambda i: (i, 0)),
            pl.BlockSpec((1, d), lambda i: (0, 0)),
            pl.BlockSpec((None, None, 1, d), lambda i: (i // per_b, sc_idx, 0, 0)),
            pl.BlockSpec((None, None, 1, d), lambda i: (i // per_b, sh_idx, 0, 0)),
        ],
        out_specs=[pl.BlockSpec((tm, d), lambda i: (i, 0)) for _ in out_dtypes],
        out_shape=[jax.ShapeDtypeStruct((t, d), dt) for dt in out_dtypes],
        compiler_params=_params(1),
        name="ada_norm",
    )(x, g, m4, m4)


def _mm_kernel(*refs, n_extra, epilogue, prologue, cast_rows):
    x_ref, w_ref, b_ref = refs[:3]
    extra = refs[3:3 + n_extra]
    o_ref = refs[3 + n_extra]
    wbf_ref = refs[4 + n_extra]

    @pl.when(pl.program_id(1) == 0)
    def _():
        k = w_ref.shape[0]

        def body(r, c):
            rows = pl.ds(pl.multiple_of(r * cast_rows, cast_rows), cast_rows)
            wbf_ref[rows, :] = w_ref[rows, :].astype(BF16)
            return c

        lax.fori_loop(0, k // cast_rows, body, 0)

    x = x_ref[...]
    if prologue is not None:
        x = prologue(x)
    acc = jnp.dot(x.astype(BF16), wbf_ref[...], preferred_element_type=F32) + b_ref[...]
    o_ref[...] = epilogue(acc, *[e[...] for e in extra]).astype(o_ref.dtype)


def _matmul(x, w, w_spec, k, n, bias, *, tm, tn, out_dtype, epilogue=None, extra=(),
            extra_specs=(), prologue=None, name="matmul"):
    m = x.shape[0]
    if epilogue is None:
        epilogue = lambda acc: acc
    kern = functools.partial(_mm_kernel, n_extra=len(extra), epilogue=epilogue,
                             prologue=prologue, cast_rows=_tile(k, 256))
    return pl.pallas_call(
        kern,
        grid=(n // tn, m // tm),
        in_specs=[pl.BlockSpec((tm, k), lambda j, i: (i, 0)),
                  w_spec,
                  pl.BlockSpec((1, tn), lambda j, i: (0, j)),
                  *extra_specs],
        out_specs=pl.BlockSpec((tm, tn), lambda j, i: (i, j)),
        out_shape=jax.ShapeDtypeStruct((m, n), out_dtype),
        scratch_shapes=[pltpu.VMEM((k, tn), BF16)],
        compiler_params=_params(2),
        name=name,
    )(x, w, bias, *extra)


def _rope_epilogue(acc, tab, *, groups):
    outs = []
    for g, kind in enumerate(groups):
        xg = acc[:, g * LANES:(g + 1) * LANES]
        if kind is not None:
            slot, half = kind
            base = 3 * slot * LANES
            cos = tab[:, base:base + LANES]
            sin_lo = tab[:, base + LANES:base + 2 * LANES]
            sin_hi = tab[:, base + 2 * LANES:base + 3 * LANES]
            xg = (xg * cos + pltpu.roll(xg, half, 1) * sin_lo
                  + pltpu.roll(xg, LANES - half, 1) * sin_hi)
        outs.append(xg)
    return outs[0] if len(outs) == 1 else jnp.concatenate(outs, axis=1)


def _rope_table(pos, period, active):
    rot = period // ROPE_FRACTION
    half = rot // 2
    inv_freq = jnp.float32(ROPE_THETA) ** (-jnp.arange(half, dtype=F32) / half)
    lane = np.arange(LANES)
    r = lane % period
    on = lane < active
    first = on & (r < half)
    second = on & (r >= half) & (r < rot)
    fidx = np.where(first, r, np.where(second, r - half, 0))
    ang = pos[:, None] * inv_freq[fidx][None, :]
    cos, sin = jnp.cos(ang), jnp.sin(ang)
    c = jnp.where((first | second)[None, :], cos, 1.0)
    s_lo = jnp.where(second[None, :], sin, 0.0)
    s_hi = jnp.where(first[None, :], -sin, 0.0)
    return jnp.concatenate([c, s_lo, s_hi], axis=1)


def _qk(q, k):
    return lax.dot_general(q, k, (((1,), (1,)), ((), ())), preferred_element_type=F32)


def _softmax_tile(carry, s, v):
    m, l, acc = carry
    m_new = jnp.maximum(m, jnp.max(s, axis=-1, keepdims=True))
    alpha = jnp.exp(m - m_new)
    p = jnp.exp(s - m_new)
    l = alpha * l + jnp.sum(p, axis=-1, keepdims=True)
    acc = alpha * acc + jnp.dot(p.astype(BF16), v, preferred_element_type=F32)
    return m_new, l, acc


def _softmax_init(rows):
    return (jnp.full((rows, 1), NEG, F32), jnp.zeros((rows, 1), F32),
            jnp.zeros((rows, HEAD_DIM), F32))


def _rows(ref, j, t):
    return ref[pl.ds(pl.multiple_of(j * t, t), t), :]


def _attn_diff_kernel(dl_ref, g_ref, q_ref, k_ref, v_ref, o_ref, *, tq, lam_init):
    i = pl.program_id(2)
    half = HEAD_DIM // 2
    scale = half ** -0.5
    q = q_ref[...]
    lane = lax.broadcasted_iota(I32, q.shape, 1)
    zero = jnp.zeros_like(q)
    qz = jnp.concatenate([jnp.where(lane < half, q, zero), jnp.where(lane >= half, q, zero)], axis=0)

    def full_tile(j, carry):
        return _softmax_tile(carry, _qk(qz, _rows(k_ref, j, tq)) * scale, _rows(v_ref, j, tq))

    carry = lax.fori_loop(0, i, full_tile, _softmax_init(2 * tq))
    r = lax.broadcasted_iota(I32, (2 * tq, tq), 0)
    r = jnp.where(r >= tq, r - tq, r)
    c = lax.broadcasted_iota(I32, (2 * tq, tq), 1)
    vis = c < (r // CHUNK + 1) * CHUNK
    s = jnp.where(vis, _qk(qz, _rows(k_ref, i, tq)) * scale, NEG)
    _, l, acc = _softmax_tile(carry, s, _rows(v_ref, i, tq))

    dl = dl_ref[...]
    lam = (jnp.exp(jnp.sum(dl[0:1] * dl[1:2], axis=-1, keepdims=True))
           - jnp.exp(jnp.sum(dl[2:3] * dl[3:4], axis=-1, keepdims=True)) + lam_init)
    o = acc / l
    o = o[:tq] - lam * o[tq:]
    var = jnp.mean(o * o, axis=-1, keepdims=True)
    o_ref[...] = (o * lax.rsqrt(var + RMS_EPS) * g_ref[...] * (1.0 - lam_init)).astype(o_ref.dtype)


def _attn_forget_kernel(dq_ref, dk_ref, q_ref, k_ref, v_ref, o_ref, *, tq):
    i = pl.program_id(2)
    scale = HEAD_DIM ** -0.5
    q = q_ref[...]
    dq = dq_ref[...]

    def scores(j):
        dk = dk_ref[:, pl.ds(pl.multiple_of(j * tq, tq), tq)]
        return _qk(q, _rows(k_ref, j, tq)) * scale + dq - dk

    def full_tile(j, carry):
        return _softmax_tile(carry, scores(j), _rows(v_ref, j, tq))

    carry = lax.fori_loop(0, i, full_tile, _softmax_init(tq))
    r = lax.broadcasted_iota(I32, (tq, tq), 0)
    c = lax.broadcasted_iota(I32, (tq, tq), 1)
    s = jnp.where(c <= r, scores(i), NEG)
    _, l, acc = _softmax_tile(carry, s, _rows(v_ref, i, tq))
    o_ref[...] = (acc / l).astype(o_ref.dtype)


def _log_sigmoid(z):
    return jnp.minimum(z, 0.0) - jnp.log(1.0 + jnp.exp(-jnp.abs(z)))


def _attn_stick_kernel(q_ref, k_ref, v_ref, o_ref, *, tq):
    i = pl.program_id(2)
    scale = HEAD_DIM ** -0.5
    q = q_ref[...]
    r = lax.broadcasted_iota(I32, (tq, tq), 0)
    c = lax.broadcasted_iota(I32, (tq, tq), 1)
    later = jnp.where(r > c, 1.0, 0.0).astype(BF16)

    def tile(j, carry, mask):
        suffix, acc = carry
        z = _qk(q, _rows(k_ref, j, tq)) * scale
        log_beta = _log_sigmoid(z)
        log_keep = log_beta - z
        if mask is not None:
            log_keep = jnp.where(mask, log_keep, 0.0)
        hi = log_keep.astype(BF16)
        lo = (log_keep - hi.astype(F32)).astype(BF16)
        after = (jnp.dot(hi, later, preferred_element_type=F32)
                 + jnp.dot(lo, later, preferred_element_type=F32) + suffix)
        w = jnp.exp(log_beta + after)
        if mask is not None:
            w = jnp.where(mask, w, 0.0)
        acc = acc + jnp.dot(w.astype(BF16), _rows(v_ref, j, tq), preferred_element_type=F32)
        return suffix + jnp.sum(log_keep, axis=-1, keepdims=True), acc

    carry = tile(i, (jnp.zeros((tq, 1), F32), jnp.zeros((tq, HEAD_DIM), F32)), c < r)
    _, acc = lax.fori_loop(0, i, lambda jj, cr: tile(i - 1 - jj, cr, None), carry)
    o_ref[...] = acc.astype(o_ref.dtype)


def _attention(kernel, q_arr, q_col, k_arr, k_col, v_arr, v_col, heads, batch, seq, tq,
               pre=(), pre_specs=(), name="attention"):
    nq = seq // tq
    t = batch * seq
    return pl.pallas_call(
        kernel,
        grid=(batch, heads, nq),
        in_specs=[*pre_specs,
                  pl.BlockSpec((tq, HEAD_DIM), lambda b, h, i: (b * nq + i, q_col + h)),
                  pl.BlockSpec((seq, HEAD_DIM), lambda b, h, i: (b, k_col + h)),
                  pl.BlockSpec((seq, HEAD_DIM), lambda b, h, i: (b, v_col + h))],
        out_specs=pl.BlockSpec((tq, HEAD_DIM), lambda b, h, i: (b * nq + i, h)),
        out_shape=jax.ShapeDtypeStruct((t, heads * HEAD_DIM), BF16),
        compiler_params=_params(3),
        name=name,
    )(*pre, q_arr, k_arr, v_arr)


def _sortable(x):
    bits = pltpu.bitcast(x + 0.0, I32)
    return jnp.where(bits < 0, bits ^ jnp.int32(0x7FFFFFFF), bits)


def _attn_sparse_kernel(q_ref, sq_ref, kv_ref, ik_ref, o_ref, key_ref, *, tq, seq, heads, topk,
                        iq_col, iw_col):
    i = pl.program_id(1)
    scale = HEAD_DIM ** -0.5
    n_tiles = seq // tq
    q0 = i * tq

    sq = sq_ref[...]
    lane_k = lax.broadcasted_iota(I32, (tq, LANES), 1)
    iq_groups = [sq[:, iq_col + g * LANES: iq_col + (g + 1) * LANES].astype(BF16)
                 for g in range(IDX_HEADS * IDX_DIM // LANES)]
    iw = [sq[:, iw_col + h: iw_col + h + 1] for h in range(IDX_HEADS)]
    row = lax.broadcasted_iota(I32, (tq, tq), 0)
    col = lax.broadcasted_iota(I32, (tq, tq), 1)
    chunk_end = ((q0 + row) // CHUNK + 1) * CHUNK

    key_ref[...] = jnp.full((tq, seq), INT_MIN, I32)

    def score_tile(j, c):
        ikt = _rows(ik_ref, j, tq)
        ik_lo = jnp.where(lane_k < IDX_DIM, ikt, 0.0)
        ik_hi = pltpu.roll(ik_lo, IDX_DIM, 1)
        ik_lo = ik_lo.astype(BF16)
        ik_hi = ik_hi.astype(BF16)
        score = jnp.zeros((tq, tq), F32)
        for h in range(IDX_HEADS):
            s = _qk(iq_groups[h // 2], ik_hi if h % 2 else ik_lo)
            score = score + iw[h] * jnp.maximum(s, 0.0)
        vis = (j * tq + col) < chunk_end
        key_ref[:, pl.ds(pl.multiple_of(j * tq, tq), tq)] = jnp.where(vis, _sortable(score), INT_MIN)
        return c

    lax.fori_loop(0, i + 1, score_tile, 0)

    def count_ge(cand):
        return jnp.sum(jnp.where(key_ref[...] >= cand, 1.0, 0.0), axis=-1, keepdims=True)

    kf = jnp.float32(topk)
    thr = jnp.where(count_ge(jnp.zeros((tq, 1), I32)) >= kf, 0, INT_MIN).astype(I32)

    def bisect(b, thr):
        cand = thr | lax.shift_left(jnp.int32(1), 30 - b)
        return jnp.where(count_ge(cand) >= kf, cand, thr)

    thr = lax.fori_loop(0, 31, bisect, thr)
    thr = jnp.maximum(thr, INT_MIN + 1)
    n_ge = count_ge(thr)

    @pl.when(jnp.max(n_ge) > kf)
    def _():
        need = kf - jnp.sum(jnp.where(key_ref[...] > thr, 1.0, 0.0), axis=-1, keepdims=True)
        upto = jnp.where(row <= col, 1.0, 0.0).astype(BF16)

        def fix(j, seen):
            cols = pl.ds(pl.multiple_of(j * tq, tq), tq)
            kt = key_ref[:, cols]
            eq = kt == thr
            rank = jnp.dot(jnp.where(eq, 1.0, 0.0).astype(BF16), upto, preferred_element_type=F32) + seen
            key_ref[:, cols] = jnp.where(eq & (rank > need), INT_MIN, kt)
            return seen + jnp.sum(jnp.where(eq, 1.0, 0.0), axis=-1, keepdims=True)

        lax.fori_loop(0, n_tiles, fix, jnp.zeros((tq, 1), F32))

    qs = jnp.concatenate([q_ref[:, h * HEAD_DIM:(h + 1) * HEAD_DIM] for h in range(heads)], axis=0)

    def attn_tile(j, carry):
        kvt = _rows(kv_ref, j, tq)
        kt = kvt[:, :HEAD_DIM].astype(BF16)
        vt = kvt[:, HEAD_DIM:].astype(BF16)
        bias = jnp.where(key_ref[:, pl.ds(pl.multiple_of(j * tq, tq), tq)] >= thr, 0.0, NEG)
        s = _qk(qs, kt) * scale + jnp.concatenate([bias] * heads, axis=0)
        return _softmax_tile(carry, s, vt)

    _, l, acc = lax.fori_loop(0, i + 1, attn_tile, _softmax_init(heads * tq))
    o = acc / l
    for h in range(heads):
        o_ref[:, h * HEAD_DIM:(h + 1) * HEAD_DIM] = o[h * tq:(h + 1) * tq].astype(o_ref.dtype)


def _attn_sparse(dq, small, heads, batch, seq, tq, topk, iq_col, ik_col, iw_col):
    nq = seq // tq
    t = batch * seq
    bw = heads * HEAD_DIM
    w = small.shape[1]
    kern = functools.partial(_attn_sparse_kernel, tq=tq, seq=seq, heads=heads, topk=topk,
                             iq_col=iq_col, iw_col=iw_col)
    return pl.pallas_call(
        kern,
        grid=(batch, nq),
        in_specs=[pl.BlockSpec((tq, bw), lambda b, i: (b * nq + i, 0)),
                  pl.BlockSpec((tq, w), lambda b, i: (b * nq + i, 0)),
                  pl.BlockSpec((seq, 2 * HEAD_DIM), lambda b, i: (b, 0)),
                  pl.BlockSpec((seq, LANES), lambda b, i: (b, ik_col // LANES))],
        out_specs=pl.BlockSpec((tq, bw), lambda b, i: (b * nq + i, 0)),
        out_shape=jax.ShapeDtypeStruct((t, bw), BF16),
        scratch_shapes=[pltpu.VMEM((tq, seq), I32)],
        compiler_params=_params(2),
        name="attn_sparse",
    )(dq, small, small, small)


def _forget_cumsum_kernel(x_ref, b_ref, o_ref, *, seq, blk):
    r = lax.broadcasted_iota(I32, (blk, blk), 0)
    c = lax.broadcasted_iota(I32, (blk, blk), 1)
    tri = jnp.where(c <= r, 1.0, 0.0).astype(BF16)
    carry = jnp.zeros((1, LANES), F32)
    for s in range(seq // blk):
        logf = _log_sigmoid(x_ref[s * blk:(s + 1) * blk, :] + b_ref[...])
        hi = logf.astype(BF16)
        rem = logf - hi.astype(F32)
        mid = rem.astype(BF16)
        lo = (rem - mid.astype(F32)).astype(BF16)
        local = (jnp.dot(tri, hi, preferred_element_type=F32)
                 + jnp.dot(tri, mid, preferred_element_type=F32)
                 + jnp.dot(tri, lo, preferred_element_type=F32)) + carry
        o_ref[s * blk:(s + 1) * blk, :] = local
        carry = local[blk - 1:blk, :]


def _forget_cumsum(small, bias_row, batch, seq, col):
    return pl.pallas_call(
        functools.partial(_forget_cumsum_kernel, seq=seq, blk=_tile(seq, 256)),
        grid=(batch,),
        in_specs=[pl.BlockSpec((seq, LANES), lambda b: (b, col // LANES)),
                  pl.BlockSpec((1, LANES), lambda b: (0, 0))],
        out_specs=pl.BlockSpec((seq, LANES), lambda b: (b, 0)),
        out_shape=jax.ShapeDtypeStruct((batch * seq, LANES), F32),
        compiler_params=_params(1),
        name="forget_cumsum",
    )(small, bias_row)


def _merge_kernel(*refs, n_br, cast_rows):
    o_refs = refs[:n_br]
    g_refs = refs[n_br:2 * n_br]
    w_ref = refs[2 * n_br]
    out_ref = refs[2 * n_br + 1]
    wbf_ref = refs[2 * n_br + 2]

    @pl.when(pl.program_id(1) == 0)
    def _():
        for b in range(n_br):
            def body(r, c, b=b):
                rows = pl.ds(pl.multiple_of(r * cast_rows, cast_rows), cast_rows)
                wbf_ref[b, rows, :] = w_ref[b, rows, :].astype(BF16)
                return c
            lax.fori_loop(0, w_ref.shape[1] // cast_rows, body, 0)

    acc = None
    for b in range(n_br):
        term = g_refs[b][...] * jnp.dot(o_refs[b][...], wbf_ref[b], preferred_element_type=F32)
        acc = term if acc is None else acc + term
    out_ref[...] = acc.astype(out_ref.dtype)


def _merge(outs, gates, w_branch, layer, tm, tn):
    _, n_br, bw, d = w_branch.shape
    t = outs[0].shape[0]
    nd = d // tn
    kern = functools.partial(_merge_kernel, n_br=n_br, cast_rows=_tile(bw, 256))
    return pl.pallas_call(
        kern,
        grid=(nd, t // tm),
        in_specs=[*[pl.BlockSpec((tm, bw), lambda j, i: (i, 0)) for _ in range(n_br)],
                  *[pl.BlockSpec((tm, tn), lambda j, i, b=b: (i, b * nd + j)) for b in range(n_br)],
                  pl.BlockSpec((None, n_br, bw, tn), lambda j, i: (layer, 0, 0, j))],
        out_specs=pl.BlockSpec((tm, tn), lambda j, i: (i, j)),
        out_shape=jax.ShapeDtypeStruct((t, d), BF16),
        scratch_shapes=[pltpu.VMEM((n_br, bw, tn), BF16)],
        compiler_params=_params(2),
        name="gated_merge",
    )(*outs, *([gates] * n_br), w_branch)


def _router_kernel(h_ref, w_ref, b_ref, meta_i_ref, meta_w_ref, cnt_ref, run_ref, *, tm, n_exp):
    @pl.when(pl.program_id(0) == 0)
    def _():
        run_ref[...] = jnp.zeros_like(run_ref)

    lane = lax.broadcasted_iota(I32, (tm, LANES), 1)
    lane_f = lane.astype(F32)
    logits = jnp.dot(h_ref[...], w_ref[...].astype(BF16), preferred_element_type=F32) + b_ref[...]
    logits = jnp.where(lane < n_exp, logits, -jnp.inf)

    vals, idxs = [], []
    work = logits
    for _ in range(TOP_K):
        v = jnp.max(work, axis=-1, keepdims=True)
        ix = jnp.min(jnp.where(work == v, lane_f, float(LANES)), axis=-1, keepdims=True).astype(I32)
        vals.append(v)
        idxs.append(ix)
        work = jnp.where(lane == ix, -jnp.inf, work)
    es = [jnp.exp(v - vals[0]) for v in vals]
    denom = es[0] + es[1] + es[2] + es[3]

    onehots = [jnp.where(lane == ix, 1.0, 0.0) for ix in idxs]
    sel = onehots[0] + onehots[1] + onehots[2] + onehots[3]
    r = lax.broadcasted_iota(I32, (tm, tm), 0)
    c = lax.broadcasted_iota(I32, (tm, tm), 1)
    before = jnp.where(c < r, 1.0, 0.0).astype(BF16)
    rank = jnp.dot(before, sel.astype(BF16), preferred_element_type=F32) + run_ref[...]
    run_ref[...] = run_ref[...] + jnp.sum(sel, axis=0, keepdims=True)

    meta_i = jnp.zeros((tm, LANES), I32)
    meta_w = jnp.zeros((tm, LANES), F32)
    for k in range(TOP_K):
        rk = jnp.sum(onehots[k] * rank, axis=-1, keepdims=True).astype(I32)
        meta_i = jnp.where(lane == k, idxs[k], meta_i)
        meta_i = jnp.where(lane == TOP_K + k, rk, meta_i)
        meta_w = jnp.where(lane == k, es[k] / denom, meta_w)
    meta_i_ref[...] = meta_i
    meta_w_ref[...] = meta_w
    cnt_ref[...] = run_ref[...]


def _router(h, w_router_pad, b_router_pad, n_exp):
    t, d = h.shape
    tm = _tile(t, 256)
    kern = functools.partial(_router_kernel, tm=tm, n_exp=n_exp)
    return pl.pallas_call(
        kern,
        grid=(t // tm,),
        in_specs=[pl.BlockSpec((tm, d), lambda i: (i, 0)),
                  pl.BlockSpec((d, LANES), lambda i: (0, 0)),
                  pl.BlockSpec((1, LANES), lambda i: (0, 0))],
        out_specs=[pl.BlockSpec((tm, LANES), lambda i: (i, 0)),
                   pl.BlockSpec((tm, LANES), lambda i: (i, 0)),
                   pl.BlockSpec((1, LANES), lambda i: (0, 0))],
        out_shape=[jax.ShapeDtypeStruct((t, LANES), I32),
                   jax.ShapeDtypeStruct((t, LANES), F32),
                   jax.ShapeDtypeStruct((1, LANES), F32)],
        scratch_shapes=[pltpu.VMEM((1, LANES), F32)],
        compiler_params=_params(1),
        name="router",
    )(h, w_router_pad, b_router_pad)


def _gather_rows_kernel(idx_ref, src_ref, o_ref, sem, *, tm):
    base = pl.program_id(0) * tm

    def row_copy(r):
        return pltpu.make_async_copy(src_ref.at[pl.ds(idx_ref[base + r], 1), :],
                                     o_ref.at[pl.ds(r, 1), :], sem)

    def start(r, c):
        row_copy(r).start()
        return c

    def wait(r, c):
        row_copy(r).wait()
        return c

    lax.fori_loop(0, tm, start, 0)
    lax.fori_loop(0, tm, wait, 0)


def _gather_rows(src, idx, tm):
    n = idx.shape[0]
    d = src.shape[1]
    return pl.pallas_call(
        functools.partial(_gather_rows_kernel, tm=tm),
        grid_spec=pltpu.PrefetchScalarGridSpec(
            num_scalar_prefetch=1,
            grid=(n // tm,),
            in_specs=[pl.BlockSpec(memory_space=pl.ANY)],
            out_specs=pl.BlockSpec((tm, d), lambda i, idx: (i, 0)),
            scratch_shapes=[pltpu.SemaphoreType.DMA]),
        out_shape=jax.ShapeDtypeStruct((n, d), src.dtype),
        compiler_params=_params(1),
        name="gather_rows",
    )(idx, src)


def _ffn_kernel(te_ref, nu_ref, x_ref, wg_ref, wl_ref, bg_ref, bl_ref, wd_ref, bd_ref, o_ref):
    @pl.when(pl.program_id(0) < nu_ref[0])
    def _():
        x = x_ref[...].astype(BF16)
        glu = jnp.dot(x, wg_ref[...], preferred_element_type=F32) + bg_ref[...]
        lin = jnp.dot(x, wl_ref[...], preferred_element_type=F32) + bl_ref[...]
        glu = jnp.minimum(glu, SWIGLU_LIMIT)
        lin = jnp.clip(lin, -SWIGLU_LIMIT, SWIGLU_LIMIT)
        act = glu * jax.nn.sigmoid(SWIGLU_ALPHA * glu) * (lin + 1.0)
        o_ref[...] = jnp.dot(act.astype(BF16), wd_ref[...], preferred_element_type=F32) + bd_ref[...]

    @pl.when(pl.program_id(0) >= nu_ref[0])
    def _():
        o_ref[...] = jnp.zeros_like(o_ref)


def _expert_ffn(xs, tile_expert, n_used, w_g, w_l, b_g, b_l, w_dn, b_dn, tm):
    r, d = xs.shape
    ff = w_g.shape[2]
    n_tiles = r // tm

    def xrow(i, te, nu):
        return (jnp.minimum(i, nu[0] - 1), 0)

    def wsel(i, te, nu):
        return (te[i], 0, 0)

    return pl.pallas_call(
        _ffn_kernel,
        grid_spec=pltpu.PrefetchScalarGridSpec(
            num_scalar_prefetch=2,
            grid=(n_tiles,),
            in_specs=[pl.BlockSpec((tm, d), xrow),
                      pl.BlockSpec((None, d, ff), wsel),
                      pl.BlockSpec((None, d, ff), wsel),
                      pl.BlockSpec((None, 1, ff), wsel),
                      pl.BlockSpec((None, 1, ff), wsel),
                      pl.BlockSpec((None, ff, d), wsel),
                      pl.BlockSpec((None, 1, d), wsel)],
            out_specs=pl.BlockSpec((tm, d), lambda i, te, nu: (i, 0))),
        out_shape=jax.ShapeDtypeStruct((r, d), F32),
        compiler_params=_params(1),
        name="expert_ffn",
    )(tile_expert, n_used, xs, w_g, w_l, b_g, b_l, w_dn, b_dn)


def _combine_kernel(pos_ref, ys_ref, x_ref, g_ref, w_ref, o_ref, buf_ref, sem, *, tm):
    base = pl.program_id(0) * tm * TOP_K

    def row_copy(t, k):
        return pltpu.make_async_copy(ys_ref.at[pl.ds(pos_ref[base + t * TOP_K + k], 1), :],
                                     buf_ref.at[k, pl.ds(t, 1), :], sem)

    def start(t, c):
        for k in range(TOP_K):
            row_copy(t, k).start()
        return c

    def wait(t, c):
        for k in range(TOP_K):
            row_copy(t, k).wait()
        return c

    lax.fori_loop(0, tm, start, 0)
    lax.fori_loop(0, tm, wait, 0)
    w = w_ref[...]
    y = w[:, 0:1] * buf_ref[0]
    for k in range(1, TOP_K):
        y = y + w[:, k:k + 1] * buf_ref[k]
    o_ref[...] = x_ref[...] + g_ref[...] * y


def _combine(ys, pos_flat, x, m4, gate_idx, meta_w, seq, tm):
    t, d = x.shape
    per_b = seq // tm
    return pl.pallas_call(
        functools.partial(_combine_kernel, tm=tm),
        grid_spec=pltpu.PrefetchScalarGridSpec(
            num_scalar_prefetch=1,
            grid=(t // tm,),
            in_specs=[pl.BlockSpec(memory_space=pl.ANY),
                      pl.BlockSpec((tm, d), lambda i, p: (i, 0)),
                      pl.BlockSpec((None, None, 1, d), lambda i, p: (i // per_b, gate_idx, 0, 0)),
                      pl.BlockSpec((tm, LANES), lambda i, p: (i, 0))],
            out_specs=pl.BlockSpec((tm, d), lambda i, p: (i, 0)),
            scratch_shapes=[pltpu.VMEM((TOP_K, tm, d), F32), pltpu.SemaphoreType.DMA]),
        out_shape=jax.ShapeDtypeStruct((t, d), F32),
        compiler_params=_params(1),
        name="moe_combine",
    )(pos_flat, ys, x, m4, meta_w)


def _moe(h_bf, h_f32, x, m4, w_router, b_router, w_gu, b_gu, w_dn, b_dn, seq):
    t, d = x.shape
    n_exp = w_router.shape[1]
    tm = _tile(t, 256)
    wr = jnp.pad(w_router, ((0, 0), (0, LANES - n_exp)))
    br = jnp.pad(b_router, (0, LANES - n_exp))[None, :]
    meta_i, meta_w, counts = _router(h_bf, wr, br, n_exp)

    counts = counts[0, :n_exp].astype(I32)
    padded = ((counts + tm - 1) // tm) * tm
    ends = jnp.cumsum(padded)
    starts = ends - padded
    n_rows = t * TOP_K + n_exp * tm
    n_tiles = n_rows // tm
    experts = meta_i[:, :TOP_K]
    pos = starts[experts] + meta_i[:, TOP_K:2 * TOP_K]
    tile_expert = jnp.minimum(
        jnp.searchsorted(ends, jnp.arange(n_tiles, dtype=I32) * tm, side="right"), n_exp - 1).astype(I32)
    n_used = (ends[-1] // tm).astype(I32)[None]
    token_of_row = jnp.zeros((n_rows,), I32).at[pos.reshape(-1)].set(
        jnp.repeat(jnp.arange(t, dtype=I32), TOP_K))

    xs = _gather_rows(h_f32, token_of_row, tm)
    w_g = w_gu[:, :, 0::2].astype(BF16)
    w_l = w_gu[:, :, 1::2].astype(BF16)
    ys = _expert_ffn(xs, tile_expert, n_used, w_g, w_l, b_gu[:, None, 0::2], b_gu[:, None, 1::2],
                     w_dn.astype(BF16), b_dn[:, None, :], tm)
    return _combine(ys, pos.reshape(-1), x, m4, 5, meta_w, seq, _tile(seq, 64))


def _col_spec(k, tn, off_blocks=0):
    return pl.BlockSpec((k, tn), lambda j, i: (0, j + off_blocks))


def _layer_col_spec(layer, k, tn, off_blocks=0):
    return pl.BlockSpec((None, k, tn), lambda j, i: (layer, 0, j + off_blocks))


def _layer(x, m4, tabs, lam_init, batch, seq, layer, p):
    t, d = x.shape
    bw = p["w_branch"].shape[2]
    heads = bw // HEAD_DIM
    tm = _tile(seq, 1024)
    tn = _tile(bw, 512)
    w_in_all = p["w_in_all"]
    w_in = w_in_all[layer]
    zero = lambda n: jnp.zeros((1, n), F32)

    (h,) = _ada_norm(x, p["norm1_g"], m4, 1, 0, seq, (BF16,))

    tab8, tab16, tab_small = tabs
    tab_spec = lambda w: pl.BlockSpec((tm, w), lambda j, i: (i, 0))
    rope8 = functools.partial(_rope_epilogue, groups=((0, 8),) * (tn // LANES))
    rope16 = functools.partial(_rope_epilogue, groups=((0, 16),) * (tn // LANES))
    qk_a = _matmul(h, w_in_all, _layer_col_spec(layer, d, tn), d, 2 * bw, zero(2 * bw), tm=tm, tn=tn,
                   out_dtype=BF16, epilogue=rope8, extra=(tab8,), extra_specs=(tab_spec(3 * LANES),),
                   name="proj_a_qk")
    mid = _matmul(h, w_in_all, _layer_col_spec(layer, d, tn, 2 * bw // tn), d, 4 * bw, zero(4 * bw),
                  tm=tm, tn=tn, out_dtype=BF16, name="proj_av_b")
    c0 = 6 * bw + heads
    qkv_c = _matmul(h, w_in[:, c0:c0 + 3 * bw], _col_spec(d, tn), d, 3 * bw, zero(3 * bw), tm=tm, tn=tn,
                    out_dtype=BF16, name="proj_c")
    q_d = _matmul(h, w_in[:, c0 + 3 * bw:c0 + 4 * bw], _col_spec(d, tn), d, bw, zero(bw), tm=tm, tn=tn,
                  out_dtype=BF16, epilogue=rope16, extra=(tab16,), extra_specs=(tab_spec(3 * LANES),),
                  name="proj_d_q")
    s0 = c0 + 4 * bw
    n_iq = IDX_HEADS * IDX_DIM
    used = 2 * HEAD_DIM + n_iq + IDX_DIM + IDX_HEADS + heads
    ws = -(-used // LANES) * LANES
    w_small = jnp.concatenate(
        [w_in[:, s0:s0 + 2 * HEAD_DIM + n_iq + IDX_DIM + IDX_HEADS], w_in[:, 6 * bw:6 * bw + heads],
         jnp.zeros((d, ws - used), F32)], axis=1)
    ik_col = 2 * HEAD_DIM + n_iq
    iw_col = ik_col + IDX_DIM
    fl_col = iw_col + IDX_HEADS
    groups_small = ((0, 16), None) + ((1, 8),) * (n_iq // LANES) + ((2, 8),)
    tm_s = _tile(seq, 512)
    small = _matmul(h, w_small, pl.BlockSpec((d, ws), lambda j, i: (0, 0), pipeline_mode=pl.Buffered(1)),
                    d, ws, zero(ws), tm=tm_s, tn=ws, out_dtype=F32,
                    epilogue=functools.partial(_rope_epilogue, groups=groups_small), extra=(tab_small,),
                    extra_specs=(pl.BlockSpec((tm_s, 9 * LANES), lambda j, i: (i, 0)),), name="proj_small")

    fl_lane = fl_col % LANES
    bias_row = jnp.zeros((1, LANES), F32).at[0, fl_lane:fl_lane + heads].set(p["b_forget"])
    dcum = _forget_cumsum(small, bias_row, batch, seq, ik_col)[:, fl_lane:fl_lane + heads]
    dcum = dcum.reshape(batch, seq, heads).transpose(0, 2, 1)
    dq_col = dcum[..., None]
    dk_row = dcum[:, :, None, :]

    tq = _tile(seq, 256)
    nq = seq // tq
    hb = bw // HEAD_DIM
    o_a = _attention(
        functools.partial(_attn_diff_kernel, tq=tq, lam_init=lam_init),
        qk_a, 0, qk_a, hb, mid, 0, heads, batch, seq, tq,
        pre=(p["diff_lambda"], p["diff_norm_g"][None, :]),
        pre_specs=(pl.BlockSpec(p["diff_lambda"].shape, lambda b, h, i: (0, 0)),
                   pl.BlockSpec((1, HEAD_DIM), lambda b, h, i: (0, 0))),
        name="attn_diff")
    o_b = _attention(
        functools.partial(_attn_forget_kernel, tq=tq),
        mid, hb, mid, 2 * hb, mid, 3 * hb, heads, batch, seq, tq,
        pre=(dq_col, dk_row),
        pre_specs=(pl.BlockSpec((None, None, tq, 1), lambda b, h, i: (b, h, i, 0)),
                   pl.BlockSpec((None, None, 1, seq), lambda b, h, i: (b, h, 0, 0))),
        name="attn_forget")
    o_c = _attention(functools.partial(_attn_stick_kernel, tq=tq),
                     qkv_c, 0, qkv_c, hb, qkv_c, 2 * hb, heads, batch, seq, tq, name="attn_stick")
    o_d = _attn_sparse(q_d, small, heads, batch, seq, tq, min(TOPK_MAX, seq // 4),
                       2 * HEAD_DIM, ik_col, iw_col)

    n_br = p["w_gate"].shape[1]
    nd = d // tn
    gates = _matmul(h, p["w_gate"],
                    pl.BlockSpec((None, None, d, tn), lambda j, i: (layer, j // nd, 0, j % nd)),
                    d, n_br * d, p["b_gate"].reshape(1, n_br * d), tm=tm, tn=tn, out_dtype=F32,
                    epilogue=jax.nn.sigmoid, name="gates")
    merged = _merge((o_a, o_b, o_c, o_d), gates, p["w_branch"], layer, _tile(seq, 512), tn)
    per_b = seq // tm
    x = _matmul(merged, p["w_out"], _layer_col_spec(layer, d, tn), d, d, zero(d), tm=tm, tn=tn,
                out_dtype=F32, epilogue=lambda acc, xr, g: xr + g * acc, extra=(x, m4),
                extra_specs=(pl.BlockSpec((tm, tn), lambda j, i: (i, j)),
                             pl.BlockSpec((None, None, 1, tn), lambda j, i: (i // per_b, 2, 0, j))),
                name="out_proj")

    h_bf, h_f32 = _ada_norm(x, p["norm2_g"], m4, 4, 3, seq, (BF16, F32))
    return _moe(h_bf, h_f32, x, m4, p["w_router"], p["b_router"], p["w_gu"], p["b_gu"],
                p["w_dn"], p["b_dn"], seq)


@jax.jit
def _forward(x, c, positions, w_mod, b_mod, ada_table, norm1_g, w_in, b_forget, diff_lambda,
             diff_norm_g, w_branch, w_gate, b_gate, w_out, norm2_g, w_router, b_router,
             w_gu, b_gu, w_dn, b_dn, final_norm_g):
    batch, seq, d = x.shape
    depth = w_in.shape[0]
    t = batch * seq
    xt = x.reshape(t, d)

    c8 = jnp.pad(c, ((0, 8 - batch), (0, 0)))
    n_mod = w_mod.shape[1]
    mod = _matmul(c8, w_mod, _col_spec(d, _tile(n_mod, 512)), d, n_mod, b_mod[None, :], tm=8,
                  tn=_tile(n_mod, 512), out_dtype=F32, prologue=lambda v: v * jax.nn.sigmoid(v),
                  name="mod_proj")[:batch]
    mod = mod.reshape(batch, N_MOD, d)

    pos = positions.reshape(t).astype(F32)
    tabs = (_rope_table(pos, HEAD_DIM // 2, LANES),
            _rope_table(pos, HEAD_DIM, LANES),
            jnp.concatenate([_rope_table(pos, HEAD_DIM, LANES), _rope_table(pos, IDX_DIM, LANES),
                             _rope_table(pos, IDX_DIM, IDX_DIM)], axis=1))

    for l in range(depth):
        m4 = (mod + ada_table[l][None])[:, :, None, :]
        lam_init = 0.8 - 0.6 * float(np.exp(-0.3 * l))
        p = dict(norm1_g=norm1_g[l][None, :], w_in_all=w_in, b_forget=b_forget[l],
                 diff_lambda=diff_lambda[l], diff_norm_g=diff_norm_g[l], w_branch=w_branch,
                 w_gate=w_gate, b_gate=b_gate[l], w_out=w_out, norm2_g=norm2_g[l][None, :],
                 w_router=w_router[l], b_router=b_router[l], w_gu=w_gu[l], b_gu=b_gu[l],
                 w_dn=w_dn[l], b_dn=b_dn[l])
        xt = _layer(xt, m4, tabs, lam_init, batch, seq, l, p)

    zeros4 = jnp.zeros((batch, 1, 1, d), F32)
    (out,) = _ada_norm(xt, final_norm_g[None, :], zeros4, 0, 0, seq, (F32,))
    return out.reshape(batch, seq, d)


def kernel(x, c, positions, w_mod, b_mod, ada_table, norm1_g, w_in, b_forget, diff_lambda,
           diff_norm_g, w_branch, w_gate, b_gate, w_out, norm2_g, w_router, b_router,
           w_gu, b_gu, w_dn, b_dn, final_norm_g):
    return _forward(x, c, positions, w_mod, b_mod, ada_table, norm1_g, w_in, b_forget, diff_lambda,
                    diff_norm_g, w_branch, w_gate, b_gate, w_out, norm2_g, w_router, b_router,
                    w_gu, b_gu, w_dn, b_dn, final_norm_g)
```

```python
import functools

import numpy as np
import jax
import jax.numpy as jnp
from jax import lax
from jax.experimental import pallas as pl
from jax.experimental.pallas import tpu as pltpu

F32 = jnp.float32
BF16 = jnp.bfloat16
I32 = jnp.int32

LANES = 128
HEAD_DIM = 128
CHUNK = 64
IDX_HEADS = 8
IDX_DIM = 64
TOPK_MAX = 256
ROPE_THETA = 500000.0
ROPE_FRACTION = 4
TOP_K = 4
SWIGLU_LIMIT = 7.0
SWIGLU_ALPHA = 1.702
RMS_EPS = 1e-5
N_MOD = 6
NEG = -1e30
INT_MIN = -(2 ** 31)
VMEM_LIMIT = 56 * 1024 * 1024


def _params(n_axes, vmem=VMEM_LIMIT):
    return pltpu.CompilerParams(dimension_semantics=("arbitrary",) * n_axes,
                                vmem_limit_bytes=vmem)


def _tile(n, pref):
    t = min(n, pref)
    while n % t:
        t //= 2
    return t


def _store_token_rows(o_ref, val, tm):
    nseg = val.shape[1] // LANES
    for s in range(nseg):
        o_ref[pl.ds(s, tm, stride=nseg), :] = val[:, s * LANES:(s + 1) * LANES]


def _load_token_rows(ref, tm, nseg, lead=()):
    return [ref[(*lead, pl.ds(s, tm, stride=nseg), slice(None))] for s in range(nseg)]


def _norm_kernel(x_ref, g_ref, sc_ref, sh_ref, *o_refs, tm, token_rows):
    x = x_ref[...]
    var = jnp.mean(x * x, axis=-1, keepdims=True)
    y = x * lax.rsqrt(var + RMS_EPS) * g_ref[...]
    h = y * (1.0 + sc_ref[...]) + sh_ref[...]
    for o, rows in zip(o_refs, token_rows):
        if rows:
            _store_token_rows(o, h, tm)
        else:
            o[...] = h.astype(o.dtype)


def _ada_norm(x, g, m4, sc_idx, sh_idx, seq, out_dtypes, token_rows=None):
    t, d = x.shape
    tm = _tile(seq, 256)
    per_b = seq // tm
    nseg = d // LANES
    token_rows = token_rows or (False,) * len(out_dtypes)
    out_specs, out_shape = [], []
    for dt, rows in zip(out_dtypes, token_rows):
        if rows:
            out_specs.append(pl.BlockSpec((tm * nseg, LANES), lambda i: (i, 0)))
            out_shape.append(jax.ShapeDtypeStruct((t * nseg, LANES), F32))
        else:
            out_specs.append(pl.BlockSpec((tm, d), lambda i: (i, 0)))
            out_shape.append(jax.ShapeDtypeStruct((t, d), dt))
    return pl.pallas_call(
        functools.partial(_norm_kernel, tm=tm, token_rows=token_rows),
        grid=(t // tm,),
        in_specs=[
            pl.BlockSpec((tm, d), lambda i: (i, 0)),
            pl.BlockSpec((1, d), lambda i: (0, 0)),
            pl.BlockSpec((None, None, 1, d), lambda i: (i // per_b, sc_idx, 0, 0)),
            pl.BlockSpec((None, None, 1, d), lambda i: (i // per_b, sh_idx, 0, 0)),
        ],
        out_specs=out_specs,
        out_shape=out_shape,
        compiler_params=_params(1),
        name="ada_norm",
    )(x, g, m4, m4)


def _mm_kernel(*refs, n_extra, epilogue, prologue, cast_rows):
    x_ref, w_ref, b_ref = refs[:3]
    extra = refs[3:3 + n_extra]
    o_ref = refs[3 + n_extra]
    wbf_ref = refs[4 + n_extra]

    @pl.when(pl.program_id(1) == 0)
    def _():
        k = w_ref.shape[0]

        def body(r, c):
            rows = pl.ds(pl.multiple_of(r * cast_rows, cast_rows), cast_rows)
            wbf_ref[rows, :] = w_ref[rows, :].astype(BF16)
            return c

        lax.fori_loop(0, k // cast_rows, body, 0)

    x = x_ref[...]
    if prologue is not None:
        x = prologue(x)
    acc = jnp.dot(x.astype(BF16), wbf_ref[...], preferred_element_type=F32) + b_ref[...]
    o_ref[...] = epilogue(acc, *[e[...] for e in extra]).astype(o_ref.dtype)


def _matmul(x, w, w_spec, k, n, bias, *, tm, tn, out_dtype, epilogue=None, extra=(),
            extra_specs=(), prologue=None, name="matmul"):
    m = x.shape[0]
    if epilogue is None:
        epilogue = lambda acc: acc
    kern = functools.partial(_mm_kernel, n_extra=len(extra), epilogue=epilogue,
                             prologue=prologue, cast_rows=_tile(k, 256))
    return pl.pallas_call(
        kern,
        grid=(n // tn, m // tm),
        in_specs=[pl.BlockSpec((tm, k), lambda j, i: (i, 0)),
                  w_spec,
                  pl.BlockSpec((1, tn), lambda j, i: (0, j)),
                  *extra_specs],
        out_specs=pl.BlockSpec((tm, tn), lambda j, i: (i, j)),
        out_shape=jax.ShapeDtypeStruct((m, n), out_dtype),
        scratch_shapes=[pltpu.VMEM((k, tn), BF16)],
        compiler_params=_params(2),
        name=name,
    )(x, w, bias, *extra)


def _rope_epilogue(acc, tab, *, groups):
    outs = []
    for g, kind in enumerate(groups):
        xg = acc[:, g * LANES:(g + 1) * LANES]
        if kind is not None:
            slot, half = kind
            base = 3 * slot * LANES
            cos = tab[:, base:base + LANES]
            sin_lo = tab[:, base + LANES:base + 2 * LANES]
            sin_hi = tab[:, base + 2 * LANES:base + 3 * LANES]
            xg = (xg * cos + pltpu.roll(xg, half, 1) * sin_lo
                  + pltpu.roll(xg, LANES - half, 1) * sin_hi)
        outs.append(xg)
    return outs[0] if len(outs) == 1 else jnp.concatenate(outs, axis=1)


def _rope_table(pos, period, active):
    rot = period // ROPE_FRACTION
    half = rot // 2
    inv_freq = jnp.float32(ROPE_THETA) ** (-jnp.arange(half, dtype=F32) / half)
    lane = np.arange(LANES)
    r = lane % period
    on = lane < active
    first = on & (r < half)
    second = on & (r >= half) & (r < rot)
    fidx = np.where(first, r, np.where(second, r - half, 0))
    ang = pos[:, None] * inv_freq[fidx][None, :]
    cos, sin = jnp.cos(ang), jnp.sin(ang)
    c = jnp.where((first | second)[None, :], cos, 1.0)
    s_lo = jnp.where(second[None, :], sin, 0.0)
    s_hi = jnp.where(first[None, :], -sin, 0.0)
    return jnp.concatenate([c, s_lo, s_hi], axis=1)


def _qk(q, k):
    return lax.dot_general(q, k, (((1,), (1,)), ((), ())), preferred_element_type=F32)


def _softmax_tile(carry, s, v):
    m, l, acc = carry
    m_new = jnp.maximum(m, jnp.max(s, axis=-1, keepdims=True))
    alpha = jnp.exp(m - m_new)
    p = jnp.exp(s - m_new)
    l = alpha * l + jnp.sum(p, axis=-1, keepdims=True)
    acc = alpha * acc + jnp.dot(p.astype(BF16), v, preferred_element_type=F32)
    return m_new, l, acc


def _softmax_init(rows):
    return (jnp.full((rows, 1), NEG, F32), jnp.zeros((rows, 1), F32),
            jnp.zeros((rows, HEAD_DIM), F32))


def _rows(ref, j, t):
    return ref[pl.ds(pl.multiple_of(j * t, t), t), :]


def _attn_diff_kernel(dl_ref, g_ref, q_ref, k_ref, v_ref, o_ref, *, tq, lam_init):
    i = pl.program_id(2)
    half = HEAD_DIM // 2
    scale = half ** -0.5
    q = q_ref[...]
    lane = lax.broadcasted_iota(I32, q.shape, 1)
    zero = jnp.zeros_like(q)
    qz = jnp.concatenate([jnp.where(lane < half, q, zero), jnp.where(lane >= half, q, zero)], axis=0)

    def full_tile(j, carry):
        return _softmax_tile(carry, _qk(qz, _rows(k_ref, j, tq)) * scale, _rows(v_ref, j, tq))

    carry = lax.fori_loop(0, i, full_tile, _softmax_init(2 * tq))
    r = lax.broadcasted_iota(I32, (2 * tq, tq), 0)
    r = jnp.where(r >= tq, r - tq, r)
    c = lax.broadcasted_iota(I32, (2 * tq, tq), 1)
    vis = c < (r // CHUNK + 1) * CHUNK
    s = jnp.where(vis, _qk(qz, _rows(k_ref, i, tq)) * scale, NEG)
    _, l, acc = _softmax_tile(carry, s, _rows(v_ref, i, tq))

    dl = dl_ref[...]
    lam = (jnp.exp(jnp.sum(dl[0:1] * dl[1:2], axis=-1, keepdims=True))
           - jnp.exp(jnp.sum(dl[2:3] * dl[3:4], axis=-1, keepdims=True)) + lam_init)
    o = acc / l
    o = o[:tq] - lam * o[tq:]
    var = jnp.mean(o * o, axis=-1, keepdims=True)
    o_ref[...] = (o * lax.rsqrt(var + RMS_EPS) * g_ref[...] * (1.0 - lam_init)).astype(o_ref.dtype)


def _attn_forget_kernel(dq_ref, dk_ref, q_ref, k_ref, v_ref, o_ref, *, tq):
    i = pl.program_id(2)
    scale = HEAD_DIM ** -0.5
    q = q_ref[...]
    dq = dq_ref[...]

    def scores(j):
        dk = dk_ref[:, pl.ds(pl.multiple_of(j * tq, tq), tq)]
        return _qk(q, _rows(k_ref, j, tq)) * scale + dq - dk

    def full_tile(j, carry):
        return _softmax_tile(carry, scores(j), _rows(v_ref, j, tq))

    carry = lax.fori_loop(0, i, full_tile, _softmax_init(tq))
    r = lax.broadcasted_iota(I32, (tq, tq), 0)
    c = lax.broadcasted_iota(I32, (tq, tq), 1)
    s = jnp.where(c <= r, scores(i), NEG)
    _, l, acc = _softmax_tile(carry, s, _rows(v_ref, i, tq))
    o_ref[...] = (acc / l).astype(o_ref.dtype)


def _log_sigmoid(z):
    return jnp.minimum(z, 0.0) - jnp.log(1.0 + jnp.exp(-jnp.abs(z)))


def _attn_stick_kernel(q_ref, k_ref, v_ref, o_ref, *, tq):
    i = pl.program_id(2)
    scale = HEAD_DIM ** -0.5
    q = q_ref[...]
    r = lax.broadcasted_iota(I32, (tq, tq), 0)
    c = lax.broadcasted_iota(I32, (tq, tq), 1)
    cb = min(tq, 256)
    later = jnp.where(lax.broadcasted_iota(I32, (cb, cb), 0) > lax.broadcasted_iota(I32, (cb, cb), 1),
                      1.0, 0.0).astype(BF16)

    def tile(j, carry, mask):
        suffix, acc = carry
        z = _qk(q, _rows(k_ref, j, tq)) * scale
        log_beta = _log_sigmoid(z)
        log_keep = log_beta - z
        if mask is not None:
            log_keep = jnp.where(mask, log_keep, 0.0)
        parts = [None] * (tq // cb)
        for blk in reversed(range(tq // cb)):
            lk = log_keep[:, blk * cb:(blk + 1) * cb]
            hi = lk.astype(BF16)
            lo = (lk - hi.astype(F32)).astype(BF16)
            parts[blk] = (jnp.dot(hi, later, preferred_element_type=F32)
                          + jnp.dot(lo, later, preferred_element_type=F32) + suffix)
            suffix = suffix + jnp.sum(lk, axis=-1, keepdims=True)
        after = parts[0] if len(parts) == 1 else jnp.concatenate(parts, axis=1)
        w = jnp.exp(log_beta + after)
        if mask is not None:
            w = jnp.where(mask, w, 0.0)
        acc = acc + jnp.dot(w.astype(BF16), _rows(v_ref, j, tq), preferred_element_type=F32)
        return suffix, acc

    carry = tile(i, (jnp.zeros((tq, 1), F32), jnp.zeros((tq, HEAD_DIM), F32)), c < r)
    _, acc = lax.fori_loop(0, i, lambda jj, cr: tile(i - 1 - jj, cr, None), carry)
    o_ref[...] = acc.astype(o_ref.dtype)


def _attention(kernel, q_arr, q_col, k_arr, k_col, v_arr, v_col, heads, batch, seq, tq,
               pre=(), pre_specs=(), name="attention"):
    nq = seq // tq
    t = batch * seq
    return pl.pallas_call(
        kernel,
        grid=(batch, heads, nq),
        in_specs=[*pre_specs,
                  pl.BlockSpec((tq, HEAD_DIM), lambda b, h, i: (b * nq + i, q_col + h)),
                  pl.BlockSpec((seq, HEAD_DIM), lambda b, h, i: (b, k_col + h)),
                  pl.BlockSpec((seq, HEAD_DIM), lambda b, h, i: (b, v_col + h))],
        out_specs=pl.BlockSpec((tq, HEAD_DIM), lambda b, h, i: (b * nq + i, h)),
        out_shape=jax.ShapeDtypeStruct((t, heads * HEAD_DIM), BF16),
        compiler_params=_params(3),
        name=name,
    )(*pre, q_arr, k_arr, v_arr)


def _sortable(x):
    bits = pltpu.bitcast(x + 0.0, I32)
    return jnp.where(bits < 0, bits ^ jnp.int32(0x7FFFFFFF), bits)


def _attn_sparse_kernel(q_ref, sq_ref, kv_ref, ik_ref, o_ref, key_ref, *, tq, seq, heads, topk,
                        iq_col, iw_col):
    i = pl.program_id(1)
    scale = HEAD_DIM ** -0.5
    n_tiles = seq // tq
    q0 = i * tq

    sq = sq_ref[...]
    lane_k = lax.broadcasted_iota(I32, (tq, LANES), 1)
    iq_groups = [sq[:, iq_col + g * LANES: iq_col + (g + 1) * LANES].astype(BF16)
                 for g in range(IDX_HEADS * IDX_DIM // LANES)]
    iw = [sq[:, iw_col + h: iw_col + h + 1] for h in range(IDX_HEADS)]
    row = lax.broadcasted_iota(I32, (tq, tq), 0)
    col = lax.broadcasted_iota(I32, (tq, tq), 1)
    chunk_end = ((q0 + row) // CHUNK + 1) * CHUNK

    key_ref[...] = jnp.full((tq, seq), INT_MIN, I32)

    def score_tile(j, c):
        ikt = _rows(ik_ref, j, tq)
        ik_lo = jnp.where(lane_k < IDX_DIM, ikt, 0.0)
        ik_hi = pltpu.roll(ik_lo, IDX_DIM, 1)
        ik_lo = ik_lo.astype(BF16)
        ik_hi = ik_hi.astype(BF16)
        score = jnp.zeros((tq, tq), F32)
        for h in range(IDX_HEADS):
            s = _qk(iq_groups[h // 2], ik_hi if h % 2 else ik_lo)
            score = score + iw[h] * jnp.maximum(s, 0.0)
        vis = (j * tq + col) < chunk_end
        key_ref[:, pl.ds(pl.multiple_of(j * tq, tq), tq)] = jnp.where(vis, _sortable(score), INT_MIN)
        return c

    lax.fori_loop(0, i + 1, score_tile, 0)

    def count_ge(cand):
        return jnp.sum(jnp.where(key_ref[...] >= cand, 1.0, 0.0), axis=-1, keepdims=True)

    kf = jnp.float32(topk)
    thr = jnp.where(count_ge(jnp.zeros((tq, 1), I32)) >= kf, 0, INT_MIN).astype(I32)

    def bisect(b, thr):
        cand = thr | lax.shift_left(jnp.int32(1), 30 - b)
        return jnp.where(count_ge(cand) >= kf, cand, thr)

    thr = lax.fori_loop(0, jnp.where((i + 1) * tq <= topk, 0, 31), bisect, thr)
    thr = jnp.maximum(thr, INT_MIN + 1)
    n_ge = count_ge(thr)

    @pl.when(jnp.max(n_ge) > kf)
    def _():
        need = kf - jnp.sum(jnp.where(key_ref[...] > thr, 1.0, 0.0), axis=-1, keepdims=True)
        upto = jnp.where(row <= col, 1.0, 0.0).astype(BF16)

        def fix(j, seen):
            cols = pl.ds(pl.multiple_of(j * tq, tq), tq)
            kt = key_ref[:, cols]
            eq = kt == thr
            rank = jnp.dot(jnp.where(eq, 1.0, 0.0).astype(BF16), upto, preferred_element_type=F32) + seen
            key_ref[:, cols] = jnp.where(eq & (rank > need), INT_MIN, kt)
            return seen + jnp.sum(jnp.where(eq, 1.0, 0.0), axis=-1, keepdims=True)

        lax.fori_loop(0, n_tiles, fix, jnp.zeros((tq, 1), F32))

    qs = jnp.concatenate([q_ref[:, h * HEAD_DIM:(h + 1) * HEAD_DIM] for h in range(heads)], axis=0)

    def attn_tile(j, carry):
        kvt = _rows(kv_ref, j, tq)
        kt = kvt[:, :HEAD_DIM].astype(BF16)
        vt = kvt[:, HEAD_DIM:].astype(BF16)
        bias = jnp.where(key_ref[:, pl.ds(pl.multiple_of(j * tq, tq), tq)] >= thr, 0.0, NEG)
        s = _qk(qs, kt) * scale + jnp.concatenate([bias] * heads, axis=0)
        return _softmax_tile(carry, s, vt)

    _, l, acc = lax.fori_loop(0, i + 1, attn_tile, _softmax_init(heads * tq))
    o = acc / l
    for h in range(heads):
        o_ref[:, h * HEAD_DIM:(h + 1) * HEAD_DIM] = o[h * tq:(h + 1) * tq].astype(o_ref.dtype)


def _attn_sparse(dq, small, heads, batch, seq, tq, topk, iq_col, ik_col, iw_col):
    nq = seq // tq
    t = batch * seq
    bw = heads * HEAD_DIM
    w = small.shape[1]
    kern = functools.partial(_attn_sparse_kernel, tq=tq, seq=seq, heads=heads, topk=topk,
                             iq_col=iq_col, iw_col=iw_col)
    return pl.pallas_call(
        kern,
        grid=(batch, nq),
        in_specs=[pl.BlockSpec((tq, bw), lambda b, i: (b * nq + i, 0)),
                  pl.BlockSpec((tq, w), lambda b, i: (b * nq + i, 0)),
                  pl.BlockSpec((seq, 2 * HEAD_DIM), lambda b, i: (b, 0)),
                  pl.BlockSpec((seq, LANES), lambda b, i: (b, ik_col // LANES))],
        out_specs=pl.BlockSpec((tq, bw), lambda b, i: (b * nq + i, 0)),
        out_shape=jax.ShapeDtypeStruct((t, bw), BF16),
        scratch_shapes=[pltpu.VMEM((tq, seq), I32)],
        compiler_params=_params(2),
        name="attn_sparse",
    )(dq, small, small, small)


def _forget_cumsum_kernel(x_ref, b_ref, o_ref, *, seq, blk):
    r = lax.broadcasted_iota(I32, (blk, blk), 0)
    c = lax.broadcasted_iota(I32, (blk, blk), 1)
    tri = jnp.where(c <= r, 1.0, 0.0).astype(BF16)
    carry = jnp.zeros((1, LANES), F32)
    for s in range(seq // blk):
        logf = _log_sigmoid(x_ref[s * blk:(s + 1) * blk, :] + b_ref[...])
        hi = logf.astype(BF16)
        rem = logf - hi.astype(F32)
        mid = rem.astype(BF16)
        lo = (rem - mid.astype(F32)).astype(BF16)
        local = (jnp.dot(tri, hi, preferred_element_type=F32)
                 + jnp.dot(tri, mid, preferred_element_type=F32)
                 + jnp.dot(tri, lo, preferred_element_type=F32)) + carry
        o_ref[s * blk:(s + 1) * blk, :] = local
        carry = local[blk - 1:blk, :]


def _forget_cumsum(small, bias_row, batch, seq, col):
    return pl.pallas_call(
        functools.partial(_forget_cumsum_kernel, seq=seq, blk=_tile(seq, 256)),
        grid=(batch,),
        in_specs=[pl.BlockSpec((seq, LANES), lambda b: (b, col // LANES)),
                  pl.BlockSpec((1, LANES), lambda b: (0, 0))],
        out_specs=pl.BlockSpec((seq, LANES), lambda b: (b, 0)),
        out_shape=jax.ShapeDtypeStruct((batch * seq, LANES), F32),
        compiler_params=_params(1),
        name="forget_cumsum",
    )(small, bias_row)


def _merge_kernel(*refs, n_br, cast_rows):
    o_refs = refs[:n_br]
    g_refs = refs[n_br:2 * n_br]
    w_ref = refs[2 * n_br]
    out_ref = refs[2 * n_br + 1]
    wbf_ref = refs[2 * n_br + 2]

    @pl.when(pl.program_id(1) == 0)
    def _():
        for b in range(n_br):
            def body(r, c, b=b):
                rows = pl.ds(pl.multiple_of(r * cast_rows, cast_rows), cast_rows)
                wbf_ref[b, rows, :] = w_ref[b, rows, :].astype(BF16)
                return c
            lax.fori_loop(0, w_ref.shape[1] // cast_rows, body, 0)

    acc = None
    for b in range(n_br):
        term = g_refs[b][...] * jnp.dot(o_refs[b][...], wbf_ref[b], preferred_element_type=F32)
        acc = term if acc is None else acc + term
    out_ref[...] = acc.astype(out_ref.dtype)


def _merge(outs, gates, w_branch, layer, tm, tn):
    _, n_br, bw, d = w_branch.shape
    t = outs[0].shape[0]
    nd = d // tn
    kern = functools.partial(_merge_kernel, n_br=n_br, cast_rows=_tile(bw, 256))
    return pl.pallas_call(
        kern,
        grid=(nd, t // tm),
        in_specs=[*[pl.BlockSpec((tm, bw), lambda j, i: (i, 0)) for _ in range(n_br)],
                  *[pl.BlockSpec((tm, tn), lambda j, i, b=b: (i, b * nd + j)) for b in range(n_br)],
                  pl.BlockSpec((None, n_br, bw, tn), lambda j, i: (layer, 0, 0, j))],
        out_specs=pl.BlockSpec((tm, tn), lambda j, i: (i, j)),
        out_shape=jax.ShapeDtypeStruct((t, d), BF16),
        scratch_shapes=[pltpu.VMEM((n_br, bw, tn), BF16)],
        compiler_params=_params(2),
        name="gated_merge",
    )(*outs, *([gates] * n_br), w_branch)


def _router_kernel(h_ref, w_ref, b_ref, meta_i_ref, meta_w_ref, cnt_ref, run_ref, *, tm, n_exp):
    @pl.when(pl.program_id(0) == 0)
    def _():
        run_ref[...] = jnp.zeros_like(run_ref)

    lane = lax.broadcasted_iota(I32, (tm, LANES), 1)
    lane_f = lane.astype(F32)
    logits = jnp.dot(h_ref[...], w_ref[...].astype(BF16), preferred_element_type=F32) + b_ref[...]
    logits = jnp.where(lane < n_exp, logits, -jnp.inf)

    vals, idxs = [], []
    work = logits
    for _ in range(TOP_K):
        v = jnp.max(work, axis=-1, keepdims=True)
        ix = jnp.min(jnp.where(work == v, lane_f, float(LANES)), axis=-1, keepdims=True).astype(I32)
        vals.append(v)
        idxs.append(ix)
        work = jnp.where(lane == ix, -jnp.inf, work)
    es = [jnp.exp(v - vals[0]) for v in vals]
    denom = es[0] + es[1] + es[2] + es[3]

    onehots = [jnp.where(lane == ix, 1.0, 0.0) for ix in idxs]
    sel = onehots[0] + onehots[1] + onehots[2] + onehots[3]
    r = lax.broadcasted_iota(I32, (tm, tm), 0)
    c = lax.broadcasted_iota(I32, (tm, tm), 1)
    before = jnp.where(c < r, 1.0, 0.0).astype(BF16)
    rank = jnp.dot(before, sel.astype(BF16), preferred_element_type=F32) + run_ref[...]
    run_ref[...] = run_ref[...] + jnp.sum(sel, axis=0, keepdims=True)

    meta_i = jnp.zeros((tm, LANES), I32)
    meta_w = jnp.zeros((tm, LANES), F32)
    for k in range(TOP_K):
        rk = jnp.sum(onehots[k] * rank, axis=-1, keepdims=True).astype(I32)
        meta_i = jnp.where(lane == k, idxs[k], meta_i)
        meta_i = jnp.where(lane == TOP_K + k, rk, meta_i)
        meta_w = jnp.where(lane == k, es[k] / denom, meta_w)
    meta_i_ref[...] = meta_i
    meta_w_ref[...] = meta_w
    cnt_ref[...] = run_ref[...]


def _router(h, w_router_pad, b_router_pad, n_exp):
    t, d = h.shape
    tm = _tile(t, 256)
    kern = functools.partial(_router_kernel, tm=tm, n_exp=n_exp)
    return pl.pallas_call(
        kern,
        grid=(t // tm,),
        in_specs=[pl.BlockSpec((tm, d), lambda i: (i, 0)),
                  pl.BlockSpec((d, LANES), lambda i: (0, 0)),
                  pl.BlockSpec((1, LANES), lambda i: (0, 0))],
        out_specs=[pl.BlockSpec((tm, LANES), lambda i: (i, 0)),
                   pl.BlockSpec((tm, LANES), lambda i: (i, 0)),
                   pl.BlockSpec((1, LANES), lambda i: (0, 0))],
        out_shape=[jax.ShapeDtypeStruct((t, LANES), I32),
                   jax.ShapeDtypeStruct((t, LANES), F32),
                   jax.ShapeDtypeStruct((1, LANES), F32)],
        scratch_shapes=[pltpu.VMEM((1, LANES), F32)],
        compiler_params=_params(1),
        name="router",
    )(h, w_router_pad, b_router_pad)


def _ffn_kernel(te_ref, nu_ref, tok_ref, h_ref, wgu_ref, bgu_ref, wdn_ref, bdn_ref, o_ref,
                xbuf, pick_ref, sem, *, tm, nseg, kc, nc):
    i = pl.program_id(0)
    n_used = nu_ref[0]
    d, ff = wgu_ref.shape[0], wdn_ref.shape[0]

    def fetch(tile, slot):
        def body(r, c):
            tok = tok_ref[tile * tm + r]
            pltpu.make_async_copy(h_ref.at[pl.ds(tok * nseg, nseg), :],
                                  xbuf.at[slot, pl.ds(r * nseg, nseg), :], sem.at[slot]).start()
            return c
        lax.fori_loop(0, tm, body, 0)

    @pl.when(i == 0)
    def _():
        r = lax.broadcasted_iota(I32, (2 * ff, ff), 0)
        c = lax.broadcasted_iota(I32, (2 * ff, ff), 1)
        pick_ref[...] = jnp.where(r == 2 * c, 1.0, 0.0).astype(BF16)
        fetch(0, 0)

    @pl.when(i + 1 < n_used)
    def _():
        fetch(i + 1, (i + 1) % 2)

    @pl.when(i < n_used)
    def _():
        slot = i % 2
        pltpu.make_async_copy(xbuf.at[slot], xbuf.at[slot], sem.at[slot]).wait()
        parts = _load_token_rows(xbuf, tm, nseg, lead=(slot,))
        per = kc // LANES
        gu = bgu_ref[...]
        for c in range(d // kc):
            xc = jnp.concatenate(parts[c * per:(c + 1) * per], axis=1).astype(BF16)
            gu = gu + jnp.dot(xc, wgu_ref[c * kc:(c + 1) * kc, :].astype(BF16),
                              preferred_element_type=F32)
        lin = pltpu.roll(gu, 2 * ff - 1, 1)
        glu = jnp.minimum(gu, SWIGLU_LIMIT)
        lin = jnp.clip(lin, -SWIGLU_LIMIT, SWIGLU_LIMIT)
        act = glu * jax.nn.sigmoid(SWIGLU_ALPHA * glu) * (lin + 1.0)
        act = jnp.dot(act.astype(BF16), pick_ref[...], preferred_element_type=F32).astype(BF16)
        for c in range(d // nc):
            cols = slice(c * nc, (c + 1) * nc)
            o_ref[:, cols] = (jnp.dot(act, wdn_ref[:, cols].astype(BF16), preferred_element_type=F32)
                              + bdn_ref[:, cols])

    @pl.when(i >= n_used)
    def _():
        o_ref[...] = jnp.zeros_like(o_ref)


def _expert_ffn(h_rows, tile_expert, n_used, token_of_row, w_gu, b_gu, w_dn, b_dn, layer, tm):
    _, n_exp, d, ff2 = w_gu.shape
    ff = ff2 // 2
    nseg = d // LANES
    n_tiles = token_of_row.shape[0] // tm
    once = pl.Buffered(1)

    def wsel(i, te, nu, tok):
        return (layer, te[i], 0, 0)

    kern = functools.partial(_ffn_kernel, tm=tm, nseg=nseg, kc=_tile(d, 512), nc=_tile(d, 1024))
    return pl.pallas_call(
        kern,
        grid_spec=pltpu.PrefetchScalarGridSpec(
            num_scalar_prefetch=3,
            grid=(n_tiles,),
            in_specs=[pl.BlockSpec(memory_space=pl.ANY),
                      pl.BlockSpec((None, None, d, ff2), wsel, pipeline_mode=once),
                      pl.BlockSpec((None, None, 1, ff2), wsel),
                      pl.BlockSpec((None, None, ff, d), wsel, pipeline_mode=once),
                      pl.BlockSpec((None, None, 1, d), wsel)],
            out_specs=pl.BlockSpec((tm, d), lambda i, te, nu, tok: (i, 0)),
            scratch_shapes=[pltpu.VMEM((2, tm * nseg, LANES), F32),
                            pltpu.VMEM((ff2, ff), BF16),
                            pltpu.SemaphoreType.DMA((2,))]),
        out_shape=jax.ShapeDtypeStruct((n_tiles * tm, d), F32),
        compiler_params=_params(1),
        name="expert_ffn",
    )(tile_expert, n_used, token_of_row, h_rows, w_gu, b_gu, w_dn, b_dn)


def _combine_kernel(pos_ref, ys_ref, x_ref, g_ref, w_ref, o_ref, buf, sem, *, tm):
    i = pl.program_id(0)

    def fetch(tile, slot):
        def body(t, c):
            for k in range(TOP_K):
                row = pos_ref[(tile * tm + t) * TOP_K + k]
                pltpu.make_async_copy(ys_ref.at[pl.ds(row, 1), :],
                                      buf.at[slot, k, pl.ds(t, 1), :], sem.at[slot]).start()
            return c
        lax.fori_loop(0, tm, body, 0)

    @pl.when(i == 0)
    def _():
        fetch(0, 0)

    @pl.when(i + 1 < pl.num_programs(0))
    def _():
        fetch(i + 1, (i + 1) % 2)

    slot = i % 2
    pltpu.make_async_copy(buf.at[slot], buf.at[slot], sem.at[slot]).wait()
    w = w_ref[...]
    y = w[:, 0:1] * buf[slot, 0]
    for k in range(1, TOP_K):
        y = y + w[:, k:k + 1] * buf[slot, k]
    o_ref[...] = x_ref[...] + g_ref[...] * y


def _combine(ys, pos_flat, x, m4, gate_idx, meta_w, seq, tm):
    t, d = x.shape
    per_b = seq // tm
    return pl.pallas_call(
        functools.partial(_combine_kernel, tm=tm),
        grid_spec=pltpu.PrefetchScalarGridSpec(
            num_scalar_prefetch=1,
            grid=(t // tm,),
            in_specs=[pl.BlockSpec(memory_space=pl.ANY),
                      pl.BlockSpec((tm, d), lambda i, p: (i, 0)),
                      pl.BlockSpec((None, None, 1, d), lambda i, p: (i // per_b, gate_idx, 0, 0)),
                      pl.BlockSpec((tm, LANES), lambda i, p: (i, 0))],
            out_specs=pl.BlockSpec((tm, d), lambda i, p: (i, 0)),
            scratch_shapes=[pltpu.VMEM((2, TOP_K, tm, d), F32),
                            pltpu.SemaphoreType.DMA((2,))]),
        out_shape=jax.ShapeDtypeStruct((t, d), F32),
        compiler_params=_params(1),
        name="moe_combine",
    )(pos_flat, ys, x, m4, meta_w)


def _moe(h_bf, h_rows, x, m4, w_router, b_router, w_gu, b_gu, w_dn, b_dn, layer, seq):
    t, d = x.shape
    n_exp = w_router.shape[1]
    tm = _tile(t, 256)
    wr = jnp.pad(w_router, ((0, 0), (0, LANES - n_exp)))
    br = jnp.pad(b_router, (0, LANES - n_exp))[None, :]
    meta_i, meta_w, counts = _router(h_bf, wr, br, n_exp)

    counts = counts[0, :n_exp].astype(I32)
    padded = ((counts + tm - 1) // tm) * tm
    ends = jnp.cumsum(padded)
    starts = ends - padded
    n_rows = t * TOP_K + n_exp * tm
    n_tiles = n_rows // tm
    experts = meta_i[:, :TOP_K]
    pos = starts[experts] + meta_i[:, TOP_K:2 * TOP_K]
    tile_start = jnp.arange(n_tiles, dtype=I32) * tm
    tile_expert = jnp.minimum(jnp.sum((ends[None, :] <= tile_start[:, None]).astype(I32), axis=1),
                              n_exp - 1)
    n_used = (ends[-1] // tm).astype(I32)[None]
    token_of_row = jnp.zeros((n_rows,), I32).at[pos.reshape(-1)].set(
        jnp.repeat(jnp.arange(t, dtype=I32), TOP_K))

    ys = _expert_ffn(h_rows, tile_expert, n_used, token_of_row, w_gu, b_gu[:, :, None, :], w_dn,
                     b_dn[:, :, None, :], layer, tm)
    return _combine(ys, pos.reshape(-1), x, m4, 5, meta_w, seq, _tile(seq, 64))


def _col_spec(k, tn, off_blocks=0):
    return pl.BlockSpec((k, tn), lambda j, i: (0, j + off_blocks))


def _layer_col_spec(layer, k, tn, off_blocks=0):
    return pl.BlockSpec((None, k, tn), lambda j, i: (layer, 0, j + off_blocks))


def _layer(x, m4, tabs, lam_init, batch, seq, layer, p):
    t, d = x.shape
    bw = p["w_branch"].shape[2]
    heads = bw // HEAD_DIM
    tm = _tile(seq, 1024)
    tn = _tile(bw, 512)
    w_in_all = p["w_in_all"]
    w_in = w_in_all[layer]
    zero = lambda n: jnp.zeros((1, n), F32)

    (h,) = _ada_norm(x, p["norm1_g"], m4, 1, 0, seq, (BF16,))

    tab8, tab16, tab_small = tabs
    tab_spec = lambda w: pl.BlockSpec((tm, w), lambda j, i: (i, 0))
    rope8 = functools.partial(_rope_epilogue, groups=((0, 8),) * (tn // LANES))
    rope16 = functools.partial(_rope_epilogue, groups=((0, 16),) * (tn // LANES))
    qk_a = _matmul(h, w_in_all, _layer_col_spec(layer, d, tn), d, 2 * bw, zero(2 * bw), tm=tm, tn=tn,
                   out_dtype=BF16, epilogue=rope8, extra=(tab8,), extra_specs=(tab_spec(3 * LANES),),
                   name="proj_a_qk")
    mid = _matmul(h, w_in_all, _layer_col_spec(layer, d, tn, 2 * bw // tn), d, 4 * bw, zero(4 * bw),
                  tm=tm, tn=tn, out_dtype=BF16, name="proj_av_b")
    c0 = 6 * bw + heads
    qkv_c = _matmul(h, w_in[:, c0:c0 + 3 * bw], _col_spec(d, tn), d, 3 * bw, zero(3 * bw), tm=tm, tn=tn,
                    out_dtype=BF16, name="proj_c")
    q_d = _matmul(h, w_in[:, c0 + 3 * bw:c0 + 4 * bw], _col_spec(d, tn), d, bw, zero(bw), tm=tm, tn=tn,
                  out_dtype=BF16, epilogue=rope16, extra=(tab16,), extra_specs=(tab_spec(3 * LANES),),
                  name="proj_d_q")
    s0 = c0 + 4 * bw
    n_iq = IDX_HEADS * IDX_DIM
    used = 2 * HEAD_DIM + n_iq + IDX_DIM + IDX_HEADS + heads
    ws = -(-used // LANES) * LANES
    w_small = jnp.concatenate(
        [w_in[:, s0:s0 + 2 * HEAD_DIM + n_iq + IDX_DIM + IDX_HEADS], w_in[:, 6 * bw:6 * bw + heads],
         jnp.zeros((d, ws - used), F32)], axis=1)
    ik_col = 2 * HEAD_DIM + n_iq
    iw_col = ik_col + IDX_DIM
    fl_col = iw_col + IDX_HEADS
    groups_small = ((0, 16), None) + ((1, 8),) * (n_iq // LANES) + ((2, 8),)
    tm_s = _tile(seq, 512)
    small = _matmul(h, w_small, pl.BlockSpec((d, ws), lambda j, i: (0, 0), pipeline_mode=pl.Buffered(1)),
                    d, ws, zero(ws), tm=tm_s, tn=ws, out_dtype=F32,
                    epilogue=functools.partial(_rope_epilogue, groups=groups_small), extra=(tab_small,),
                    extra_specs=(pl.BlockSpec((tm_s, 9 * LANES), lambda j, i: (i, 0)),), name="proj_small")

    fl_lane = fl_col % LANES
    bias_row = jnp.zeros((1, LANES), F32).at[0, fl_lane:fl_lane + heads].set(p["b_forget"])
    dcum = _forget_cumsum(small, bias_row, batch, seq, ik_col)[:, fl_lane:fl_lane + heads]
    dcum = dcum.reshape(batch, seq, heads).transpose(0, 2, 1)
    dq_col = dcum[..., None]
    dk_row = dcum[:, :, None, :]

    tq = _tile(seq, 512)
    hb = bw // HEAD_DIM
    o_a = _attention(
        functools.partial(_attn_diff_kernel, tq=tq, lam_init=lam_init),
        qk_a, 0, qk_a, hb, mid, 0, heads, batch, seq, tq,
        pre=(p["diff_lambda"], p["diff_norm_g"][None, :]),
        pre_specs=(pl.BlockSpec(p["diff_lambda"].shape, lambda b, h, i: (0, 0)),
                   pl.BlockSpec((1, HEAD_DIM), lambda b, h, i: (0, 0))),
        name="attn_diff")
    o_b = _attention(
        functools.partial(_attn_forget_kernel, tq=tq),
        mid, hb, mid, 2 * hb, mid, 3 * hb, heads, batch, seq, tq,
        pre=(dq_col, dk_row),
        pre_specs=(pl.BlockSpec((None, None, tq, 1), lambda b, h, i: (b, h, i, 0)),
                   pl.BlockSpec((None, None, 1, seq), lambda b, h, i: (b, h, 0, 0))),
        name="attn_forget")
    o_c = _attention(functools.partial(_attn_stick_kernel, tq=tq),
                     qkv_c, 0, qkv_c, hb, qkv_c, 2 * hb, heads, batch, seq, tq, name="attn_stick")
    o_d = _attn_sparse(q_d, small, heads, batch, seq, _tile(seq, 256), min(TOPK_MAX, seq // 4),
                       2 * HEAD_DIM, ik_col, iw_col)

    n_br = p["w_gate"].shape[1]
    nd = d // tn
    gates = _matmul(h, p["w_gate"],
                    pl.BlockSpec((None, None, d, tn), lambda j, i: (layer, j // nd, 0, j % nd)),
                    d, n_br * d, p["b_gate"].reshape(1, n_br * d), tm=tm, tn=tn, out_dtype=F32,
                    epilogue=jax.nn.sigmoid, name="gates")
    merged = _merge((o_a, o_b, o_c, o_d), gates, p["w_branch"], layer, _tile(seq, 512), tn)
    per_b = seq // tm
    x = _matmul(merged, p["w_out"], _layer_col_spec(layer, d, tn), d, d, zero(d), tm=tm, tn=tn,
                out_dtype=F32, epilogue=lambda acc, xr, g: xr + g * acc, extra=(x, m4),
                extra_specs=(pl.BlockSpec((tm, tn), lambda j, i: (i, j)),
                             pl.BlockSpec((None, None, 1, tn), lambda j, i: (i // per_b, 2, 0, j))),
                name="out_proj")

    h_bf, h_rows = _ada_norm(x, p["norm2_g"], m4, 4, 3, seq, (BF16, F32), token_rows=(False, True))
    return _moe(h_bf, h_rows, x, m4, p["w_router"], p["b_router"], p["w_gu"], p["b_gu"],
                p["w_dn"], p["b_dn"], layer, seq)


@jax.jit
def _forward(x, c, positions, w_mod, b_mod, ada_table, norm1_g, w_in, b_forget, diff_lambda,
             diff_norm_g, w_branch, w_gate, b_gate, w_out, norm2_g, w_router, b_router,
             w_gu, b_gu, w_dn, b_dn, final_norm_g):
    batch, seq, d = x.shape
    depth = w_in.shape[0]
    t = batch * seq
    xt = x.reshape(t, d)

    c8 = jnp.pad(c, ((0, 8 - batch), (0, 0)))
    n_mod = w_mod.shape[1]
    mod = _matmul(c8, w_mod, _col_spec(d, _tile(n_mod, 512)), d, n_mod, b_mod[None, :], tm=8,
                  tn=_tile(n_mod, 512), out_dtype=F32, prologue=lambda v: v * jax.nn.sigmoid(v),
                  name="mod_proj")[:batch]
    mod = mod.reshape(batch, N_MOD, d)

    pos = positions.reshape(t).astype(F32)
    tabs = (_rope_table(pos, HEAD_DIM // 2, LANES),
            _rope_table(pos, HEAD_DIM, LANES),
            jnp.concatenate([_rope_table(pos, HEAD_DIM, LANES), _rope_table(pos, IDX_DIM, LANES),
                             _rope_table(pos, IDX_DIM, IDX_DIM)], axis=1))

    for l in range(depth):
        m4 = (mod + ada_table[l][None])[:, :, None, :]
        lam_init = 0.8 - 0.6 * float(np.exp(-0.3 * l))
        p = dict(norm1_g=norm1_g[l][None, :], w_in_all=w_in, b_forget=b_forget[l],
                 diff_lambda=diff_lambda[l], diff_norm_g=diff_norm_g[l], w_branch=w_branch,
                 w_gate=w_gate, b_gate=b_gate[l], w_out=w_out, norm2_g=norm2_g[l][None, :],
                 w_router=w_router[l], b_router=b_router[l], w_gu=w_gu, b_gu=b_gu,
                 w_dn=w_dn, b_dn=b_dn)
        xt = _layer(xt, m4, tabs, lam_init, batch, seq, l, p)

    zeros4 = jnp.zeros((batch, 1, 1, d), F32)
    (out,) = _ada_norm(xt, final_norm_g[None, :], zeros4, 0, 0, seq, (F32,))
    return out.reshape(batch, seq, d)


def kernel(x, c, positions, w_mod, b_mod, ada_table, norm1_g, w_in, b_forget, diff_lambda,
           diff_norm_g, w_branch, w_gate, b_gate, w_out, norm2_g, w_router, b_router,
           w_gu, b_gu, w_dn, b_dn, final_norm_g):
    return _forward(x, c, positions, w_mod, b_mod, ada_table, norm1_g, w_in, b_forget, diff_lambda,
                    diff_norm_g, w_branch, w_gate, b_gate, w_out, norm2_g, w_router, b_router,
                    w_gu, b_gu, w_dn, b_dn, final_norm_g)
```

```python
import functools

import numpy as np
import jax
import jax.numpy as jnp
from jax import lax
from jax.experimental import pallas as pl
from jax.experimental.pallas import tpu as pltpu

F32 = jnp.float32
BF16 = jnp.bfloat16
I32 = jnp.int32

LANES = 128
HEAD_DIM = 128
CHUNK = 64
IDX_HEADS = 8
IDX_DIM = 64
TOPK_MAX = 256
ROPE_THETA = 500000.0
ROPE_FRACTION = 4
TOP_K = 4
SWIGLU_LIMIT = 7.0
SWIGLU_ALPHA = 1.702
RMS_EPS = 1e-5
N_MOD = 6
NEG = -1e30
INT_MIN = -(2 ** 31)
ROW_PITCH_PAD = 4
VMEM_LIMIT = 56 * 1024 * 1024


def _params(n_axes, vmem=VMEM_LIMIT):
    return pltpu.CompilerParams(dimension_semantics=("arbitrary",) * n_axes,
                                vmem_limit_bytes=vmem)


def _tile(n, pref):
    t = min(n, pref)
    while n % t:
        t //= 2
    return t


def _row_pitch(nseg):
    return nseg + ROW_PITCH_PAD


def _store_token_rows(o_ref, val, tm):
    nseg = val.shape[1] // LANES
    pitch = _row_pitch(nseg)
    for s in range(nseg):
        o_ref[pl.ds(s, tm, stride=pitch), :] = val[:, s * LANES:(s + 1) * LANES]
    for s in range(nseg, pitch):
        o_ref[pl.ds(s, tm, stride=pitch), :] = jnp.zeros((tm, LANES), o_ref.dtype)


def _load_token_rows(ref, tm, nseg, lead=()):
    return [ref[(*lead, pl.ds(s, tm, stride=_row_pitch(nseg)), slice(None))] for s in range(nseg)]


def _norm_kernel(x_ref, g_ref, sc_ref, sh_ref, *o_refs, tm, token_rows):
    x = x_ref[...]
    var = jnp.mean(x * x, axis=-1, keepdims=True)
    y = x * lax.rsqrt(var + RMS_EPS) * g_ref[...]
    h = y * (1.0 + sc_ref[...]) + sh_ref[...]
    for o, rows in zip(o_refs, token_rows):
        if rows:
            _store_token_rows(o, h, tm)
        else:
            o[...] = h.astype(o.dtype)


def _ada_norm(x, g, m4, sc_idx, sh_idx, seq, out_dtypes, token_rows=None):
    t, d = x.shape
    tm = _tile(seq, 256)
    per_b = seq // tm
    pitch = _row_pitch(d // LANES)
    token_rows = token_rows or (False,) * len(out_dtypes)
    out_specs, out_shape = [], []
    for dt, rows in zip(out_dtypes, token_rows):
        if rows:
            out_specs.append(pl.BlockSpec((tm * pitch, LANES), lambda i: (i, 0)))
            out_shape.append(jax.ShapeDtypeStruct((t * pitch, LANES), F32))
        else:
            out_specs.append(pl.BlockSpec((tm, d), lambda i: (i, 0)))
            out_shape.append(jax.ShapeDtypeStruct((t, d), dt))
    return pl.pallas_call(
        functools.partial(_norm_kernel, tm=tm, token_rows=token_rows),
        grid=(t // tm,),
        in_specs=[
            pl.BlockSpec((tm, d), lambda i: (i, 0)),
            pl.BlockSpec((1, d), lambda i: (0, 0)),
            pl.BlockSpec((None, None, 1, d), lambda i: (i // per_b, sc_idx, 0, 0)),
            pl.BlockSpec((None, None, 1, d), lambda i: (i // per_b, sh_idx, 0, 0)),
        ],
        out_specs=out_specs,
        out_shape=out_shape,
        compiler_params=_params(1),
        name="ada_norm",
    )(x, g, m4, m4)


def _mm_kernel(*refs, n_extra, epilogue, prologue, cast_rows, w_rows_are_outputs):
    x_ref, w_ref, b_ref = refs[:3]
    extra = refs[3:3 + n_extra]
    o_ref = refs[3 + n_extra]
    wbf_ref = refs[4 + n_extra]
    if len(w_ref.shape) == 3:
        w_ref = w_ref.at[0]

    @pl.when(pl.program_id(1) == 0)
    def _():
        def body(r, c):
            rows = pl.ds(pl.multiple_of(r * cast_rows, cast_rows), cast_rows)
            wbf_ref[rows, :] = w_ref[rows, :].astype(BF16)
            return c

        lax.fori_loop(0, w_ref.shape[0] // cast_rows, body, 0)

    x = x_ref[...]
    if prologue is not None:
        x = prologue(x)
    x = x.astype(BF16)
    if w_rows_are_outputs:
        acc = _qk(x, wbf_ref[...])
    else:
        acc = jnp.dot(x, wbf_ref[...], preferred_element_type=F32)
    o_ref[...] = epilogue(acc + b_ref[...], *[e[...] for e in extra]).astype(o_ref.dtype)


def _matmul(x, w, w_spec, k, n, bias, *, tm, tn, out_dtype, epilogue=None, extra=(),
            extra_specs=(), prologue=None, w_rows_are_outputs=False, name="matmul"):
    m = x.shape[0]
    if epilogue is None:
        epilogue = lambda acc: acc
    w_block = (tn, k) if w_rows_are_outputs else (k, tn)
    kern = functools.partial(_mm_kernel, n_extra=len(extra), epilogue=epilogue, prologue=prologue,
                             cast_rows=_tile(w_block[0], 256), w_rows_are_outputs=w_rows_are_outputs)
    return pl.pallas_call(
        kern,
        grid=(n // tn, m // tm),
        in_specs=[pl.BlockSpec((tm, k), lambda j, i: (i, 0)),
                  w_spec,
                  pl.BlockSpec((1, tn), lambda j, i: (0, j)),
                  *extra_specs],
        out_specs=pl.BlockSpec((tm, tn), lambda j, i: (i, j)),
        out_shape=jax.ShapeDtypeStruct((m, n), out_dtype),
        scratch_shapes=[pltpu.VMEM(w_block, BF16)],
        compiler_params=_params(2),
        name=name,
    )(x, w, bias, *extra)


def _rope_epilogue(acc, tab, *, groups):
    outs = []
    for g, kind in enumerate(groups):
        xg = acc[:, g * LANES:(g + 1) * LANES]
        if kind is not None:
            slot, half = kind
            base = 3 * slot * LANES
            cos = tab[:, base:base + LANES]
            sin_lo = tab[:, base + LANES:base + 2 * LANES]
            sin_hi = tab[:, base + 2 * LANES:base + 3 * LANES]
            xg = (xg * cos + pltpu.roll(xg, half, 1) * sin_lo
                  + pltpu.roll(xg, LANES - half, 1) * sin_hi)
        outs.append(xg)
    return outs[0] if len(outs) == 1 else jnp.concatenate(outs, axis=1)


def _rope_table(pos, period, active):
    rot = period // ROPE_FRACTION
    half = rot // 2
    inv_freq = jnp.float32(ROPE_THETA) ** (-jnp.arange(half, dtype=F32) / half)
    lane = np.arange(LANES)
    r = lane % period
    on = lane < active
    first = on & (r < half)
    second = on & (r >= half) & (r < rot)
    fidx = np.where(first, r, np.where(second, r - half, 0))
    ang = pos[:, None] * inv_freq[fidx][None, :]
    cos, sin = jnp.cos(ang), jnp.sin(ang)
    c = jnp.where((first | second)[None, :], cos, 1.0)
    s_lo = jnp.where(second[None, :], sin, 0.0)
    s_hi = jnp.where(first[None, :], -sin, 0.0)
    return jnp.concatenate([c, s_lo, s_hi], axis=1)


def _qk(q, k):
    return lax.dot_general(q, k, (((1,), (1,)), ((), ())), preferred_element_type=F32)


def _softmax_tile(carry, s, v):
    m, l, acc = carry
    m_new = jnp.maximum(m, jnp.max(s, axis=-1, keepdims=True))
    alpha = jnp.exp(m - m_new)
    p = jnp.exp(s - m_new)
    l = alpha * l + jnp.sum(p, axis=-1, keepdims=True)
    acc = alpha * acc + jnp.dot(p.astype(BF16), v, preferred_element_type=F32)
    return m_new, l, acc


def _softmax_init(rows):
    return (jnp.full((rows, 1), NEG, F32), jnp.zeros((rows, 1), F32),
            jnp.zeros((rows, HEAD_DIM), F32))


def _rows(ref, j, t):
    return ref[pl.ds(pl.multiple_of(j * t, t), t), :]


def _attn_diff_kernel(dl_ref, g_ref, q_ref, k_ref, v_ref, o_ref, *, tq, lam_init):
    i = pl.program_id(2)
    half = HEAD_DIM // 2
    scale = half ** -0.5
    q = q_ref[...]
    lane = lax.broadcasted_iota(I32, q.shape, 1)
    zero = jnp.zeros_like(q)
    qz = jnp.concatenate([jnp.where(lane < half, q, zero), jnp.where(lane >= half, q, zero)], axis=0)

    def full_tile(j, carry):
        return _softmax_tile(carry, _qk(qz, _rows(k_ref, j, tq)) * scale, _rows(v_ref, j, tq))

    carry = lax.fori_loop(0, i, full_tile, _softmax_init(2 * tq))
    r = lax.broadcasted_iota(I32, (2 * tq, tq), 0)
    r = jnp.where(r >= tq, r - tq, r)
    c = lax.broadcasted_iota(I32, (2 * tq, tq), 1)
    vis = c < (r // CHUNK + 1) * CHUNK
    s = jnp.where(vis, _qk(qz, _rows(k_ref, i, tq)) * scale, NEG)
    _, l, acc = _softmax_tile(carry, s, _rows(v_ref, i, tq))

    dl = dl_ref[...]
    lam = (jnp.exp(jnp.sum(dl[0:1] * dl[1:2], axis=-1, keepdims=True))
           - jnp.exp(jnp.sum(dl[2:3] * dl[3:4], axis=-1, keepdims=True)) + lam_init)
    o = acc / l
    o = o[:tq] - lam * o[tq:]
    var = jnp.mean(o * o, axis=-1, keepdims=True)
    o_ref[...] = (o * lax.rsqrt(var + RMS_EPS) * g_ref[...] * (1.0 - lam_init)).astype(o_ref.dtype)


def _attn_forget_kernel(dq_ref, dk_ref, q_ref, k_ref, v_ref, o_ref, *, tq):
    i = pl.program_id(2)
    scale = HEAD_DIM ** -0.5
    q = q_ref[...]
    dq = dq_ref[...]

    def scores(j):
        dk = dk_ref[:, pl.ds(pl.multiple_of(j * tq, tq), tq)]
        return _qk(q, _rows(k_ref, j, tq)) * scale + dq - dk

    def full_tile(j, carry):
        return _softmax_tile(carry, scores(j), _rows(v_ref, j, tq))

    carry = lax.fori_loop(0, i, full_tile, _softmax_init(tq))
    r = lax.broadcasted_iota(I32, (tq, tq), 0)
    c = lax.broadcasted_iota(I32, (tq, tq), 1)
    s = jnp.where(c <= r, scores(i), NEG)
    _, l, acc = _softmax_tile(carry, s, _rows(v_ref, i, tq))
    o_ref[...] = (acc / l).astype(o_ref.dtype)


def _log_sigmoid(z):
    return jnp.minimum(z, 0.0) - jnp.log(1.0 + jnp.exp(-jnp.abs(z)))


def _attn_stick_kernel(q_ref, k_ref, v_ref, o_ref, *, tq):
    i = pl.program_id(2)
    scale = HEAD_DIM ** -0.5
    q = q_ref[...]
    r = lax.broadcasted_iota(I32, (tq, tq), 0)
    c = lax.broadcasted_iota(I32, (tq, tq), 1)
    cb = min(tq, 256)
    later = jnp.where(lax.broadcasted_iota(I32, (cb, cb), 0) > lax.broadcasted_iota(I32, (cb, cb), 1),
                      1.0, 0.0).astype(BF16)

    def tile(j, carry, mask):
        suffix, acc = carry
        z = _qk(q, _rows(k_ref, j, tq)) * scale
        log_beta = _log_sigmoid(z)
        log_keep = log_beta - z
        if mask is not None:
            log_keep = jnp.where(mask, log_keep, 0.0)
        parts = [None] * (tq // cb)
        for blk in reversed(range(tq // cb)):
            lk = log_keep[:, blk * cb:(blk + 1) * cb]
            hi = lk.astype(BF16)
            lo = (lk - hi.astype(F32)).astype(BF16)
            parts[blk] = (jnp.dot(hi, later, preferred_element_type=F32)
                          + jnp.dot(lo, later, preferred_element_type=F32) + suffix)
            suffix = suffix + jnp.sum(lk, axis=-1, keepdims=True)
        after = parts[0] if len(parts) == 1 else jnp.concatenate(parts, axis=1)
        w = jnp.exp(log_beta + after)
        if mask is not None:
            w = jnp.where(mask, w, 0.0)
        acc = acc + jnp.dot(w.astype(BF16), _rows(v_ref, j, tq), preferred_element_type=F32)
        return suffix, acc

    carry = tile(i, (jnp.zeros((tq, 1), F32), jnp.zeros((tq, HEAD_DIM), F32)), c < r)
    _, acc = lax.fori_loop(0, i, lambda jj, cr: tile(i - 1 - jj, cr, None), carry)
    o_ref[...] = acc.astype(o_ref.dtype)


def _attention(kernel, q_arr, q_col, k_arr, k_col, v_arr, v_col, heads, batch, seq, tq,
               pre=(), pre_specs=(), name="attention"):
    nq = seq // tq
    t = batch * seq
    return pl.pallas_call(
        kernel,
        grid=(batch, heads, nq),
        in_specs=[*pre_specs,
                  pl.BlockSpec((tq, HEAD_DIM), lambda b, h, i: (b * nq + i, q_col + h)),
                  pl.BlockSpec((seq, HEAD_DIM), lambda b, h, i: (b, k_col + h)),
                  pl.BlockSpec((seq, HEAD_DIM), lambda b, h, i: (b, v_col + h))],
        out_specs=pl.BlockSpec((tq, HEAD_DIM), lambda b, h, i: (b * nq + i, h)),
        out_shape=jax.ShapeDtypeStruct((t, heads * HEAD_DIM), BF16),
        compiler_params=_params(3),
        name=name,
    )(*pre, q_arr, k_arr, v_arr)


def _sortable(x):
    bits = pltpu.bitcast(x + 0.0, I32)
    return jnp.where(bits < 0, bits ^ jnp.int32(0x7FFFFFFF), bits)


def _attn_sparse_kernel(q_ref, sq_ref, kv_ref, ik_ref, o_ref, key_ref, thr_ref, *, tq, seq, heads,
                        topk, iq_col, iw_col):
    i = pl.program_id(1)
    scale = HEAD_DIM ** -0.5
    n_tiles = seq // tq
    q0 = i * tq

    sq = sq_ref[...]
    lane_k = lax.broadcasted_iota(I32, (tq, LANES), 1)
    iq_groups = [sq[:, iq_col + g * LANES: iq_col + (g + 1) * LANES].astype(BF16)
                 for g in range(IDX_HEADS * IDX_DIM // LANES)]
    iw = [sq[:, iw_col + h: iw_col + h + 1] for h in range(IDX_HEADS)]
    row = lax.broadcasted_iota(I32, (tq, tq), 0)
    col = lax.broadcasted_iota(I32, (tq, tq), 1)
    chunk_end = ((q0 + row) // CHUNK + 1) * CHUNK

    key_ref[...] = jnp.full((tq, seq), INT_MIN, I32)

    def score_tile(j, c):
        ikt = _rows(ik_ref, j, tq)
        ik_lo = jnp.where(lane_k < IDX_DIM, ikt, 0.0)
        ik_hi = pltpu.roll(ik_lo, IDX_DIM, 1)
        ik_lo = ik_lo.astype(BF16)
        ik_hi = ik_hi.astype(BF16)
        score = jnp.zeros((tq, tq), F32)
        for h in range(IDX_HEADS):
            s = _qk(iq_groups[h // 2], ik_hi if h % 2 else ik_lo)
            score = score + iw[h] * jnp.maximum(s, 0.0)
        vis = (j * tq + col) < chunk_end
        key_ref[:, pl.ds(pl.multiple_of(j * tq, tq), tq)] = jnp.where(vis, _sortable(score), INT_MIN)
        return c

    lax.fori_loop(0, i + 1, score_tile, 0)

    def count_ge(cand, ncols=seq):
        return jnp.sum(jnp.where(key_ref[:, :ncols] >= cand, 1.0, 0.0), axis=-1, keepdims=True)

    kf = jnp.float32(topk)

    def find_threshold(ncols):
        thr = jnp.where(count_ge(jnp.zeros((tq, 1), I32), ncols) >= kf, 0, INT_MIN).astype(I32)

        def bisect(b, thr):
            cand = thr | lax.shift_left(jnp.int32(1), 30 - b)
            return jnp.where(count_ge(cand, ncols) >= kf, cand, thr)

        thr = lax.fori_loop(0, jnp.where((i + 1) * tq <= topk, 0, 31), bisect, thr)
        thr_ref[...] = jnp.maximum(thr, INT_MIN + 1)

    lo = 0
    for hi in sorted({min(n_tiles, v) for v in range(2, n_tiles + 2, 2)}):
        pl.when((i + 1 > lo) & (i + 1 <= hi))(functools.partial(find_threshold, hi * tq))
        lo = hi
    thr = thr_ref[...]
    n_ge = count_ge(thr)

    @pl.when(jnp.max(n_ge) > kf)
    def _():
        need = kf - jnp.sum(jnp.where(key_ref[...] > thr, 1.0, 0.0), axis=-1, keepdims=True)
        upto = jnp.where(row <= col, 1.0, 0.0).astype(BF16)

        def fix(j, seen):
            cols = pl.ds(pl.multiple_of(j * tq, tq), tq)
            kt = key_ref[:, cols]
            eq = kt == thr
            rank = jnp.dot(jnp.where(eq, 1.0, 0.0).astype(BF16), upto, preferred_element_type=F32) + seen
            key_ref[:, cols] = jnp.where(eq & (rank > need), INT_MIN, kt)
            return seen + jnp.sum(jnp.where(eq, 1.0, 0.0), axis=-1, keepdims=True)

        lax.fori_loop(0, n_tiles, fix, jnp.zeros((tq, 1), F32))

    qs = jnp.concatenate([q_ref[:, h * HEAD_DIM:(h + 1) * HEAD_DIM] for h in range(heads)], axis=0)

    def attn_tile(j, carry):
        kvt = _rows(kv_ref, j, tq)
        kt = kvt[:, :HEAD_DIM].astype(BF16)
        vt = kvt[:, HEAD_DIM:].astype(BF16)
        bias = jnp.where(key_ref[:, pl.ds(pl.multiple_of(j * tq, tq), tq)] >= thr, 0.0, NEG)
        s = _qk(qs, kt) * scale + jnp.concatenate([bias] * heads, axis=0)
        return _softmax_tile(carry, s, vt)

    _, l, acc = lax.fori_loop(0, i + 1, attn_tile, _softmax_init(heads * tq))
    o = acc / l
    for h in range(heads):
        o_ref[:, h * HEAD_DIM:(h + 1) * HEAD_DIM] = o[h * tq:(h + 1) * tq].astype(o_ref.dtype)


def _attn_sparse(dq, small, heads, batch, seq, tq, topk, iq_col, ik_col, iw_col):
    nq = seq // tq
    t = batch * seq
    bw = heads * HEAD_DIM
    w = small.shape[1]
    kern = functools.partial(_attn_sparse_kernel, tq=tq, seq=seq, heads=heads, topk=topk,
                             iq_col=iq_col, iw_col=iw_col)
    return pl.pallas_call(
        kern,
        grid=(batch, nq),
        in_specs=[pl.BlockSpec((tq, bw), lambda b, i: (b * nq + i, 0)),
                  pl.BlockSpec((tq, w), lambda b, i: (b * nq + i, 0)),
                  pl.BlockSpec((seq, 2 * HEAD_DIM), lambda b, i: (b, 0)),
                  pl.BlockSpec((seq, LANES), lambda b, i: (b, ik_col // LANES))],
        out_specs=pl.BlockSpec((tq, bw), lambda b, i: (b * nq + i, 0)),
        out_shape=jax.ShapeDtypeStruct((t, bw), BF16),
        scratch_shapes=[pltpu.VMEM((tq, seq), I32), pltpu.VMEM((tq, 1), I32)],
        compiler_params=_params(2),
        name="attn_sparse",
    )(dq, small, small, small)


def _forget_cumsum_kernel(x_ref, b_ref, o_ref, *, seq, blk):
    r = lax.broadcasted_iota(I32, (blk, blk), 0)
    c = lax.broadcasted_iota(I32, (blk, blk), 1)
    tri = jnp.where(c <= r, 1.0, 0.0).astype(BF16)
    carry = jnp.zeros((1, LANES), F32)
    for s in range(seq // blk):
        logf = _log_sigmoid(x_ref[s * blk:(s + 1) * blk, :] + b_ref[...])
        hi = logf.astype(BF16)
        rem = logf - hi.astype(F32)
        mid = rem.astype(BF16)
        lo = (rem - mid.astype(F32)).astype(BF16)
        local = (jnp.dot(tri, hi, preferred_element_type=F32)
                 + jnp.dot(tri, mid, preferred_element_type=F32)
                 + jnp.dot(tri, lo, preferred_element_type=F32)) + carry
        o_ref[s * blk:(s + 1) * blk, :] = local
        carry = local[blk - 1:blk, :]


def _forget_cumsum(small, bias_row, batch, seq, col):
    return pl.pallas_call(
        functools.partial(_forget_cumsum_kernel, seq=seq, blk=_tile(seq, 256)),
        grid=(batch,),
        in_specs=[pl.BlockSpec((seq, LANES), lambda b: (b, col // LANES)),
                  pl.BlockSpec((1, LANES), lambda b: (0, 0))],
        out_specs=pl.BlockSpec((seq, LANES), lambda b: (b, 0)),
        out_shape=jax.ShapeDtypeStruct((batch * seq, LANES), F32),
        compiler_params=_params(1),
        name="forget_cumsum",
    )(small, bias_row)


def _merge_kernel(*refs, n_br, cast_rows):
    o_refs = refs[:n_br]
    g_refs = refs[n_br:2 * n_br]
    w_ref = refs[2 * n_br]
    out_ref = refs[2 * n_br + 1]
    wbf_ref = refs[2 * n_br + 2]

    @pl.when(pl.program_id(1) == 0)
    def _():
        for b in range(n_br):
            def body(r, c, b=b):
                rows = pl.ds(pl.multiple_of(r * cast_rows, cast_rows), cast_rows)
                wbf_ref[b, rows, :] = w_ref[b, rows, :].astype(BF16)
                return c
            lax.fori_loop(0, w_ref.shape[1] // cast_rows, body, 0)

    acc = None
    for b in range(n_br):
        term = g_refs[b][...] * jnp.dot(o_refs[b][...], wbf_ref[b], preferred_element_type=F32)
        acc = term if acc is None else acc + term
    out_ref[...] = acc.astype(out_ref.dtype)


def _merge(outs, gates, w_branch, layer, tm, tn):
    _, n_br, bw, d = w_branch.shape
    t = outs[0].shape[0]
    nd = d // tn
    kern = functools.partial(_merge_kernel, n_br=n_br, cast_rows=_tile(bw, 256))
    return pl.pallas_call(
        kern,
        grid=(nd, t // tm),
        in_specs=[*[pl.BlockSpec((tm, bw), lambda j, i: (i, 0)) for _ in range(n_br)],
                  *[pl.BlockSpec((tm, tn), lambda j, i, b=b: (i, b * nd + j)) for b in range(n_br)],
                  pl.BlockSpec((None, n_br, bw, tn), lambda j, i: (layer, 0, 0, j))],
        out_specs=pl.BlockSpec((tm, tn), lambda j, i: (i, j)),
        out_shape=jax.ShapeDtypeStruct((t, d), BF16),
        scratch_shapes=[pltpu.VMEM((n_br, bw, tn), BF16)],
        compiler_params=_params(2),
        name="gated_merge",
    )(*outs, *([gates] * n_br), w_branch)


def _router_kernel(h_ref, w_ref, b_ref, meta_i_ref, meta_w_ref, cnt_ref, run_ref, *, tm, n_exp):
    @pl.when(pl.program_id(0) == 0)
    def _():
        run_ref[...] = jnp.zeros_like(run_ref)

    lane = lax.broadcasted_iota(I32, (tm, LANES), 1)
    lane_f = lane.astype(F32)
    logits = jnp.dot(h_ref[...], w_ref[...].astype(BF16), preferred_element_type=F32) + b_ref[...]
    logits = jnp.where(lane < n_exp, logits, -jnp.inf)

    vals, idxs = [], []
    work = logits
    for _ in range(TOP_K):
        v = jnp.max(work, axis=-1, keepdims=True)
        ix = jnp.min(jnp.where(work == v, lane_f, float(LANES)), axis=-1, keepdims=True).astype(I32)
        vals.append(v)
        idxs.append(ix)
        work = jnp.where(lane == ix, -jnp.inf, work)
    es = [jnp.exp(v - vals[0]) for v in vals]
    denom = es[0] + es[1] + es[2] + es[3]

    onehots = [jnp.where(lane == ix, 1.0, 0.0) for ix in idxs]
    sel = onehots[0] + onehots[1] + onehots[2] + onehots[3]
    r = lax.broadcasted_iota(I32, (tm, tm), 0)
    c = lax.broadcasted_iota(I32, (tm, tm), 1)
    before = jnp.where(c < r, 1.0, 0.0).astype(BF16)
    rank = jnp.dot(before, sel.astype(BF16), preferred_element_type=F32) + run_ref[...]
    run_ref[...] = run_ref[...] + jnp.sum(sel, axis=0, keepdims=True)

    meta_i = jnp.zeros((tm, LANES), I32)
    meta_w = jnp.zeros((tm, LANES), F32)
    for k in range(TOP_K):
        rk = jnp.sum(onehots[k] * rank, axis=-1, keepdims=True).astype(I32)
        meta_i = jnp.where(lane == k, idxs[k], meta_i)
        meta_i = jnp.where(lane == TOP_K + k, rk, meta_i)
        meta_w = jnp.where(lane == k, es[k] / denom, meta_w)
    meta_i_ref[...] = meta_i
    meta_w_ref[...] = meta_w
    cnt_ref[...] = run_ref[...]


def _router(h, w_router_pad, b_router_pad, n_exp):
    t, d = h.shape
    tm = _tile(t, 256)
    kern = functools.partial(_router_kernel, tm=tm, n_exp=n_exp)
    return pl.pallas_call(
        kern,
        grid=(t // tm,),
        in_specs=[pl.BlockSpec((tm, d), lambda i: (i, 0)),
                  pl.BlockSpec((d, LANES), lambda i: (0, 0)),
                  pl.BlockSpec((1, LANES), lambda i: (0, 0))],
        out_specs=[pl.BlockSpec((tm, LANES), lambda i: (i, 0)),
                   pl.BlockSpec((tm, LANES), lambda i: (i, 0)),
                   pl.BlockSpec((1, LANES), lambda i: (0, 0))],
        out_shape=[jax.ShapeDtypeStruct((t, LANES), I32),
                   jax.ShapeDtypeStruct((t, LANES), F32),
                   jax.ShapeDtypeStruct((1, LANES), F32)],
        scratch_shapes=[pltpu.VMEM((1, LANES), F32)],
        compiler_params=_params(1),
        name="router",
    )(h, w_router_pad, b_router_pad)


def _ffn_kernel(te_ref, nu_ref, tok_ref, h_ref, wgu_ref, bgu_ref, wdn_ref, bdn_ref, o_ref,
                xbuf, pick_ref, sem, *, tm, nseg, kc, nc):
    i = pl.program_id(0)
    n_used = nu_ref[0]
    d, ff = wgu_ref.shape[0], wdn_ref.shape[0]
    pitch = _row_pitch(nseg)

    def fetch(tile, slot):
        def body(r, c):
            tok = tok_ref[tile * tm + r]
            pltpu.make_async_copy(h_ref.at[pl.ds(tok * pitch, nseg), :],
                                  xbuf.at[slot, pl.ds(r * pitch, nseg), :], sem.at[slot]).start()
            return c
        lax.fori_loop(0, tm, body, 0)

    @pl.when(i == 0)
    def _():
        r = lax.broadcasted_iota(I32, (2 * ff, ff), 0)
        c = lax.broadcasted_iota(I32, (2 * ff, ff), 1)
        pick_ref[...] = jnp.where(r == 2 * c, 1.0, 0.0).astype(BF16)
        fetch(0, 0)

    @pl.when(i + 1 < n_used)
    def _():
        fetch(i + 1, (i + 1) % 2)

    @pl.when(i < n_used)
    def _():
        slot = i % 2
        done = xbuf.at[slot, pl.ds(0, tm * nseg), :]
        pltpu.make_async_copy(done, done, sem.at[slot]).wait()
        parts = _load_token_rows(xbuf, tm, nseg, lead=(slot,))
        per = kc // LANES
        gu = bgu_ref[...]
        for c in range(d // kc):
            xc = jnp.concatenate(parts[c * per:(c + 1) * per], axis=1).astype(BF16)
            gu = gu + jnp.dot(xc, wgu_ref[c * kc:(c + 1) * kc, :].astype(BF16),
                              preferred_element_type=F32)
        lin = pltpu.roll(gu, 2 * ff - 1, 1)
        glu = jnp.minimum(gu, SWIGLU_LIMIT)
        lin = jnp.clip(lin, -SWIGLU_LIMIT, SWIGLU_LIMIT)
        act = glu * jax.nn.sigmoid(SWIGLU_ALPHA * glu) * (lin + 1.0)
        act = jnp.dot(act.astype(BF16), pick_ref[...], preferred_element_type=F32).astype(BF16)
        for c in range(d // nc):
            cols = slice(c * nc, (c + 1) * nc)
            o_ref[:, cols] = (jnp.dot(act, wdn_ref[:, cols].astype(BF16), preferred_element_type=F32)
                              + bdn_ref[:, cols])

    @pl.when(i >= n_used)
    def _():
        o_ref[...] = jnp.zeros_like(o_ref)


def _expert_ffn(h_rows, tile_expert, n_used, token_of_row, w_gu, b_gu, w_dn, b_dn, layer, tm):
    _, n_exp, d, ff2 = w_gu.shape
    ff = ff2 // 2
    nseg = d // LANES
    n_tiles = token_of_row.shape[0] // tm
    once = pl.Buffered(1)

    def wsel(i, te, nu, tok):
        return (layer, te[i], 0, 0)

    kern = functools.partial(_ffn_kernel, tm=tm, nseg=nseg, kc=_tile(d, 512), nc=_tile(d, 1024))
    return pl.pallas_call(
        kern,
        grid_spec=pltpu.PrefetchScalarGridSpec(
            num_scalar_prefetch=3,
            grid=(n_tiles,),
            in_specs=[pl.BlockSpec(memory_space=pl.ANY),
                      pl.BlockSpec((None, None, d, ff2), wsel, pipeline_mode=once),
                      pl.BlockSpec((None, None, 1, ff2), wsel),
                      pl.BlockSpec((None, None, ff, d), wsel, pipeline_mode=once),
                      pl.BlockSpec((None, None, 1, d), wsel)],
            out_specs=pl.BlockSpec((tm, d), lambda i, te, nu, tok: (i, 0)),
            scratch_shapes=[pltpu.VMEM((2, tm * _row_pitch(nseg), LANES), F32),
                            pltpu.VMEM((ff2, ff), BF16),
                            pltpu.SemaphoreType.DMA((2,))]),
        out_shape=jax.ShapeDtypeStruct((n_tiles * tm, d), F32),
        compiler_params=_params(1),
        name="expert_ffn",
    )(tile_expert, n_used, token_of_row, h_rows, w_gu, b_gu, w_dn, b_dn)


def _combine_kernel(pos_ref, ys_ref, x_ref, g_ref, w_ref, o_ref, buf, sem, *, tm):
    i = pl.program_id(0)

    def fetch(tile, slot):
        def body(t, c):
            for k in range(TOP_K):
                row = pos_ref[(tile * tm + t) * TOP_K + k]
                pltpu.make_async_copy(ys_ref.at[pl.ds(row, 1), :],
                                      buf.at[slot, k, pl.ds(t, 1), :], sem.at[slot]).start()
            return c
        lax.fori_loop(0, tm, body, 0)

    @pl.when(i == 0)
    def _():
        fetch(0, 0)

    @pl.when(i + 1 < pl.num_programs(0))
    def _():
        fetch(i + 1, (i + 1) % 2)

    slot = i % 2
    pltpu.make_async_copy(buf.at[slot], buf.at[slot], sem.at[slot]).wait()
    w = w_ref[...]
    y = w[:, 0:1] * buf[slot, 0]
    for k in range(1, TOP_K):
        y = y + w[:, k:k + 1] * buf[slot, k]
    o_ref[...] = x_ref[...] + g_ref[...] * y


def _combine(ys, pos_flat, x, m4, gate_idx, meta_w, seq, tm):
    t, d = x.shape
    per_b = seq // tm
    return pl.pallas_call(
        functools.partial(_combine_kernel, tm=tm),
        grid_spec=pltpu.PrefetchScalarGridSpec(
            num_scalar_prefetch=1,
            grid=(t // tm,),
            in_specs=[pl.BlockSpec(memory_space=pl.ANY),
                      pl.BlockSpec((tm, d), lambda i, p: (i, 0)),
                      pl.BlockSpec((None, None, 1, d), lambda i, p: (i // per_b, gate_idx, 0, 0)),
                      pl.BlockSpec((tm, LANES), lambda i, p: (i, 0))],
            out_specs=pl.BlockSpec((tm, d), lambda i, p: (i, 0)),
            scratch_shapes=[pltpu.VMEM((2, TOP_K, tm, d), F32),
                            pltpu.SemaphoreType.DMA((2,))]),
        out_shape=jax.ShapeDtypeStruct((t, d), F32),
        compiler_params=_params(1),
        name="moe_combine",
    )(pos_flat, ys, x, m4, meta_w)


def _moe(h_bf, h_rows, x, m4, w_router, b_router, w_gu, b_gu, w_dn, b_dn, layer, seq):
    t, d = x.shape
    n_exp = w_router.shape[1]
    tm = _tile(t, 256)
    wr = jnp.pad(w_router, ((0, 0), (0, LANES - n_exp)))
    br = jnp.pad(b_router, (0, LANES - n_exp))[None, :]
    meta_i, meta_w, counts = _router(h_bf, wr, br, n_exp)

    counts = counts[0, :n_exp].astype(I32)
    padded = ((counts + tm - 1) // tm) * tm
    ends = jnp.cumsum(padded)
    starts = ends - padded
    n_rows = t * TOP_K + n_exp * tm
    n_tiles = n_rows // tm
    experts = meta_i[:, :TOP_K]
    pos = starts[experts] + meta_i[:, TOP_K:2 * TOP_K]
    tile_start = jnp.arange(n_tiles, dtype=I32) * tm
    tile_expert = jnp.minimum(jnp.sum((ends[None, :] <= tile_start[:, None]).astype(I32), axis=1),
                              n_exp - 1)
    n_used = (ends[-1] // tm).astype(I32)[None]
    token_of_row = jnp.zeros((n_rows,), I32).at[pos.reshape(-1)].set(
        jnp.repeat(jnp.arange(t, dtype=I32), TOP_K))

    ys = _expert_ffn(h_rows, tile_expert, n_used, token_of_row, w_gu, b_gu[:, :, None, :], w_dn,
                     b_dn[:, :, None, :], layer, tm)
    return _combine(ys, pos.reshape(-1), x, m4, 5, meta_w, seq, _tile(seq, 64))


def _col_spec(k, tn, off_blocks=0):
    return pl.BlockSpec((k, tn), lambda j, i: (0, j + off_blocks))


def _layer_col_spec(layer, k, tn, off_blocks=0):
    return pl.BlockSpec((None, k, tn), lambda j, i: (layer, 0, j + off_blocks))


def _layer(x, m4, tabs, lam_init, batch, seq, layer, p):
    t, d = x.shape
    bw = p["w_branch"].shape[2]
    heads = bw // HEAD_DIM
    tm = _tile(seq, 1024)
    tn = _tile(bw, 512)
    w_in_t = p["w_in_t"]
    zero = lambda n: jnp.zeros((1, n), F32)

    (h,) = _ada_norm(x, p["norm1_g"], m4, 1, 0, seq, (BF16,))

    tab8, tab16, tab_small = tabs
    tab_spec = lambda w: pl.BlockSpec((tm, w), lambda j, i: (i, 0))
    rope8 = functools.partial(_rope_epilogue, groups=((0, 8),) * (tn // LANES))
    rope16 = functools.partial(_rope_epilogue, groups=((0, 16),) * (tn // LANES))

    def in_proj(row0, n, name, **kw):
        spec = pl.BlockSpec((pl.Element(1), pl.Element(tn), pl.Element(d)),
                            lambda j, i: (layer, pl.multiple_of(row0 + j * tn, 8), 0))
        return _matmul(h, w_in_t, spec, d, n, zero(n), tm=tm, tn=tn, out_dtype=BF16,
                       w_rows_are_outputs=True, name=name, **kw)

    c0 = 6 * bw + heads
    qk_a = in_proj(0, 2 * bw, "proj_a_qk", epilogue=rope8, extra=(tab8,),
                   extra_specs=(tab_spec(3 * LANES),))
    mid = in_proj(2 * bw, 4 * bw, "proj_av_b")
    qkv_c = in_proj(c0, 3 * bw, "proj_c")
    q_d = in_proj(c0 + 3 * bw, bw, "proj_d_q", epilogue=rope16, extra=(tab16,),
                  extra_specs=(tab_spec(3 * LANES),))
    s0 = c0 + 4 * bw
    n_iq = IDX_HEADS * IDX_DIM
    used = 2 * HEAD_DIM + n_iq + IDX_DIM + IDX_HEADS + heads
    ws = -(-used // LANES) * LANES
    w_small_t = jnp.concatenate(
        [w_in_t[layer, s0:s0 + 2 * HEAD_DIM + n_iq + IDX_DIM + IDX_HEADS],
         w_in_t[layer, 6 * bw:6 * bw + heads], jnp.zeros((ws - used, d), F32)], axis=0)
    ik_col = 2 * HEAD_DIM + n_iq
    iw_col = ik_col + IDX_DIM
    fl_col = iw_col + IDX_HEADS
    groups_small = ((0, 16), None) + ((1, 8),) * (n_iq // LANES) + ((2, 8),)
    tm_s = _tile(seq, 512)
    small = _matmul(h, w_small_t, pl.BlockSpec((ws, d), lambda j, i: (0, 0), pipeline_mode=pl.Buffered(1)),
                    d, ws, zero(ws), tm=tm_s, tn=ws, out_dtype=F32, w_rows_are_outputs=True,
                    epilogue=functools.partial(_rope_epilogue, groups=groups_small), extra=(tab_small,),
                    extra_specs=(pl.BlockSpec((tm_s, 9 * LANES), lambda j, i: (i, 0)),), name="proj_small")

    fl_lane = fl_col % LANES
    bias_row = jnp.zeros((1, LANES), F32).at[0, fl_lane:fl_lane + heads].set(p["b_forget"])
    dcum = _forget_cumsum(small, bias_row, batch, seq, ik_col)[:, fl_lane:fl_lane + heads]
    dcum = dcum.reshape(batch, seq, heads).transpose(0, 2, 1)
    dq_col = dcum[..., None]
    dk_row = dcum[:, :, None, :]

    tq = _tile(seq, 512)
    hb = bw // HEAD_DIM
    o_a = _attention(
        functools.partial(_attn_diff_kernel, tq=tq, lam_init=lam_init),
        qk_a, 0, qk_a, hb, mid, 0, heads, batch, seq, tq,
        pre=(p["diff_lambda"], p["diff_norm_g"][None, :]),
        pre_specs=(pl.BlockSpec(p["diff_lambda"].shape, lambda b, h, i: (0, 0)),
                   pl.BlockSpec((1, HEAD_DIM), lambda b, h, i: (0, 0))),
        name="attn_diff")
    o_b = _attention(
        functools.partial(_attn_forget_kernel, tq=tq),
        mid, hb, mid, 2 * hb, mid, 3 * hb, heads, batch, seq, tq,
        pre=(dq_col, dk_row),
        pre_specs=(pl.BlockSpec((None, None, tq, 1), lambda b, h, i: (b, h, i, 0)),
                   pl.BlockSpec((None, None, 1, seq), lambda b, h, i: (b, h, 0, 0))),
        name="attn_forget")
    o_c = _attention(functools.partial(_attn_stick_kernel, tq=tq),
                     qkv_c, 0, qkv_c, hb, qkv_c, 2 * hb, heads, batch, seq, tq, name="attn_stick")
    o_d = _attn_sparse(q_d, small, heads, batch, seq, _tile(seq, 256), min(TOPK_MAX, seq // 4),
                       2 * HEAD_DIM, ik_col, iw_col)

    n_br = p["w_gate"].shape[1]
    nd = d // tn
    gates = _matmul(h, p["w_gate"],
                    pl.BlockSpec((None, None, d, tn), lambda j, i: (layer, j // nd, 0, j % nd)),
                    d, n_br * d, p["b_gate"].reshape(1, n_br * d), tm=tm, tn=tn, out_dtype=F32,
                    epilogue=jax.nn.sigmoid, name="gates")
    merged = _merge((o_a, o_b, o_c, o_d), gates, p["w_branch"], layer, _tile(seq, 512), tn)
    per_b = seq // tm
    x = _matmul(merged, p["w_out"], _layer_col_spec(layer, d, tn), d, d, zero(d), tm=tm, tn=tn,
                out_dtype=F32, epilogue=lambda acc, xr, g: xr + g * acc, extra=(x, m4),
                extra_specs=(pl.BlockSpec((tm, tn), lambda j, i: (i, j)),
                             pl.BlockSpec((None, None, 1, tn), lambda j, i: (i // per_b, 2, 0, j))),
                name="out_proj")

    h_bf, h_rows = _ada_norm(x, p["norm2_g"], m4, 4, 3, seq, (BF16, F32), token_rows=(False, True))
    return _moe(h_bf, h_rows, x, m4, p["w_router"], p["b_router"], p["w_gu"], p["b_gu"],
                p["w_dn"], p["b_dn"], layer, seq)


@jax.jit
def _forward(x, c, positions, w_mod, b_mod, ada_table, norm1_g, w_in, b_forget, diff_lambda,
             diff_norm_g, w_branch, w_gate, b_gate, w_out, norm2_g, w_router, b_router,
             w_gu, b_gu, w_dn, b_dn, final_norm_g):
    batch, seq, d = x.shape
    depth = w_in.shape[0]
    t = batch * seq
    xt = x.reshape(t, d)

    c8 = jnp.pad(c, ((0, 8 - batch), (0, 0)))
    n_mod = w_mod.shape[1]
    mod = _matmul(c8, w_mod, _col_spec(d, _tile(n_mod, 512)), d, n_mod, b_mod[None, :], tm=8,
                  tn=_tile(n_mod, 512), out_dtype=F32, prologue=lambda v: v * jax.nn.sigmoid(v),
                  name="mod_proj")[:batch]
    mod = mod.reshape(batch, N_MOD, d)

    pos = positions.reshape(t).astype(F32)
    tabs = (_rope_table(pos, HEAD_DIM // 2, LANES),
            _rope_table(pos, HEAD_DIM, LANES),
            jnp.concatenate([_rope_table(pos, HEAD_DIM, LANES), _rope_table(pos, IDX_DIM, LANES),
                             _rope_table(pos, IDX_DIM, IDX_DIM)], axis=1))

    w_in_t = jnp.swapaxes(w_in, 1, 2)
    for l in range(depth):
        m4 = (mod + ada_table[l][None])[:, :, None, :]
        lam_init = 0.8 - 0.6 * float(np.exp(-0.3 * l))
        p = dict(norm1_g=norm1_g[l][None, :], w_in_t=w_in_t, b_forget=b_forget[l],
                 diff_lambda=diff_lambda[l], diff_norm_g=diff_norm_g[l], w_branch=w_branch,
                 w_gate=w_gate, b_gate=b_gate[l], w_out=w_out, norm2_g=norm2_g[l][None, :],
                 w_router=w_router[l], b_router=b_router[l], w_gu=w_gu, b_gu=b_gu,
                 w_dn=w_dn, b_dn=b_dn)
        xt = _layer(xt, m4, tabs, lam_init, batch, seq, l, p)

    zeros4 = jnp.zeros((batch, 1, 1, d), F32)
    (out,) = _ada_norm(xt, final_norm_g[None, :], zeros4, 0, 0, seq, (F32,))
    return out.reshape(batch, seq, d)


def kernel(x, c, positions, w_mod, b_mod, ada_table, norm1_g, w_in, b_forget, diff_lambda,
           diff_norm_g, w_branch, w_gate, b_gate, w_out, norm2_g, w_router, b_router,
           w_gu, b_gu, w_dn, b_dn, final_norm_g):
    return _forward(x, c, positions, w_mod, b_mod, ada_table, norm1_g, w_in, b_forget, diff_lambda,
                    diff_norm_g, w_branch, w_gate, b_gate, w_out, norm2_g, w_router, b_router,
                    w_gu, b_gu, w_dn, b_dn, final_norm_g)
```

```python
import functools

import numpy as np
import jax
import jax.numpy as jnp
from jax import lax
from jax.experimental import pallas as pl
from jax.experimental.pallas import tpu as pltpu

F32 = jnp.float32
BF16 = jnp.bfloat16
I32 = jnp.int32

LANES = 128
HEAD_DIM = 128
CHUNK = 64
IDX_HEADS = 8
IDX_DIM = 64
TOPK_MAX = 256
ROPE_THETA = 500000.0
ROPE_FRACTION = 4
TOP_K = 4
SWIGLU_LIMIT = 7.0
SWIGLU_ALPHA = 1.702
RMS_EPS = 1e-5
N_MOD = 6
NEG = -1e30
LOG2E = 1.4426950408889634
INT_MIN = -(2 ** 31)
ROW_PITCH_PAD = 4
VMEM_LIMIT = 56 * 1024 * 1024


def _params(n_axes, vmem=VMEM_LIMIT):
    return pltpu.CompilerParams(dimension_semantics=("arbitrary",) * n_axes,
                                vmem_limit_bytes=vmem)


def _tile(n, pref):
    t = min(n, pref)
    while n % t:
        t //= 2
    return t


def _row_pitch(nseg):
    return nseg + ROW_PITCH_PAD


def _store_token_rows(o_ref, val, tm):
    nseg = val.shape[1] // LANES
    pitch = _row_pitch(nseg)
    for s in range(nseg):
        o_ref[pl.ds(s, tm, stride=pitch), :] = val[:, s * LANES:(s + 1) * LANES]
    for s in range(nseg, pitch):
        o_ref[pl.ds(s, tm, stride=pitch), :] = jnp.zeros((tm, LANES), o_ref.dtype)


def _load_token_rows(ref, tm, nseg, lead=()):
    return [ref[(*lead, pl.ds(s, tm, stride=_row_pitch(nseg)), slice(None))] for s in range(nseg)]


def _norm_kernel(x_ref, g_ref, sc_ref, sh_ref, *o_refs, tm, token_rows):
    x = x_ref[...]
    var = jnp.mean(x * x, axis=-1, keepdims=True)
    y = x * lax.rsqrt(var + RMS_EPS) * g_ref[...]
    h = y * (1.0 + sc_ref[...]) + sh_ref[...]
    for o, rows in zip(o_refs, token_rows):
        if rows:
            _store_token_rows(o, h, tm)
        else:
            o[...] = h.astype(o.dtype)


def _ada_norm(x, g, m4, sc_idx, sh_idx, seq, out_dtypes, token_rows=None):
    t, d = x.shape
    tm = _tile(seq, 256)
    per_b = seq // tm
    pitch = _row_pitch(d // LANES)
    token_rows = token_rows or (False,) * len(out_dtypes)
    out_specs, out_shape = [], []
    for dt, rows in zip(out_dtypes, token_rows):
        if rows:
            out_specs.append(pl.BlockSpec((tm * pitch, LANES), lambda i: (i, 0)))
            out_shape.append(jax.ShapeDtypeStruct((t * pitch, LANES), F32))
        else:
            out_specs.append(pl.BlockSpec((tm, d), lambda i: (i, 0)))
            out_shape.append(jax.ShapeDtypeStruct((t, d), dt))
    return pl.pallas_call(
        functools.partial(_norm_kernel, tm=tm, token_rows=token_rows),
        grid=(t // tm,),
        in_specs=[
            pl.BlockSpec((tm, d), lambda i: (i, 0)),
            pl.BlockSpec((1, d), lambda i: (0, 0)),
            pl.BlockSpec((None, None, 1, d), lambda i: (i // per_b, sc_idx, 0, 0)),
            pl.BlockSpec((None, None, 1, d), lambda i: (i // per_b, sh_idx, 0, 0)),
        ],
        out_specs=out_specs,
        out_shape=out_shape,
        compiler_params=_params(1),
        name="ada_norm",
    )(x, g, m4, m4)


def _mm_kernel(*refs, n_extra, epilogue, prologue, cast_rows, w_rows_are_outputs):
    x_ref, w_ref, b_ref = refs[:3]
    extra = refs[3:3 + n_extra]
    o_ref = refs[3 + n_extra]
    wbf_ref = refs[4 + n_extra]
    if len(w_ref.shape) == 3:
        w_ref = w_ref.at[0]

    @pl.when(pl.program_id(1) == 0)
    def _():
        def body(r, c):
            rows = pl.ds(pl.multiple_of(r * cast_rows, cast_rows), cast_rows)
            wbf_ref[rows, :] = w_ref[rows, :].astype(BF16)
            return c

        lax.fori_loop(0, w_ref.shape[0] // cast_rows, body, 0)

    x = x_ref[...]
    if prologue is not None:
        x = prologue(x)
    x = x.astype(BF16)
    if w_rows_are_outputs:
        acc = _qk(x, wbf_ref[...])
    else:
        acc = jnp.dot(x, wbf_ref[...], preferred_element_type=F32)
    o_ref[...] = epilogue(acc + b_ref[...], *[e[...] for e in extra]).astype(o_ref.dtype)


def _matmul(x, w, w_spec, k, n, bias, *, tm, tn, out_dtype, epilogue=None, extra=(),
            extra_specs=(), prologue=None, w_rows_are_outputs=False, name="matmul"):
    m = x.shape[0]
    if epilogue is None:
        epilogue = lambda acc: acc
    w_block = (tn, k) if w_rows_are_outputs else (k, tn)
    kern = functools.partial(_mm_kernel, n_extra=len(extra), epilogue=epilogue, prologue=prologue,
                             cast_rows=_tile(w_block[0], 256), w_rows_are_outputs=w_rows_are_outputs)
    return pl.pallas_call(
        kern,
        grid=(n // tn, m // tm),
        in_specs=[pl.BlockSpec((tm, k), lambda j, i: (i, 0)),
                  w_spec,
                  pl.BlockSpec((1, tn), lambda j, i: (0, j)),
                  *extra_specs],
        out_specs=pl.BlockSpec((tm, tn), lambda j, i: (i, j)),
        out_shape=jax.ShapeDtypeStruct((m, n), out_dtype),
        scratch_shapes=[pltpu.VMEM(w_block, BF16)],
        compiler_params=_params(2),
        name=name,
    )(x, w, bias, *extra)


def _rope_epilogue(acc, tab, *, groups):
    outs = []
    for g, kind in enumerate(groups):
        xg = acc[:, g * LANES:(g + 1) * LANES]
        if kind is not None:
            slot, half = kind
            base = 3 * slot * LANES
            cos = tab[:, base:base + LANES]
            sin_lo = tab[:, base + LANES:base + 2 * LANES]
            sin_hi = tab[:, base + 2 * LANES:base + 3 * LANES]
            xg = (xg * cos + pltpu.roll(xg, half, 1) * sin_lo
                  + pltpu.roll(xg, LANES - half, 1) * sin_hi)
        outs.append(xg)
    return outs[0] if len(outs) == 1 else jnp.concatenate(outs, axis=1)


def _rope_table(pos, period, active):
    rot = period // ROPE_FRACTION
    half = rot // 2
    inv_freq = jnp.float32(ROPE_THETA) ** (-jnp.arange(half, dtype=F32) / half)
    lane = np.arange(LANES)
    r = lane % period
    on = lane < active
    first = on & (r < half)
    second = on & (r >= half) & (r < rot)
    fidx = np.where(first, r, np.where(second, r - half, 0))
    ang = pos[:, None] * inv_freq[fidx][None, :]
    cos, sin = jnp.cos(ang), jnp.sin(ang)
    c = jnp.where((first | second)[None, :], cos, 1.0)
    s_lo = jnp.where(second[None, :], sin, 0.0)
    s_hi = jnp.where(first[None, :], -sin, 0.0)
    return jnp.concatenate([c, s_lo, s_hi], axis=1)


def _qk(q, k):
    return lax.dot_general(q, k, (((1,), (1,)), ((), ())), preferred_element_type=F32)


def _softmax_tile(carry, s, v):
    m, l, acc = carry
    m_new = jnp.maximum(m, jnp.max(s, axis=-1, keepdims=True))
    alpha = jnp.exp2(m - m_new)
    p = jnp.exp2(s - m_new)
    l = alpha * l + jnp.sum(p, axis=-1, keepdims=True)
    acc = alpha * acc + jnp.dot(p.astype(BF16), v, preferred_element_type=F32)
    return m_new, l, acc


def _softmax_init(rows):
    return (jnp.full((rows, 1), NEG, F32), jnp.zeros((rows, 1), F32),
            jnp.zeros((rows, HEAD_DIM), F32))


def _rows(ref, j, t):
    return ref[pl.ds(pl.multiple_of(j * t, t), t), :]


def _attn_diff_kernel(dl_ref, g_ref, q_ref, k_ref, v_ref, o_ref, *, tq, seq, lam_init):
    half = HEAD_DIM // 2
    scale2 = half ** -0.5 * LOG2E
    lane = lax.broadcasted_iota(I32, (tq, HEAD_DIM), 1)
    r = lax.broadcasted_iota(I32, (2 * tq, tq), 0)
    r = jnp.where(r >= tq, r - tq, r)
    c = lax.broadcasted_iota(I32, (2 * tq, tq), 1)
    vis = c < (r // CHUNK + 1) * CHUNK
    dl = dl_ref[...]
    lam = (jnp.exp(jnp.sum(dl[0:1] * dl[1:2], axis=-1, keepdims=True))
           - jnp.exp(jnp.sum(dl[2:3] * dl[3:4], axis=-1, keepdims=True)) + lam_init)

    for i in range(seq // tq):
        q = q_ref[i * tq:(i + 1) * tq, :]
        zero = jnp.zeros_like(q)
        qz = jnp.concatenate([jnp.where(lane < half, q, zero), jnp.where(lane >= half, q, zero)], axis=0)
        carry = _softmax_init(2 * tq)
        for j in range(i + 1):
            s = _qk(qz, k_ref[j * tq:(j + 1) * tq, :]) * scale2
            if j == i:
                s = jnp.where(vis, s, NEG)
            carry = _softmax_tile(carry, s, v_ref[j * tq:(j + 1) * tq, :])
        _, l, acc = carry
        o = acc / l
        o = o[:tq] - lam * o[tq:]
        var = jnp.mean(o * o, axis=-1, keepdims=True)
        o_ref[i * tq:(i + 1) * tq, :] = (o * lax.rsqrt(var + RMS_EPS) * g_ref[...]
                                         * (1.0 - lam_init)).astype(o_ref.dtype)


def _attn_forget_kernel(dq_ref, dk_ref, q_ref, k_ref, v_ref, o_ref, *, tq, seq):
    scale2 = HEAD_DIM ** -0.5 * LOG2E
    r = lax.broadcasted_iota(I32, (tq, tq), 0)
    c = lax.broadcasted_iota(I32, (tq, tq), 1)
    for i in range(seq // tq):
        q = q_ref[i * tq:(i + 1) * tq, :]
        dq2 = dq_ref[i * tq:(i + 1) * tq, :] * LOG2E
        carry = _softmax_init(tq)
        for j in range(i + 1):
            dk2 = dk_ref[:, j * tq:(j + 1) * tq] * LOG2E
            s = _qk(q, k_ref[j * tq:(j + 1) * tq, :]) * scale2 + dq2 - dk2
            if j == i:
                s = jnp.where(c <= r, s, NEG)
            carry = _softmax_tile(carry, s, v_ref[j * tq:(j + 1) * tq, :])
        _, l, acc = carry
        o_ref[i * tq:(i + 1) * tq, :] = (acc / l).astype(o_ref.dtype)


def _log_sigmoid(z):
    return jnp.minimum(z, 0.0) - jnp.log(1.0 + jnp.exp(-jnp.abs(z)))


def _attn_stick_kernel(q_ref, k_ref, v_ref, o_ref, *, tq, seq):
    scale = HEAD_DIM ** -0.5
    r = lax.broadcasted_iota(I32, (tq, tq), 0)
    c = lax.broadcasted_iota(I32, (tq, tq), 1)
    cb = min(tq, 256)
    later = jnp.where(lax.broadcasted_iota(I32, (cb, cb), 0) > lax.broadcasted_iota(I32, (cb, cb), 1),
                      1.0, 0.0).astype(BF16)

    def tile(q, j, carry, mask):
        suffix, acc = carry
        z = _qk(q, k_ref[j * tq:(j + 1) * tq, :]) * scale
        log_beta = _log_sigmoid(z)
        log_keep = log_beta - z
        if mask is not None:
            log_keep = jnp.where(mask, log_keep, 0.0)
        parts = [None] * (tq // cb)
        for blk in reversed(range(tq // cb)):
            lk = log_keep[:, blk * cb:(blk + 1) * cb]
            hi = lk.astype(BF16)
            lo = (lk - hi.astype(F32)).astype(BF16)
            parts[blk] = (jnp.dot(hi, later, preferred_element_type=F32)
                          + jnp.dot(lo, later, preferred_element_type=F32) + suffix)
            suffix = suffix + jnp.sum(lk, axis=-1, keepdims=True)
        after = parts[0] if len(parts) == 1 else jnp.concatenate(parts, axis=1)
        w = jnp.exp(log_beta + after)
        if mask is not None:
            w = jnp.where(mask, w, 0.0)
        acc = acc + jnp.dot(w.astype(BF16), v_ref[j * tq:(j + 1) * tq, :], preferred_element_type=F32)
        return suffix, acc

    for i in range(seq // tq):
        q = q_ref[i * tq:(i + 1) * tq, :]
        carry = tile(q, i, (jnp.zeros((tq, 1), F32), jnp.zeros((tq, HEAD_DIM), F32)), c < r)
        for j in reversed(range(i)):
            carry = tile(q, j, carry, None)
        o_ref[i * tq:(i + 1) * tq, :] = carry[1].astype(o_ref.dtype)


def _attention(kernel, q_arr, q_col, k_arr, k_col, v_arr, v_col, heads, batch, seq,
               pre=(), pre_specs=(), name="attention"):
    t = batch * seq
    return pl.pallas_call(
        kernel,
        grid=(batch, heads),
        in_specs=[*pre_specs,
                  pl.BlockSpec((seq, HEAD_DIM), lambda b, h: (b, q_col + h)),
                  pl.BlockSpec((seq, HEAD_DIM), lambda b, h: (b, k_col + h)),
                  pl.BlockSpec((seq, HEAD_DIM), lambda b, h: (b, v_col + h))],
        out_specs=pl.BlockSpec((seq, HEAD_DIM), lambda b, h: (b, h)),
        out_shape=jax.ShapeDtypeStruct((t, heads * HEAD_DIM), BF16),
        compiler_params=_params(2),
        name=name,
    )(*pre, q_arr, k_arr, v_arr)


def _sortable(x):
    bits = pltpu.bitcast(x + 0.0, I32)
    return jnp.where(bits < 0, bits ^ jnp.int32(0x7FFFFFFF), bits)


def _attn_sparse_kernel(q_ref, sq_ref, kv_ref, ik_ref, o_ref, key_ref, thr_ref, *, tq, seq, heads,
                        topk, iq_col, iw_col):
    i = pl.program_id(1)
    scale = HEAD_DIM ** -0.5
    n_tiles = seq // tq
    q0 = i * tq

    sq = sq_ref[...]
    lane_k = lax.broadcasted_iota(I32, (tq, LANES), 1)
    iq_groups = [sq[:, iq_col + g * LANES: iq_col + (g + 1) * LANES].astype(BF16)
                 for g in range(IDX_HEADS * IDX_DIM // LANES)]
    iw = [sq[:, iw_col + h: iw_col + h + 1] for h in range(IDX_HEADS)]
    row = lax.broadcasted_iota(I32, (tq, tq), 0)
    col = lax.broadcasted_iota(I32, (tq, tq), 1)
    chunk_end = ((q0 + row) // CHUNK + 1) * CHUNK

    key_ref[...] = jnp.full((tq, seq), INT_MIN, I32)

    def score_tile(j, c):
        ikt = _rows(ik_ref, j, tq)
        ik_lo = jnp.where(lane_k < IDX_DIM, ikt, 0.0)
        ik_hi = pltpu.roll(ik_lo, IDX_DIM, 1)
        ik_lo = ik_lo.astype(BF16)
        ik_hi = ik_hi.astype(BF16)
        score = jnp.zeros((tq, tq), F32)
        for h in range(IDX_HEADS):
            s = _qk(iq_groups[h // 2], ik_hi if h % 2 else ik_lo)
            score = score + iw[h] * jnp.maximum(s, 0.0)
        vis = (j * tq + col) < chunk_end
        key_ref[:, pl.ds(pl.multiple_of(j * tq, tq), tq)] = jnp.where(vis, _sortable(score), INT_MIN)
        return c

    lax.fori_loop(0, i + 1, score_tile, 0)

    def count_ge(cand, ncols=seq):
        return jnp.sum(jnp.where(key_ref[:, :ncols] >= cand, 1.0, 0.0), axis=-1, keepdims=True)

    kf = jnp.float32(topk)

    def find_threshold(ncols):
        thr = jnp.where(count_ge(jnp.zeros((tq, 1), I32), ncols) >= kf, 0, INT_MIN).astype(I32)

        def bisect(b, thr):
            cand = thr | lax.shift_left(jnp.int32(1), 30 - b)
            return jnp.where(count_ge(cand, ncols) >= kf, cand, thr)

        thr = lax.fori_loop(0, jnp.where((i + 1) * tq <= topk, 0, 31), bisect, thr)
        thr_ref[...] = jnp.maximum(thr, INT_MIN + 1)

    lo = 0
    for hi in sorted({min(n_tiles, v) for v in range(2, n_tiles + 2, 2)}):
        pl.when((i + 1 > lo) & (i + 1 <= hi))(functools.partial(find_threshold, hi * tq))
        lo = hi
    thr = thr_ref[...]
    n_ge = count_ge(thr)

    @pl.when(jnp.max(n_ge) > kf)
    def _():
        need = kf - jnp.sum(jnp.where(key_ref[...] > thr, 1.0, 0.0), axis=-1, keepdims=True)
        upto = jnp.where(row <= col, 1.0, 0.0).astype(BF16)

        def fix(j, seen):
            cols = pl.ds(pl.multiple_of(j * tq, tq), tq)
            kt = key_ref[:, cols]
            eq = kt == thr
            rank = jnp.dot(jnp.where(eq, 1.0, 0.0).astype(BF16), upto, preferred_element_type=F32) + seen
            key_ref[:, cols] = jnp.where(eq & (rank > need), INT_MIN, kt)
            return seen + jnp.sum(jnp.where(eq, 1.0, 0.0), axis=-1, keepdims=True)

        lax.fori_loop(0, n_tiles, fix, jnp.zeros((tq, 1), F32))

    qs = jnp.concatenate([q_ref[:, h * HEAD_DIM:(h + 1) * HEAD_DIM] for h in range(heads)], axis=0)

    def attn_tile(j, carry):
        kvt = _rows(kv_ref, j, tq)
        kt = kvt[:, :HEAD_DIM].astype(BF16)
        vt = kvt[:, HEAD_DIM:].astype(BF16)
        bias = jnp.where(key_ref[:, pl.ds(pl.multiple_of(j * tq, tq), tq)] >= thr, 0.0, NEG)
        s = _qk(qs, kt) * (scale * LOG2E) + jnp.concatenate([bias] * heads, axis=0)
        return _softmax_tile(carry, s, vt)

    _, l, acc = lax.fori_loop(0, i + 1, attn_tile, _softmax_init(heads * tq))
    o = acc / l
    for h in range(heads):
        o_ref[:, h * HEAD_DIM:(h + 1) * HEAD_DIM] = o[h * tq:(h + 1) * tq].astype(o_ref.dtype)


def _attn_sparse(dq, small, heads, batch, seq, tq, topk, iq_col, ik_col, iw_col):
    nq = seq // tq
    t = batch * seq
    bw = heads * HEAD_DIM
    w = small.shape[1]
    kern = functools.partial(_attn_sparse_kernel, tq=tq, seq=seq, heads=heads, topk=topk,
                             iq_col=iq_col, iw_col=iw_col)
    return pl.pallas_call(
        kern,
        grid=(batch, nq),
        in_specs=[pl.BlockSpec((tq, bw), lambda b, i: (b * nq + i, 0)),
                  pl.BlockSpec((tq, w), lambda b, i: (b * nq + i, 0)),
                  pl.BlockSpec((seq, 2 * HEAD_DIM), lambda b, i: (b, 0)),
                  pl.BlockSpec((seq, LANES), lambda b, i: (b, ik_col // LANES))],
        out_specs=pl.BlockSpec((tq, bw), lambda b, i: (b * nq + i, 0)),
        out_shape=jax.ShapeDtypeStruct((t, bw), BF16),
        scratch_shapes=[pltpu.VMEM((tq, seq), I32), pltpu.VMEM((tq, 1), I32)],
        compiler_params=_params(2),
        name="attn_sparse",
    )(dq, small, small, small)


def _forget_cumsum_kernel(x_ref, b_ref, o_ref, *, seq, blk):
    r = lax.broadcasted_iota(I32, (blk, blk), 0)
    c = lax.broadcasted_iota(I32, (blk, blk), 1)
    tri = jnp.where(c <= r, 1.0, 0.0).astype(BF16)
    carry = jnp.zeros((1, LANES), F32)
    for s in range(seq // blk):
        logf = _log_sigmoid(x_ref[s * blk:(s + 1) * blk, :] + b_ref[...])
        hi = logf.astype(BF16)
        rem = logf - hi.astype(F32)
        mid = rem.astype(BF16)
        lo = (rem - mid.astype(F32)).astype(BF16)
        local = (jnp.dot(tri, hi, preferred_element_type=F32)
                 + jnp.dot(tri, mid, preferred_element_type=F32)
                 + jnp.dot(tri, lo, preferred_element_type=F32)) + carry
        o_ref[s * blk:(s + 1) * blk, :] = local
        carry = local[blk - 1:blk, :]


def _forget_cumsum(small, bias_row, batch, seq, col):
    return pl.pallas_call(
        functools.partial(_forget_cumsum_kernel, seq=seq, blk=_tile(seq, 256)),
        grid=(batch,),
        in_specs=[pl.BlockSpec((seq, LANES), lambda b: (b, col // LANES)),
                  pl.BlockSpec((1, LANES), lambda b: (0, 0))],
        out_specs=pl.BlockSpec((seq, LANES), lambda b: (b, 0)),
        out_shape=jax.ShapeDtypeStruct((batch * seq, LANES), F32),
        compiler_params=_params(1),
        name="forget_cumsum",
    )(small, bias_row)


def _merge_kernel(*refs, n_br, cast_rows):
    o_refs = refs[:n_br]
    g_refs = refs[n_br:2 * n_br]
    w_ref = refs[2 * n_br]
    out_ref = refs[2 * n_br + 1]
    wbf_ref = refs[2 * n_br + 2]

    @pl.when(pl.program_id(1) == 0)
    def _():
        for b in range(n_br):
            def body(r, c, b=b):
                rows = pl.ds(pl.multiple_of(r * cast_rows, cast_rows), cast_rows)
                wbf_ref[b, rows, :] = w_ref[b, rows, :].astype(BF16)
                return c
            lax.fori_loop(0, w_ref.shape[1] // cast_rows, body, 0)

    acc = None
    for b in range(n_br):
        term = g_refs[b][...] * jnp.dot(o_refs[b][...], wbf_ref[b], preferred_element_type=F32)
        acc = term if acc is None else acc + term
    out_ref[...] = acc.astype(out_ref.dtype)


def _merge(outs, gates, w_branch, layer, tm, tn):
    _, n_br, bw, d = w_branch.shape
    t = outs[0].shape[0]
    nd = d // tn
    kern = functools.partial(_merge_kernel, n_br=n_br, cast_rows=_tile(bw, 256))
    return pl.pallas_call(
        kern,
        grid=(nd, t // tm),
        in_specs=[*[pl.BlockSpec((tm, bw), lambda j, i: (i, 0)) for _ in range(n_br)],
                  *[pl.BlockSpec((tm, tn), lambda j, i, b=b: (i, b * nd + j)) for b in range(n_br)],
                  pl.BlockSpec((None, n_br, bw, tn), lambda j, i: (layer, 0, 0, j))],
        out_specs=pl.BlockSpec((tm, tn), lambda j, i: (i, j)),
        out_shape=jax.ShapeDtypeStruct((t, d), BF16),
        scratch_shapes=[pltpu.VMEM((n_br, bw, tn), BF16)],
        compiler_params=_params(2),
        name="gated_merge",
    )(*outs, *([gates] * n_br), w_branch)


def _router_kernel(h_ref, w_ref, b_ref, meta_i_ref, meta_w_ref, cnt_ref, run_ref, *, tm, n_exp):
    @pl.when(pl.program_id(0) == 0)
    def _():
        run_ref[...] = jnp.zeros_like(run_ref)

    lane = lax.broadcasted_iota(I32, (tm, LANES), 1)
    lane_f = lane.astype(F32)
    logits = jnp.dot(h_ref[...], w_ref[...].astype(BF16), preferred_element_type=F32) + b_ref[...]
    logits = jnp.where(lane < n_exp, logits, -jnp.inf)

    vals, idxs = [], []
    work = logits
    for _ in range(TOP_K):
        v = jnp.max(work, axis=-1, keepdims=True)
        ix = jnp.min(jnp.where(work == v, lane_f, float(LANES)), axis=-1, keepdims=True).astype(I32)
        vals.append(v)
        idxs.append(ix)
        work = jnp.where(lane == ix, -jnp.inf, work)
    es = [jnp.exp(v - vals[0]) for v in vals]
    denom = es[0] + es[1] + es[2] + es[3]

    onehots = [jnp.where(lane == ix, 1.0, 0.0) for ix in idxs]
    sel = onehots[0] + onehots[1] + onehots[2] + onehots[3]
    r = lax.broadcasted_iota(I32, (tm, tm), 0)
    c = lax.broadcasted_iota(I32, (tm, tm), 1)
    before = jnp.where(c < r, 1.0, 0.0).astype(BF16)
    rank = jnp.dot(before, sel.astype(BF16), preferred_element_type=F32) + run_ref[...]
    run_ref[...] = run_ref[...] + jnp.sum(sel, axis=0, keepdims=True)

    meta_i = jnp.zeros((tm, LANES), I32)
    meta_w = jnp.zeros((tm, LANES), F32)
    for k in range(TOP_K):
        rk = jnp.sum(onehots[k] * rank, axis=-1, keepdims=True).astype(I32)
        meta_i = jnp.where(lane == k, idxs[k], meta_i)
        meta_i = jnp.where(lane == TOP_K + k, rk, meta_i)
        meta_w = jnp.where(lane == k, es[k] / denom, meta_w)
    meta_i_ref[...] = meta_i
    meta_w_ref[...] = meta_w
    cnt_ref[...] = run_ref[...]


def _router(h, w_router_pad, b_router_pad, n_exp):
    t, d = h.shape
    tm = _tile(t, 256)
    kern = functools.partial(_router_kernel, tm=tm, n_exp=n_exp)
    return pl.pallas_call(
        kern,
        grid=(t // tm,),
        in_specs=[pl.BlockSpec((tm, d), lambda i: (i, 0)),
                  pl.BlockSpec((d, LANES), lambda i: (0, 0)),
                  pl.BlockSpec((1, LANES), lambda i: (0, 0))],
        out_specs=[pl.BlockSpec((tm, LANES), lambda i: (i, 0)),
                   pl.BlockSpec((tm, LANES), lambda i: (i, 0)),
                   pl.BlockSpec((1, LANES), lambda i: (0, 0))],
        out_shape=[jax.ShapeDtypeStruct((t, LANES), I32),
                   jax.ShapeDtypeStruct((t, LANES), F32),
                   jax.ShapeDtypeStruct((1, LANES), F32)],
        scratch_shapes=[pltpu.VMEM((1, LANES), F32)],
        compiler_params=_params(1),
        name="router",
    )(h, w_router_pad, b_router_pad)


def _ffn_kernel(te_ref, nu_ref, tok_ref, h_ref, wgu_ref, bgu_ref, wdn_ref, bdn_ref, o_ref,
                xbuf, pick_ref, sem, *, tm, nseg, kc, nc):
    i = pl.program_id(0)
    n_used = nu_ref[0]
    d, ff = wgu_ref.shape[0], wdn_ref.shape[0]
    pitch = _row_pitch(nseg)

    def fetch(tile, slot):
        def body(r, c):
            tok = tok_ref[tile * tm + r]
            pltpu.make_async_copy(h_ref.at[pl.ds(tok * pitch, nseg), :],
                                  xbuf.at[slot, pl.ds(r * pitch, nseg), :], sem.at[slot]).start()
            return c
        lax.fori_loop(0, tm, body, 0)

    @pl.when(i == 0)
    def _():
        r = lax.broadcasted_iota(I32, (2 * ff, ff), 0)
        c = lax.broadcasted_iota(I32, (2 * ff, ff), 1)
        pick_ref[...] = jnp.where(r == 2 * c, 1.0, 0.0).astype(BF16)
        fetch(0, 0)

    @pl.when(i + 1 < n_used)
    def _():
        fetch(i + 1, (i + 1) % 2)

    @pl.when(i < n_used)
    def _():
        slot = i % 2
        done = xbuf.at[slot, pl.ds(0, tm * nseg), :]
        pltpu.make_async_copy(done, done, sem.at[slot]).wait()
        parts = _load_token_rows(xbuf, tm, nseg, lead=(slot,))
        per = kc // LANES
        gu = bgu_ref[...]
        for c in range(d // kc):
            xc = jnp.concatenate(parts[c * per:(c + 1) * per], axis=1).astype(BF16)
            gu = gu + jnp.dot(xc, wgu_ref[c * kc:(c + 1) * kc, :].astype(BF16),
                              preferred_element_type=F32)
        lin = pltpu.roll(gu, 2 * ff - 1, 1)
        glu = jnp.minimum(gu, SWIGLU_LIMIT)
        lin = jnp.clip(lin, -SWIGLU_LIMIT, SWIGLU_LIMIT)
        act = glu * jax.nn.sigmoid(SWIGLU_ALPHA * glu) * (lin + 1.0)
        act = jnp.dot(act.astype(BF16), pick_ref[...], preferred_element_type=F32).astype(BF16)
        for c in range(d // nc):
            cols = slice(c * nc, (c + 1) * nc)
            o_ref[:, cols] = (jnp.dot(act, wdn_ref[:, cols].astype(BF16), preferred_element_type=F32)
                              + bdn_ref[:, cols])

    @pl.when(i >= n_used)
    def _():
        o_ref[...] = jnp.zeros_like(o_ref)


def _expert_ffn(h_rows, tile_expert, n_used, token_of_row, w_gu, b_gu, w_dn, b_dn, layer, tm):
    _, n_exp, d, ff2 = w_gu.shape
    ff = ff2 // 2
    nseg = d // LANES
    n_tiles = token_of_row.shape[0] // tm
    once = pl.Buffered(1)

    def wsel(i, te, nu, tok):
        return (layer, te[i], 0, 0)

    kern = functools.partial(_ffn_kernel, tm=tm, nseg=nseg, kc=_tile(d, 512), nc=_tile(d, 1024))
    return pl.pallas_call(
        kern,
        grid_spec=pltpu.PrefetchScalarGridSpec(
            num_scalar_prefetch=3,
            grid=(n_tiles,),
            in_specs=[pl.BlockSpec(memory_space=pl.ANY),
                      pl.BlockSpec((None, None, d, ff2), wsel, pipeline_mode=once),
                      pl.BlockSpec((None, None, 1, ff2), wsel),
                      pl.BlockSpec((None, None, ff, d), wsel, pipeline_mode=once),
                      pl.BlockSpec((None, None, 1, d), wsel)],
            out_specs=pl.BlockSpec((tm, d), lambda i, te, nu, tok: (i, 0)),
            scratch_shapes=[pltpu.VMEM((2, tm * _row_pitch(nseg), LANES), F32),
                            pltpu.VMEM((ff2, ff), BF16),
                            pltpu.SemaphoreType.DMA((2,))]),
        out_shape=jax.ShapeDtypeStruct((n_tiles * tm, d), F32),
        compiler_params=_params(1),
        name="expert_ffn",
    )(tile_expert, n_used, token_of_row, h_rows, w_gu, b_gu, w_dn, b_dn)


def _combine_kernel(pos_ref, ys_ref, x_ref, g_ref, w_ref, o_ref, buf, sem, *, tm):
    i = pl.program_id(0)

    def fetch(tile, slot):
        def body(t, c):
            for k in range(TOP_K):
                row = pos_ref[(tile * tm + t) * TOP_K + k]
                pltpu.make_async_copy(ys_ref.at[pl.ds(row, 1), :],
                                      buf.at[slot, k, pl.ds(t, 1), :], sem.at[slot]).start()
            return c
        lax.fori_loop(0, tm, body, 0)

    @pl.when(i == 0)
    def _():
        fetch(0, 0)

    @pl.when(i + 1 < pl.num_programs(0))
    def _():
        fetch(i + 1, (i + 1) % 2)

    slot = i % 2
    pltpu.make_async_copy(buf.at[slot], buf.at[slot], sem.at[slot]).wait()
    w = w_ref[...]
    y = w[:, 0:1] * buf[slot, 0]
    for k in range(1, TOP_K):
        y = y + w[:, k:k + 1] * buf[slot, k]
    o_ref[...] = x_ref[...] + g_ref[...] * y


def _combine(ys, pos_flat, x, m4, gate_idx, meta_w, seq, tm):
    t, d = x.shape
    per_b = seq // tm
    return pl.pallas_call(
        functools.partial(_combine_kernel, tm=tm),
        grid_spec=pltpu.PrefetchScalarGridSpec(
            num_scalar_prefetch=1,
            grid=(t // tm,),
            in_specs=[pl.BlockSpec(memory_space=pl.ANY),
                      pl.BlockSpec((tm, d), lambda i, p: (i, 0)),
                      pl.BlockSpec((None, None, 1, d), lambda i, p: (i // per_b, gate_idx, 0, 0)),
                      pl.BlockSpec((tm, LANES), lambda i, p: (i, 0))],
            out_specs=pl.BlockSpec((tm, d), lambda i, p: (i, 0)),
            scratch_shapes=[pltpu.VMEM((2, TOP_K, tm, d), F32),
                            pltpu.SemaphoreType.DMA((2,))]),
        out_shape=jax.ShapeDtypeStruct((t, d), F32),
        compiler_params=_params(1),
        name="moe_combine",
    )(pos_flat, ys, x, m4, meta_w)


def _moe(h_bf, h_rows, x, m4, w_router, b_router, w_gu, b_gu, w_dn, b_dn, layer, seq):
    t, d = x.shape
    n_exp = w_router.shape[1]
    tm = _tile(t, 256)
    wr = jnp.pad(w_router, ((0, 0), (0, LANES - n_exp)))
    br = jnp.pad(b_router, (0, LANES - n_exp))[None, :]
    meta_i, meta_w, counts = _router(h_bf, wr, br, n_exp)

    counts = counts[0, :n_exp].astype(I32)
    padded = ((counts + tm - 1) // tm) * tm
    ends = jnp.cumsum(padded)
    starts = ends - padded
    n_rows = t * TOP_K + n_exp * tm
    n_tiles = n_rows // tm
    experts = meta_i[:, :TOP_K]
    pos = starts[experts] + meta_i[:, TOP_K:2 * TOP_K]
    tile_start = jnp.arange(n_tiles, dtype=I32) * tm
    tile_expert = jnp.minimum(jnp.sum((ends[None, :] <= tile_start[:, None]).astype(I32), axis=1),
                              n_exp - 1)
    n_used = (ends[-1] // tm).astype(I32)[None]
    token_of_row = jnp.zeros((n_rows,), I32).at[pos.reshape(-1)].set(
        jnp.repeat(jnp.arange(t, dtype=I32), TOP_K))

    ys = _expert_ffn(h_rows, tile_expert, n_used, token_of_row, w_gu, b_gu[:, :, None, :], w_dn,
                     b_dn[:, :, None, :], layer, tm)
    return _combine(ys, pos.reshape(-1), x, m4, 5, meta_w, seq, _tile(seq, 64))


def _col_spec(k, tn, off_blocks=0):
    return pl.BlockSpec((k, tn), lambda j, i: (0, j + off_blocks))


def _layer_col_spec(layer, k, tn, off_blocks=0):
    return pl.BlockSpec((None, k, tn), lambda j, i: (layer, 0, j + off_blocks))


def _layer(x, m4, tabs, lam_init, batch, seq, layer, p):
    t, d = x.shape
    bw = p["w_branch"].shape[2]
    heads = bw // HEAD_DIM
    tm = _tile(seq, 1024)
    tn = _tile(bw, 512)
    w_in_t = p["w_in_t"]
    zero = lambda n: jnp.zeros((1, n), F32)

    (h,) = _ada_norm(x, p["norm1_g"], m4, 1, 0, seq, (BF16,))

    tab8, tab16, tab_small = tabs
    tab_spec = lambda w: pl.BlockSpec((tm, w), lambda j, i: (i, 0))
    rope8 = functools.partial(_rope_epilogue, groups=((0, 8),) * (tn // LANES))
    rope16 = functools.partial(_rope_epilogue, groups=((0, 16),) * (tn // LANES))

    def in_proj(row0, n, name, **kw):
        spec = pl.BlockSpec((pl.Element(1), pl.Element(tn), pl.Element(d)),
                            lambda j, i: (layer, pl.multiple_of(row0 + j * tn, 8), 0))
        return _matmul(h, w_in_t, spec, d, n, zero(n), tm=tm, tn=tn, out_dtype=BF16,
                       w_rows_are_outputs=True, name=name, **kw)

    c0 = 6 * bw + heads
    qk_a = in_proj(0, 2 * bw, "proj_a_qk", epilogue=rope8, extra=(tab8,),
                   extra_specs=(tab_spec(3 * LANES),))
    mid = in_proj(2 * bw, 4 * bw, "proj_av_b")
    qkv_c = in_proj(c0, 3 * bw, "proj_c")
    q_d = in_proj(c0 + 3 * bw, bw, "proj_d_q", epilogue=rope16, extra=(tab16,),
                  extra_specs=(tab_spec(3 * LANES),))
    s0 = c0 + 4 * bw
    n_iq = IDX_HEADS * IDX_DIM
    used = 2 * HEAD_DIM + n_iq + IDX_DIM + IDX_HEADS + heads
    ws = -(-used // LANES) * LANES
    w_small_t = jnp.concatenate(
        [w_in_t[layer, s0:s0 + 2 * HEAD_DIM + n_iq + IDX_DIM + IDX_HEADS],
         w_in_t[layer, 6 * bw:6 * bw + heads], jnp.zeros((ws - used, d), F32)], axis=0)
    ik_col = 2 * HEAD_DIM + n_iq
    iw_col = ik_col + IDX_DIM
    fl_col = iw_col + IDX_HEADS
    groups_small = ((0, 16), None) + ((1, 8),) * (n_iq // LANES) + ((2, 8),)
    tm_s = _tile(seq, 512)
    small = _matmul(h, w_small_t, pl.BlockSpec((ws, d), lambda j, i: (0, 0), pipeline_mode=pl.Buffered(1)),
                    d, ws, zero(ws), tm=tm_s, tn=ws, out_dtype=F32, w_rows_are_outputs=True,
                    epilogue=functools.partial(_rope_epilogue, groups=groups_small), extra=(tab_small,),
                    extra_specs=(pl.BlockSpec((tm_s, 9 * LANES), lambda j, i: (i, 0)),), name="proj_small")

    fl_lane = fl_col % LANES
    bias_row = jnp.zeros((1, LANES), F32).at[0, fl_lane:fl_lane + heads].set(p["b_forget"])
    dcum = _forget_cumsum(small, bias_row, batch, seq, ik_col)[:, fl_lane:fl_lane + heads]
    dcum = dcum.reshape(batch, seq, heads).transpose(0, 2, 1)
    dq_col = dcum[..., None]
    dk_row = dcum[:, :, None, :]

    tq = _tile(seq, 512)
    hb = bw // HEAD_DIM
    o_a = _attention(
        functools.partial(_attn_diff_kernel, tq=tq, seq=seq, lam_init=lam_init),
        qk_a, 0, qk_a, hb, mid, 0, heads, batch, seq,
        pre=(p["diff_lambda"], p["diff_norm_g"][None, :]),
        pre_specs=(pl.BlockSpec(p["diff_lambda"].shape, lambda b, h: (0, 0)),
                   pl.BlockSpec((1, HEAD_DIM), lambda b, h: (0, 0))),
        name="attn_diff")
    o_b = _attention(
        functools.partial(_attn_forget_kernel, tq=tq, seq=seq),
        mid, hb, mid, 2 * hb, mid, 3 * hb, heads, batch, seq,
        pre=(dq_col, dk_row),
        pre_specs=(pl.BlockSpec((None, None, seq, 1), lambda b, h: (b, h, 0, 0)),
                   pl.BlockSpec((None, None, 1, seq), lambda b, h: (b, h, 0, 0))),
        name="attn_forget")
    o_c = _attention(functools.partial(_attn_stick_kernel, tq=tq, seq=seq),
                     qkv_c, 0, qkv_c, hb, qkv_c, 2 * hb, heads, batch, seq, name="attn_stick")
    o_d = _attn_sparse(q_d, small, heads, batch, seq, _tile(seq, 256), min(TOPK_MAX, seq // 4),
                       2 * HEAD_DIM, ik_col, iw_col)

    n_br = p["w_gate"].shape[1]
    nd = d // tn
    gates = _matmul(h, p["w_gate"],
                    pl.BlockSpec((None, None, d, tn), lambda j, i: (layer, j // nd, 0, j % nd)),
                    d, n_br * d, p["b_gate"].reshape(1, n_br * d), tm=tm, tn=tn, out_dtype=F32,
                    epilogue=jax.nn.sigmoid, name="gates")
    merged = _merge((o_a, o_b, o_c, o_d), gates, p["w_branch"], layer, _tile(seq, 512), tn)
    per_b = seq // tm
    x = _matmul(merged, p["w_out"], _layer_col_spec(layer, d, tn), d, d, zero(d), tm=tm, tn=tn,
                out_dtype=F32, epilogue=lambda acc, xr, g: xr + g * acc, extra=(x, m4),
                extra_specs=(pl.BlockSpec((tm, tn), lambda j, i: (i, j)),
                             pl.BlockSpec((None, None, 1, tn), lambda j, i: (i // per_b, 2, 0, j))),
                name="out_proj")

    h_bf, h_rows = _ada_norm(x, p["norm2_g"], m4, 4, 3, seq, (BF16, F32), token_rows=(False, True))
    return _moe(h_bf, h_rows, x, m4, p["w_router"], p["b_router"], p["w_gu"], p["b_gu"],
                p["w_dn"], p["b_dn"], layer, seq)


@jax.jit
def _forward(x, c, positions, w_mod, b_mod, ada_table, norm1_g, w_in, b_forget, diff_lambda,
             diff_norm_g, w_branch, w_gate, b_gate, w_out, norm2_g, w_router, b_router,
             w_gu, b_gu, w_dn, b_dn, final_norm_g):
    batch, seq, d = x.shape
    depth = w_in.shape[0]
    t = batch * seq
    xt = x.reshape(t, d)

    c8 = jnp.pad(c, ((0, 8 - batch), (0, 0)))
    n_mod = w_mod.shape[1]
    mod = _matmul(c8, w_mod, _col_spec(d, _tile(n_mod, 512)), d, n_mod, b_mod[None, :], tm=8,
                  tn=_tile(n_mod, 512), out_dtype=F32, prologue=lambda v: v * jax.nn.sigmoid(v),
                  name="mod_proj")[:batch]
    mod = mod.reshape(batch, N_MOD, d)

    pos = positions.reshape(t).astype(F32)
    tabs = (_rope_table(pos, HEAD_DIM // 2, LANES),
            _rope_table(pos, HEAD_DIM, LANES),
            jnp.concatenate([_rope_table(pos, HEAD_DIM, LANES), _rope_table(pos, IDX_DIM, LANES),
                             _rope_table(pos, IDX_DIM, IDX_DIM)], axis=1))

    w_in_t = jnp.swapaxes(w_in, 1, 2)
    for l in range(depth):
        m4 = (mod + ada_table[l][None])[:, :, None, :]
        lam_init = 0.8 - 0.6 * float(np.exp(-0.3 * l))
        p = dict(norm1_g=norm1_g[l][None, :], w_in_t=w_in_t, b_forget=b_forget[l],
                 diff_lambda=diff_lambda[l], diff_norm_g=diff_norm_g[l], w_branch=w_branch,
                 w_gate=w_gate, b_gate=b_gate[l], w_out=w_out, norm2_g=norm2_g[l][None, :],
                 w_router=w_router[l], b_router=b_router[l], w_gu=w_gu, b_gu=b_gu,
                 w_dn=w_dn, b_dn=b_dn)
        xt = _layer(xt, m4, tabs, lam_init, batch, seq, l, p)

    zeros4 = jnp.zeros((batch, 1, 1, d), F32)
    (out,) = _ada_norm(xt, final_norm_g[None, :], zeros4, 0, 0, seq, (F32,))
    return out.reshape(batch, seq, d)


def kernel(x, c, positions, w_mod, b_mod, ada_table, norm1_g, w_in, b_forget, diff_lambda,
           diff_norm_g, w_branch, w_gate, b_gate, w_out, norm2_g, w_router, b_router,
           w_gu, b_gu, w_dn, b_dn, final_norm_g):
    return _forward(x, c, positions, w_mod, b_mod, ada_table, norm1_g, w_in, b_forget, diff_lambda,
                    diff_norm_g, w_branch, w_gate, b_gate, w_out, norm2_g, w_router, b_router,
                    w_gu, b_gu, w_dn, b_dn, final_norm_g)
```

```python
import functools

import numpy as np
import jax
import jax.numpy as jnp
from jax import lax
from jax.experimental import pallas as pl
from jax.experimental.pallas import tpu as pltpu

F32 = jnp.float32
BF16 = jnp.bfloat16
I32 = jnp.int32

LANES = 128
HEAD_DIM = 128
CHUNK = 64
IDX_HEADS = 8
IDX_DIM = 64
TOPK_MAX = 256
ROPE_THETA = 500000.0
ROPE_FRACTION = 4
TOP_K = 4
SWIGLU_LIMIT = 7.0
SWIGLU_ALPHA = 1.702
RMS_EPS = 1e-5
N_MOD = 6
NEG = -1e30
LOG2E = 1.4426950408889634
INT_MIN = -(2 ** 31)
ROW_PITCH_PAD = 4
VMEM_LIMIT = 56 * 1024 * 1024


def _params(n_axes, vmem=VMEM_LIMIT):
    return pltpu.CompilerParams(dimension_semantics=("arbitrary",) * n_axes,
                                vmem_limit_bytes=vmem)


def _tile(n, pref):
    t = min(n, pref)
    while n % t:
        t //= 2
    return t


def _row_pitch(nseg):
    return nseg + ROW_PITCH_PAD


def _store_token_rows(o_ref, val, tm):
    nseg = val.shape[1] // LANES
    pitch = _row_pitch(nseg)
    for s in range(nseg):
        o_ref[pl.ds(s, tm, stride=pitch), :] = val[:, s * LANES:(s + 1) * LANES]
    for s in range(nseg, pitch):
        o_ref[pl.ds(s, tm, stride=pitch), :] = jnp.zeros((tm, LANES), o_ref.dtype)


def _load_token_rows(ref, tm, nseg, lead=()):
    return [ref[(*lead, pl.ds(s, tm, stride=_row_pitch(nseg)), slice(None))] for s in range(nseg)]


def _norm_kernel(x_ref, g_ref, sc_ref, sh_ref, *o_refs, tm, token_rows):
    x = x_ref[...]
    var = jnp.mean(x * x, axis=-1, keepdims=True)
    y = x * lax.rsqrt(var + RMS_EPS) * g_ref[...]
    h = y * (1.0 + sc_ref[...]) + sh_ref[...]
    for o, rows in zip(o_refs, token_rows):
        if rows:
            _store_token_rows(o, h, tm)
        else:
            o[...] = h.astype(o.dtype)


def _ada_norm(x, g, m4, sc_idx, sh_idx, seq, out_dtypes, token_rows=None):
    t, d = x.shape
    tm = _tile(seq, 256)
    per_b = seq // tm
    pitch = _row_pitch(d // LANES)
    token_rows = token_rows or (False,) * len(out_dtypes)
    out_specs, out_shape = [], []
    for dt, rows in zip(out_dtypes, token_rows):
        if rows:
            out_specs.append(pl.BlockSpec((tm * pitch, LANES), lambda i: (i, 0)))
            out_shape.append(jax.ShapeDtypeStruct((t * pitch, LANES), F32))
        else:
            out_specs.append(pl.BlockSpec((tm, d), lambda i: (i, 0)))
            out_shape.append(jax.ShapeDtypeStruct((t, d), dt))
    return pl.pallas_call(
        functools.partial(_norm_kernel, tm=tm, token_rows=token_rows),
        grid=(t // tm,),
        in_specs=[
            pl.BlockSpec((tm, d), lambda i: (i, 0)),
            pl.BlockSpec((1, d), lambda i: (0, 0)),
            pl.BlockSpec((None, None, 1, d), lambda i: (i // per_b, sc_idx, 0, 0)),
            pl.BlockSpec((None, None, 1, d), lambda i: (i // per_b, sh_idx, 0, 0)),
        ],
        out_specs=out_specs,
        out_shape=out_shape,
        compiler_params=_params(1),
        name="ada_norm",
    )(x, g, m4, m4)


def _mm_kernel(*refs, n_extra, epilogue, prologue, cast_rows, w_rows_are_outputs):
    x_ref, w_ref, b_ref = refs[:3]
    extra = refs[3:3 + n_extra]
    o_ref = refs[3 + n_extra]
    wbf_ref = refs[4 + n_extra]
    if len(w_ref.shape) == 3:
        w_ref = w_ref.at[0]

    @pl.when(pl.program_id(1) == 0)
    def _():
        def body(r, c):
            rows = pl.ds(pl.multiple_of(r * cast_rows, cast_rows), cast_rows)
            wbf_ref[rows, :] = w_ref[rows, :].astype(BF16)
            return c

        lax.fori_loop(0, w_ref.shape[0] // cast_rows, body, 0)

    x = x_ref[...]
    if prologue is not None:
        x = prologue(x)
    x = x.astype(BF16)
    if w_rows_are_outputs:
        acc = _qk(x, wbf_ref[...])
    else:
        acc = jnp.dot(x, wbf_ref[...], preferred_element_type=F32)
    o_ref[...] = epilogue(acc + b_ref[...], *[e[...] for e in extra]).astype(o_ref.dtype)


def _matmul(x, w, w_spec, k, n, bias, *, tm, tn, out_dtype, epilogue=None, extra=(),
            extra_specs=(), prologue=None, w_rows_are_outputs=False, name="matmul"):
    m = x.shape[0]
    if epilogue is None:
        epilogue = lambda acc: acc
    w_block = (tn, k) if w_rows_are_outputs else (k, tn)
    kern = functools.partial(_mm_kernel, n_extra=len(extra), epilogue=epilogue, prologue=prologue,
                             cast_rows=_tile(w_block[0], 256), w_rows_are_outputs=w_rows_are_outputs)
    return pl.pallas_call(
        kern,
        grid=(n // tn, m // tm),
        in_specs=[pl.BlockSpec((tm, k), lambda j, i: (i, 0)),
                  w_spec,
                  pl.BlockSpec((1, tn), lambda j, i: (0, j)),
                  *extra_specs],
        out_specs=pl.BlockSpec((tm, tn), lambda j, i: (i, j)),
        out_shape=jax.ShapeDtypeStruct((m, n), out_dtype),
        scratch_shapes=[pltpu.VMEM(w_block, BF16)],
        compiler_params=_params(2),
        name=name,
    )(x, w, bias, *extra)


def _rope_epilogue(acc, tab, *, groups):
    outs = []
    for g, kind in enumerate(groups):
        xg = acc[:, g * LANES:(g + 1) * LANES]
        if kind is not None:
            slot, half = kind
            base = 3 * slot * LANES
            cos = tab[:, base:base + LANES]
            sin_lo = tab[:, base + LANES:base + 2 * LANES]
            sin_hi = tab[:, base + 2 * LANES:base + 3 * LANES]
            xg = (xg * cos + pltpu.roll(xg, half, 1) * sin_lo
                  + pltpu.roll(xg, LANES - half, 1) * sin_hi)
        outs.append(xg)
    return outs[0] if len(outs) == 1 else jnp.concatenate(outs, axis=1)


def _rope_table(pos, period, active):
    rot = period // ROPE_FRACTION
    half = rot // 2
    inv_freq = jnp.float32(ROPE_THETA) ** (-jnp.arange(half, dtype=F32) / half)
    lane = np.arange(LANES)
    r = lane % period
    on = lane < active
    first = on & (r < half)
    second = on & (r >= half) & (r < rot)
    fidx = np.where(first, r, np.where(second, r - half, 0))
    ang = pos[:, None] * inv_freq[fidx][None, :]
    cos, sin = jnp.cos(ang), jnp.sin(ang)
    c = jnp.where((first | second)[None, :], cos, 1.0)
    s_lo = jnp.where(second[None, :], sin, 0.0)
    s_hi = jnp.where(first[None, :], -sin, 0.0)
    return jnp.concatenate([c, s_lo, s_hi], axis=1)


def _qk(q, k):
    return lax.dot_general(q, k, (((1,), (1,)), ((), ())), preferred_element_type=F32)


def _softmax_tile(carry, s, v):
    m, l, acc = carry
    m_new = jnp.maximum(m, jnp.max(s, axis=-1, keepdims=True))
    alpha = jnp.exp2(m - m_new)
    p = jnp.exp2(s - m_new)
    l = alpha * l + jnp.sum(p, axis=-1, keepdims=True)
    acc = alpha * acc + jnp.dot(p.astype(BF16), v, preferred_element_type=F32)
    return m_new, l, acc


def _softmax_init(rows):
    return (jnp.full((rows, 1), NEG, F32), jnp.zeros((rows, 1), F32),
            jnp.zeros((rows, HEAD_DIM), F32))


def _rows(ref, j, t):
    return ref[pl.ds(pl.multiple_of(j * t, t), t), :]


def _attn_diff_kernel(dl_ref, g_ref, q_ref, k_ref, v_ref, o_ref, *, tq, seq, lam_init):
    half = HEAD_DIM // 2
    scale2 = half ** -0.5 * LOG2E
    lane = lax.broadcasted_iota(I32, (tq, HEAD_DIM), 1)
    r = lax.broadcasted_iota(I32, (2 * tq, tq), 0)
    r = jnp.where(r >= tq, r - tq, r)
    c = lax.broadcasted_iota(I32, (2 * tq, tq), 1)
    vis = c < (r // CHUNK + 1) * CHUNK
    dl = dl_ref[...]
    lam = (jnp.exp(jnp.sum(dl[0:1] * dl[1:2], axis=-1, keepdims=True))
           - jnp.exp(jnp.sum(dl[2:3] * dl[3:4], axis=-1, keepdims=True)) + lam_init)

    for i in range(seq // tq):
        q = q_ref[i * tq:(i + 1) * tq, :]
        zero = jnp.zeros_like(q)
        qz = jnp.concatenate([jnp.where(lane < half, q, zero), jnp.where(lane >= half, q, zero)], axis=0)
        carry = _softmax_init(2 * tq)
        for j in range(i + 1):
            s = _qk(qz, k_ref[j * tq:(j + 1) * tq, :]) * scale2
            if j == i:
                s = jnp.where(vis, s, NEG)
            carry = _softmax_tile(carry, s, v_ref[j * tq:(j + 1) * tq, :])
        _, l, acc = carry
        o = acc / l
        o = o[:tq] - lam * o[tq:]
        var = jnp.mean(o * o, axis=-1, keepdims=True)
        o_ref[i * tq:(i + 1) * tq, :] = (o * lax.rsqrt(var + RMS_EPS) * g_ref[...]
                                         * (1.0 - lam_init)).astype(o_ref.dtype)


def _attn_forget_kernel(dq_ref, dk_ref, q_ref, k_ref, v_ref, o_ref, *, tq, seq):
    scale2 = HEAD_DIM ** -0.5 * LOG2E
    r = lax.broadcasted_iota(I32, (tq, tq), 0)
    c = lax.broadcasted_iota(I32, (tq, tq), 1)
    for i in range(seq // tq):
        q = q_ref[i * tq:(i + 1) * tq, :]
        dq2 = dq_ref[i * tq:(i + 1) * tq, :] * LOG2E
        carry = _softmax_init(tq)
        for j in range(i + 1):
            dk2 = dk_ref[:, j * tq:(j + 1) * tq] * LOG2E
            s = _qk(q, k_ref[j * tq:(j + 1) * tq, :]) * scale2 + dq2 - dk2
            if j == i:
                s = jnp.where(c <= r, s, NEG)
            carry = _softmax_tile(carry, s, v_ref[j * tq:(j + 1) * tq, :])
        _, l, acc = carry
        o_ref[i * tq:(i + 1) * tq, :] = (acc / l).astype(o_ref.dtype)


def _log_sigmoid(z):
    return jnp.minimum(z, 0.0) - jnp.log(1.0 + jnp.exp(-jnp.abs(z)))


def _attn_stick_kernel(q_ref, k_ref, v_ref, o_ref, *, tq, seq):
    scale = HEAD_DIM ** -0.5
    r = lax.broadcasted_iota(I32, (tq, tq), 0)
    c = lax.broadcasted_iota(I32, (tq, tq), 1)
    cb = min(tq, 256)
    later = jnp.where(lax.broadcasted_iota(I32, (cb, cb), 0) > lax.broadcasted_iota(I32, (cb, cb), 1),
                      1.0, 0.0).astype(BF16)

    def tile(q, j, carry, mask):
        suffix, acc = carry
        z = _qk(q, k_ref[j * tq:(j + 1) * tq, :]) * scale
        log_beta = _log_sigmoid(z)
        log_keep = log_beta - z
        if mask is not None:
            log_keep = jnp.where(mask, log_keep, 0.0)
        parts = [None] * (tq // cb)
        for blk in reversed(range(tq // cb)):
            lk = log_keep[:, blk * cb:(blk + 1) * cb]
            hi = lk.astype(BF16)
            lo = (lk - hi.astype(F32)).astype(BF16)
            parts[blk] = (jnp.dot(hi, later, preferred_element_type=F32)
                          + jnp.dot(lo, later, preferred_element_type=F32) + suffix)
            suffix = suffix + jnp.sum(lk, axis=-1, keepdims=True)
        after = parts[0] if len(parts) == 1 else jnp.concatenate(parts, axis=1)
        w = jnp.exp(log_beta + after)
        if mask is not None:
            w = jnp.where(mask, w, 0.0)
        acc = acc + jnp.dot(w.astype(BF16), v_ref[j * tq:(j + 1) * tq, :], preferred_element_type=F32)
        return suffix, acc

    for i in range(seq // tq):
        q = q_ref[i * tq:(i + 1) * tq, :]
        carry = tile(q, i, (jnp.zeros((tq, 1), F32), jnp.zeros((tq, HEAD_DIM), F32)), c < r)
        for j in reversed(range(i)):
            carry = tile(q, j, carry, None)
        o_ref[i * tq:(i + 1) * tq, :] = carry[1].astype(o_ref.dtype)


def _attention(kernel, q_arr, q_col, k_arr, k_col, v_arr, v_col, heads, batch, seq,
               pre=(), pre_specs=(), name="attention"):
    t = batch * seq
    return pl.pallas_call(
        kernel,
        grid=(batch, heads),
        in_specs=[*pre_specs,
                  pl.BlockSpec((seq, HEAD_DIM), lambda b, h: (b, q_col + h)),
                  pl.BlockSpec((seq, HEAD_DIM), lambda b, h: (b, k_col + h)),
                  pl.BlockSpec((seq, HEAD_DIM), lambda b, h: (b, v_col + h))],
        out_specs=pl.BlockSpec((seq, HEAD_DIM), lambda b, h: (b, h)),
        out_shape=jax.ShapeDtypeStruct((t, heads * HEAD_DIM), BF16),
        compiler_params=_params(2),
        name=name,
    )(*pre, q_arr, k_arr, v_arr)


def _sortable(x):
    bits = pltpu.bitcast(x + 0.0, I32)
    return jnp.where(bits < 0, bits ^ jnp.int32(0x7FFFFFFF), bits)


def _attn_sparse_kernel(q_ref, sq_ref, kv_ref, ik_ref, o_ref, key_ref, thr_ref, *, tq, seq, heads,
                        topk, iq_col, iw_col):
    i = pl.program_id(1)
    scale = HEAD_DIM ** -0.5
    n_tiles = seq // tq
    q0 = i * tq

    sq = sq_ref[...]
    lane_k = lax.broadcasted_iota(I32, (tq, LANES), 1)
    iq_groups = [sq[:, iq_col + g * LANES: iq_col + (g + 1) * LANES].astype(BF16)
                 for g in range(IDX_HEADS * IDX_DIM // LANES)]
    iw = [jnp.broadcast_to(sq[:, iw_col + h: iw_col + h + 1], (tq, tq)) for h in range(IDX_HEADS)]
    row = lax.broadcasted_iota(I32, (tq, tq), 0)
    col = lax.broadcasted_iota(I32, (tq, tq), 1)
    chunk_end = ((q0 + row) // CHUNK + 1) * CHUNK

    key_ref[...] = jnp.full((tq, seq), INT_MIN, I32)

    def score_tile(j, c):
        ikt = _rows(ik_ref, j, tq)
        ik_lo = jnp.where(lane_k < IDX_DIM, ikt, 0.0)
        ik_hi = pltpu.roll(ik_lo, IDX_DIM, 1)
        ik_lo = ik_lo.astype(BF16)
        ik_hi = ik_hi.astype(BF16)
        score = jnp.zeros((tq, tq), F32)
        for h in range(IDX_HEADS):
            s = _qk(iq_groups[h // 2], ik_hi if h % 2 else ik_lo)
            score = score + iw[h] * jnp.maximum(s, 0.0)
        vis = (j * tq + col) < chunk_end
        key_ref[:, pl.ds(pl.multiple_of(j * tq, tq), tq)] = jnp.where(vis, _sortable(score), INT_MIN)
        return c

    lax.fori_loop(0, i + 1, score_tile, 0)

    def count_ge(cand, ncols=seq):
        return jnp.sum(jnp.where(key_ref[:, :ncols] >= cand, 1.0, 0.0), axis=-1, keepdims=True)

    kf = jnp.float32(topk)

    def find_threshold(ncols):
        thr = jnp.where(count_ge(jnp.zeros((tq, 1), I32), ncols) >= kf, 0, INT_MIN).astype(I32)

        def bisect(b, thr):
            cand = thr | lax.shift_left(jnp.int32(1), 30 - b)
            return jnp.where(count_ge(cand, ncols) >= kf, cand, thr)

        thr = lax.fori_loop(0, jnp.where((i + 1) * tq <= topk, 0, 31), bisect, thr)
        thr_ref[...] = jnp.maximum(thr, INT_MIN + 1)

    lo = 0
    for hi in sorted({min(n_tiles, v) for v in range(2, n_tiles + 2, 2)}):
        pl.when((i + 1 > lo) & (i + 1 <= hi))(functools.partial(find_threshold, hi * tq))
        lo = hi
    thr = thr_ref[...]
    n_ge = count_ge(thr)

    @pl.when(jnp.max(n_ge) > kf)
    def _():
        need = kf - jnp.sum(jnp.where(key_ref[...] > thr, 1.0, 0.0), axis=-1, keepdims=True)
        upto = jnp.where(row <= col, 1.0, 0.0).astype(BF16)

        def fix(j, seen):
            cols = pl.ds(pl.multiple_of(j * tq, tq), tq)
            kt = key_ref[:, cols]
            eq = kt == thr
            rank = jnp.dot(jnp.where(eq, 1.0, 0.0).astype(BF16), upto, preferred_element_type=F32) + seen
            key_ref[:, cols] = jnp.where(eq & (rank > need), INT_MIN, kt)
            return seen + jnp.sum(jnp.where(eq, 1.0, 0.0), axis=-1, keepdims=True)

        lax.fori_loop(0, n_tiles, fix, jnp.zeros((tq, 1), F32))

    qs = jnp.concatenate([q_ref[:, h * HEAD_DIM:(h + 1) * HEAD_DIM] for h in range(heads)], axis=0)

    def attn_tile(j, carry):
        kvt = _rows(kv_ref, j, tq)
        kt = kvt[:, :HEAD_DIM].astype(BF16)
        vt = kvt[:, HEAD_DIM:].astype(BF16)
        bias = jnp.where(key_ref[:, pl.ds(pl.multiple_of(j * tq, tq), tq)] >= thr, 0.0, NEG)
        s = (_qk(qs, kt) * (scale * LOG2E)).reshape(heads, tq, tq) + bias[None]
        return _softmax_tile(carry, s.reshape(heads * tq, tq), vt)

    _, l, acc = lax.fori_loop(0, i + 1, attn_tile, _softmax_init(heads * tq))
    o = acc / l
    for h in range(heads):
        o_ref[:, h * HEAD_DIM:(h + 1) * HEAD_DIM] = o[h * tq:(h + 1) * tq].astype(o_ref.dtype)


def _attn_sparse(dq, small, heads, batch, seq, tq, topk, iq_col, ik_col, iw_col):
    nq = seq // tq
    t = batch * seq
    bw = heads * HEAD_DIM
    w = small.shape[1]
    kern = functools.partial(_attn_sparse_kernel, tq=tq, seq=seq, heads=heads, topk=topk,
                             iq_col=iq_col, iw_col=iw_col)
    return pl.pallas_call(
        kern,
        grid=(batch, nq),
        in_specs=[pl.BlockSpec((tq, bw), lambda b, i: (b * nq + i, 0)),
                  pl.BlockSpec((tq, w), lambda b, i: (b * nq + i, 0)),
                  pl.BlockSpec((seq, 2 * HEAD_DIM), lambda b, i: (b, 0)),
                  pl.BlockSpec((seq, LANES), lambda b, i: (b, ik_col // LANES))],
        out_specs=pl.BlockSpec((tq, bw), lambda b, i: (b * nq + i, 0)),
        out_shape=jax.ShapeDtypeStruct((t, bw), BF16),
        scratch_shapes=[pltpu.VMEM((tq, seq), I32), pltpu.VMEM((tq, 1), I32)],
        compiler_params=_params(2),
        name="attn_sparse",
    )(dq, small, small, small)


def _forget_cumsum_kernel(x_ref, b_ref, o_ref, *, seq, blk):
    r = lax.broadcasted_iota(I32, (blk, blk), 0)
    c = lax.broadcasted_iota(I32, (blk, blk), 1)
    tri = jnp.where(c <= r, 1.0, 0.0).astype(BF16)
    carry = jnp.zeros((1, LANES), F32)
    for s in range(seq // blk):
        logf = _log_sigmoid(x_ref[s * blk:(s + 1) * blk, :] + b_ref[...])
        hi = logf.astype(BF16)
        rem = logf - hi.astype(F32)
        mid = rem.astype(BF16)
        lo = (rem - mid.astype(F32)).astype(BF16)
        local = (jnp.dot(tri, hi, preferred_element_type=F32)
                 + jnp.dot(tri, mid, preferred_element_type=F32)
                 + jnp.dot(tri, lo, preferred_element_type=F32)) + carry
        o_ref[s * blk:(s + 1) * blk, :] = local
        carry = local[blk - 1:blk, :]


def _forget_cumsum(small, bias_row, batch, seq, col):
    return pl.pallas_call(
        functools.partial(_forget_cumsum_kernel, seq=seq, blk=_tile(seq, 256)),
        grid=(batch,),
        in_specs=[pl.BlockSpec((seq, LANES), lambda b: (b, col // LANES)),
                  pl.BlockSpec((1, LANES), lambda b: (0, 0))],
        out_specs=pl.BlockSpec((seq, LANES), lambda b: (b, 0)),
        out_shape=jax.ShapeDtypeStruct((batch * seq, LANES), F32),
        compiler_params=_params(1),
        name="forget_cumsum",
    )(small, bias_row)


def _merge_kernel(*refs, n_br, cast_rows):
    o_refs = refs[:n_br]
    g_refs = refs[n_br:2 * n_br]
    w_ref = refs[2 * n_br]
    out_ref = refs[2 * n_br + 1]
    wbf_ref = refs[2 * n_br + 2]

    @pl.when(pl.program_id(1) == 0)
    def _():
        for b in range(n_br):
            def body(r, c, b=b):
                rows = pl.ds(pl.multiple_of(r * cast_rows, cast_rows), cast_rows)
                wbf_ref[b, rows, :] = w_ref[b, rows, :].astype(BF16)
                return c
            lax.fori_loop(0, w_ref.shape[1] // cast_rows, body, 0)

    acc = None
    for b in range(n_br):
        term = g_refs[b][...] * jnp.dot(o_refs[b][...], wbf_ref[b], preferred_element_type=F32)
        acc = term if acc is None else acc + term
    out_ref[...] = acc.astype(out_ref.dtype)


def _merge(outs, gates, w_branch, layer, tm, tn):
    _, n_br, bw, d = w_branch.shape
    t = outs[0].shape[0]
    nd = d // tn
    kern = functools.partial(_merge_kernel, n_br=n_br, cast_rows=_tile(bw, 256))
    return pl.pallas_call(
        kern,
        grid=(nd, t // tm),
        in_specs=[*[pl.BlockSpec((tm, bw), lambda j, i: (i, 0)) for _ in range(n_br)],
                  *[pl.BlockSpec((tm, tn), lambda j, i, b=b: (i, b * nd + j)) for b in range(n_br)],
                  pl.BlockSpec((None, n_br, bw, tn), lambda j, i: (layer, 0, 0, j))],
        out_specs=pl.BlockSpec((tm, tn), lambda j, i: (i, j)),
        out_shape=jax.ShapeDtypeStruct((t, d), BF16),
        scratch_shapes=[pltpu.VMEM((n_br, bw, tn), BF16)],
        compiler_params=_params(2),
        name="gated_merge",
    )(*outs, *([gates] * n_br), w_branch)


def _router_kernel(h_ref, w_ref, b_ref, meta_i_ref, meta_w_ref, cnt_ref, run_ref, *, tm, n_exp):
    @pl.when(pl.program_id(0) == 0)
    def _():
        run_ref[...] = jnp.zeros_like(run_ref)

    lane = lax.broadcasted_iota(I32, (tm, LANES), 1)
    lane_f = lane.astype(F32)
    logits = jnp.dot(h_ref[...], w_ref[...].astype(BF16), preferred_element_type=F32) + b_ref[...]
    logits = jnp.where(lane < n_exp, logits, -jnp.inf)

    vals, idxs = [], []
    work = logits
    for _ in range(TOP_K):
        v = jnp.max(work, axis=-1, keepdims=True)
        ix = jnp.min(jnp.where(work == v, lane_f, float(LANES)), axis=-1, keepdims=True).astype(I32)
        vals.append(v)
        idxs.append(ix)
        work = jnp.where(lane == ix, -jnp.inf, work)
    es = [jnp.exp(v - vals[0]) for v in vals]
    denom = es[0] + es[1] + es[2] + es[3]

    onehots = [jnp.where(lane == ix, 1.0, 0.0) for ix in idxs]
    sel = onehots[0] + onehots[1] + onehots[2] + onehots[3]
    r = lax.broadcasted_iota(I32, (tm, tm), 0)
    c = lax.broadcasted_iota(I32, (tm, tm), 1)
    before = jnp.where(c < r, 1.0, 0.0).astype(BF16)
    rank = jnp.dot(before, sel.astype(BF16), preferred_element_type=F32) + run_ref[...]
    run_ref[...] = run_ref[...] + jnp.sum(sel, axis=0, keepdims=True)

    meta_i = jnp.zeros((tm, LANES), I32)
    meta_w = jnp.zeros((tm, LANES), F32)
    for k in range(TOP_K):
        rk = jnp.sum(onehots[k] * rank, axis=-1, keepdims=True).astype(I32)
        meta_i = jnp.where(lane == k, idxs[k], meta_i)
        meta_i = jnp.where(lane == TOP_K + k, rk, meta_i)
        meta_w = jnp.where(lane == k, es[k] / denom, meta_w)
    meta_i_ref[...] = meta_i
    meta_w_ref[...] = meta_w
    cnt_ref[...] = run_ref[...]


def _router(h, w_router_pad, b_router_pad, n_exp):
    t, d = h.shape
    tm = _tile(t, 256)
    kern = functools.partial(_router_kernel, tm=tm, n_exp=n_exp)
    return pl.pallas_call(
        kern,
        grid=(t // tm,),
        in_specs=[pl.BlockSpec((tm, d), lambda i: (i, 0)),
                  pl.BlockSpec((d, LANES), lambda i: (0, 0)),
                  pl.BlockSpec((1, LANES), lambda i: (0, 0))],
        out_specs=[pl.BlockSpec((tm, LANES), lambda i: (i, 0)),
                   pl.BlockSpec((tm, LANES), lambda i: (i, 0)),
                   pl.BlockSpec((1, LANES), lambda i: (0, 0))],
        out_shape=[jax.ShapeDtypeStruct((t, LANES), I32),
                   jax.ShapeDtypeStruct((t, LANES), F32),
                   jax.ShapeDtypeStruct((1, LANES), F32)],
        scratch_shapes=[pltpu.VMEM((1, LANES), F32)],
        compiler_params=_params(1),
        name="router",
    )(h, w_router_pad, b_router_pad)


def _ffn_kernel(te_ref, nu_ref, tok_ref, h_ref, wgu_hbm, bgu_ref, wdn_hbm, bdn_ref, o_ref,
                xbuf, wgu_buf, wdn_buf, pick_ref, sem, wsem, *, tm, nseg, kc, nc, layer):
    i = pl.program_id(0)
    n_used = nu_ref[0]
    d, ff = wgu_buf.shape[0], wdn_buf.shape[0]
    pitch = _row_pitch(nseg)
    n_kc, n_nc = d // kc, d // nc

    def fetch(tile, slot):
        def body(r, c):
            tok = tok_ref[tile * tm + r]
            pltpu.make_async_copy(h_ref.at[pl.ds(tok * pitch, nseg), :],
                                  xbuf.at[slot, pl.ds(r * pitch, nseg), :], sem.at[slot]).start()
            return c
        lax.fori_loop(0, tm, body, 0, unroll=4)

    def gu_copy(e, c):
        rows = pl.ds(c * kc, kc)
        return pltpu.make_async_copy(wgu_hbm.at[layer, e, rows, :], wgu_buf.at[rows, :], wsem.at[c])

    def dn_copy(e, c):
        cols = pl.ds(c * nc, nc)
        return pltpu.make_async_copy(wdn_hbm.at[layer, e, :, cols], wdn_buf.at[:, cols],
                                     wsem.at[n_kc + c])

    active = i < n_used
    new_expert = (i == 0) | (te_ref[i] != te_ref[jnp.maximum(i - 1, 0)])

    @pl.when(active & new_expert)
    def _():
        for c in range(n_kc):
            gu_copy(te_ref[i], c).start()
        for c in range(n_nc):
            dn_copy(te_ref[i], c).start()

    @pl.when(i == 0)
    def _():
        r = lax.broadcasted_iota(I32, (2 * ff, ff), 0)
        c = lax.broadcasted_iota(I32, (2 * ff, ff), 1)
        pick_ref[...] = jnp.where(r == 2 * c, 1.0, 0.0).astype(BF16)
        fetch(0, 0)

    @pl.when(i + 1 < n_used)
    def _():
        fetch(i + 1, (i + 1) % 2)

    def compute(arriving):
        slot = i % 2
        done = xbuf.at[slot, pl.ds(0, tm * nseg), :]
        pltpu.make_async_copy(done, done, sem.at[slot]).wait()
        parts = _load_token_rows(xbuf, tm, nseg, lead=(slot,))
        per = kc // LANES
        gu = bgu_ref[...]
        for c in range(n_kc):
            if arriving is not None:
                gu_copy(arriving, c).wait()
            xc = jnp.concatenate(parts[c * per:(c + 1) * per], axis=1).astype(BF16)
            gu = gu + jnp.dot(xc, wgu_buf[c * kc:(c + 1) * kc, :].astype(BF16),
                              preferred_element_type=F32)
        lin = pltpu.roll(gu, 2 * ff - 1, 1)
        glu = jnp.minimum(gu, SWIGLU_LIMIT)
        lin = jnp.clip(lin, -SWIGLU_LIMIT, SWIGLU_LIMIT)
        act = glu * jax.nn.sigmoid(SWIGLU_ALPHA * glu) * (lin + 1.0)
        act = jnp.dot(act.astype(BF16), pick_ref[...], preferred_element_type=F32).astype(BF16)
        for c in range(n_nc):
            if arriving is not None:
                dn_copy(arriving, c).wait()
            cols = slice(c * nc, (c + 1) * nc)
            o_ref[:, cols] = (jnp.dot(act, wdn_buf[:, cols].astype(BF16), preferred_element_type=F32)
                              + bdn_ref[:, cols])

    pl.when(active & new_expert)(lambda: compute(te_ref[i]))
    pl.when(active & jnp.logical_not(new_expert))(lambda: compute(None))

    @pl.when(i >= n_used)
    def _():
        o_ref[...] = jnp.zeros_like(o_ref)


def _expert_ffn(h_rows, tile_expert, n_used, token_of_row, w_gu, b_gu, w_dn, b_dn, layer, tm):
    _, n_exp, d, ff2 = w_gu.shape
    ff = ff2 // 2
    nseg = d // LANES
    n_tiles = token_of_row.shape[0] // tm
    kc, nc = _tile(d, 512), _tile(d, 1024)

    def wsel(i, te, nu, tok):
        return (layer, te[i], 0, 0)

    kern = functools.partial(_ffn_kernel, tm=tm, nseg=nseg, kc=kc, nc=nc, layer=layer)
    return pl.pallas_call(
        kern,
        grid_spec=pltpu.PrefetchScalarGridSpec(
            num_scalar_prefetch=3,
            grid=(n_tiles,),
            in_specs=[pl.BlockSpec(memory_space=pl.ANY),
                      pl.BlockSpec(memory_space=pl.ANY),
                      pl.BlockSpec((None, None, 1, ff2), wsel),
                      pl.BlockSpec(memory_space=pl.ANY),
                      pl.BlockSpec((None, None, 1, d), wsel)],
            out_specs=pl.BlockSpec((tm, d), lambda i, te, nu, tok: (i, 0)),
            scratch_shapes=[pltpu.VMEM((2, tm * _row_pitch(nseg), LANES), F32),
                            pltpu.VMEM((d, ff2), F32),
                            pltpu.VMEM((ff, d), F32),
                            pltpu.VMEM((ff2, ff), BF16),
                            pltpu.SemaphoreType.DMA((2,)),
                            pltpu.SemaphoreType.DMA((d // kc + d // nc,))]),
        out_shape=jax.ShapeDtypeStruct((n_tiles * tm, d), F32),
        compiler_params=_params(1),
        name="expert_ffn",
    )(tile_expert, n_used, token_of_row, h_rows, w_gu, b_gu, w_dn, b_dn)


def _combine_kernel(pos_ref, ys_ref, x_ref, g_ref, w_ref, o_ref, buf, sem, *, tm):
    i = pl.program_id(0)

    def fetch(tile, slot):
        def body(t, c):
            for k in range(TOP_K):
                row = pos_ref[(tile * tm + t) * TOP_K + k]
                pltpu.make_async_copy(ys_ref.at[pl.ds(row, 1), :],
                                      buf.at[slot, k, pl.ds(t, 1), :], sem.at[slot]).start()
            return c
        lax.fori_loop(0, tm, body, 0, unroll=2)

    @pl.when(i == 0)
    def _():
        fetch(0, 0)

    @pl.when(i + 1 < pl.num_programs(0))
    def _():
        fetch(i + 1, (i + 1) % 2)

    slot = i % 2
    pltpu.make_async_copy(buf.at[slot], buf.at[slot], sem.at[slot]).wait()
    w = w_ref[...]
    y = w[:, 0:1] * buf[slot, 0]
    for k in range(1, TOP_K):
        y = y + w[:, k:k + 1] * buf[slot, k]
    o_ref[...] = x_ref[...] + g_ref[...] * y


def _combine(ys, pos_flat, x, m4, gate_idx, meta_w, seq, tm):
    t, d = x.shape
    per_b = seq // tm
    return pl.pallas_call(
        functools.partial(_combine_kernel, tm=tm),
        grid_spec=pltpu.PrefetchScalarGridSpec(
            num_scalar_prefetch=1,
            grid=(t // tm,),
            in_specs=[pl.BlockSpec(memory_space=pl.ANY),
                      pl.BlockSpec((tm, d), lambda i, p: (i, 0)),
                      pl.BlockSpec((None, None, 1, d), lambda i, p: (i // per_b, gate_idx, 0, 0)),
                      pl.BlockSpec((tm, LANES), lambda i, p: (i, 0))],
            out_specs=pl.BlockSpec((tm, d), lambda i, p: (i, 0)),
            scratch_shapes=[pltpu.VMEM((2, TOP_K, tm, d), F32),
                            pltpu.SemaphoreType.DMA((2,))]),
        out_shape=jax.ShapeDtypeStruct((t, d), F32),
        compiler_params=_params(1),
        name="moe_combine",
    )(pos_flat, ys, x, m4, meta_w)


def _moe(h_bf, h_rows, x, m4, w_router, b_router, w_gu, b_gu, w_dn, b_dn, layer, seq):
    t, d = x.shape
    n_exp = w_router.shape[1]
    tm = _tile(t, 256)
    wr = jnp.pad(w_router, ((0, 0), (0, LANES - n_exp)))
    br = jnp.pad(b_router, (0, LANES - n_exp))[None, :]
    meta_i, meta_w, counts = _router(h_bf, wr, br, n_exp)

    counts = counts[0, :n_exp].astype(I32)
    padded = ((counts + tm - 1) // tm) * tm
    ends = jnp.cumsum(padded)
    starts = ends - padded
    n_rows = t * TOP_K + n_exp * tm
    n_tiles = n_rows // tm
    experts = meta_i[:, :TOP_K]
    pos = starts[experts] + meta_i[:, TOP_K:2 * TOP_K]
    tile_start = jnp.arange(n_tiles, dtype=I32) * tm
    tile_expert = jnp.minimum(jnp.sum((ends[None, :] <= tile_start[:, None]).astype(I32), axis=1),
                              n_exp - 1)
    n_used = (ends[-1] // tm).astype(I32)[None]
    token_of_row = jnp.zeros((n_rows,), I32).at[pos.reshape(-1)].set(
        jnp.repeat(jnp.arange(t, dtype=I32), TOP_K))

    ys = _expert_ffn(h_rows, tile_expert, n_used, token_of_row, w_gu, b_gu[:, :, None, :], w_dn,
                     b_dn[:, :, None, :], layer, tm)
    return _combine(ys, pos.reshape(-1), x, m4, 5, meta_w, seq, _tile(seq, 64))


def _col_spec(k, tn, off_blocks=0):
    return pl.BlockSpec((k, tn), lambda j, i: (0, j + off_blocks))


def _layer_col_spec(layer, k, tn, off_blocks=0):
    return pl.BlockSpec((None, k, tn), lambda j, i: (layer, 0, j + off_blocks))


def _layer(x, m4, tabs, lam_init, batch, seq, layer, p):
    t, d = x.shape
    bw = p["w_branch"].shape[2]
    heads = bw // HEAD_DIM
    tm = _tile(seq, 1024)
    tn = _tile(bw, 512)
    w_in_t = p["w_in_t"]
    zero = lambda n: jnp.zeros((1, n), F32)

    (h,) = _ada_norm(x, p["norm1_g"], m4, 1, 0, seq, (BF16,))

    tab8, tab16, tab_small = tabs
    tab_spec = lambda w: pl.BlockSpec((tm, w), lambda j, i: (i, 0))
    rope8 = functools.partial(_rope_epilogue, groups=((0, 8),) * (tn // LANES))
    rope16 = functools.partial(_rope_epilogue, groups=((0, 16),) * (tn // LANES))

    def in_proj(row0, n, name, **kw):
        spec = pl.BlockSpec((pl.Element(1), pl.Element(tn), pl.Element(d)),
                            lambda j, i: (layer, pl.multiple_of(row0 + j * tn, 8), 0))
        return _matmul(h, w_in_t, spec, d, n, zero(n), tm=tm, tn=tn, out_dtype=BF16,
                       w_rows_are_outputs=True, name=name, **kw)

    c0 = 6 * bw + heads
    qk_a = in_proj(0, 2 * bw, "proj_a_qk", epilogue=rope8, extra=(tab8,),
                   extra_specs=(tab_spec(3 * LANES),))
    mid = in_proj(2 * bw, 4 * bw, "proj_av_b")
    qkv_c = in_proj(c0, 3 * bw, "proj_c")
    q_d = in_proj(c0 + 3 * bw, bw, "proj_d_q", epilogue=rope16, extra=(tab16,),
                  extra_specs=(tab_spec(3 * LANES),))
    s0 = c0 + 4 * bw
    n_iq = IDX_HEADS * IDX_DIM
    used = 2 * HEAD_DIM + n_iq + IDX_DIM + IDX_HEADS + heads
    ws = -(-used // LANES) * LANES
    w_small_t = jnp.concatenate(
        [w_in_t[layer, s0:s0 + 2 * HEAD_DIM + n_iq + IDX_DIM + IDX_HEADS],
         w_in_t[layer, 6 * bw:6 * bw + heads], jnp.zeros((ws - used, d), F32)], axis=0)
    ik_col = 2 * HEAD_DIM + n_iq
    iw_col = ik_col + IDX_DIM
    fl_col = iw_col + IDX_HEADS
    groups_small = ((0, 16), None) + ((1, 8),) * (n_iq // LANES) + ((2, 8),)
    tm_s = _tile(seq, 512)
    small = _matmul(h, w_small_t, pl.BlockSpec((ws, d), lambda j, i: (0, 0), pipeline_mode=pl.Buffered(1)),
                    d, ws, zero(ws), tm=tm_s, tn=ws, out_dtype=F32, w_rows_are_outputs=True,
                    epilogue=functools.partial(_rope_epilogue, groups=groups_small), extra=(tab_small,),
                    extra_specs=(pl.BlockSpec((tm_s, 9 * LANES), lambda j, i: (i, 0)),), name="proj_small")

    fl_lane = fl_col % LANES
    bias_row = jnp.zeros((1, LANES), F32).at[0, fl_lane:fl_lane + heads].set(p["b_forget"])
    dcum = _forget_cumsum(small, bias_row, batch, seq, ik_col)[:, fl_lane:fl_lane + heads]
    dcum = dcum.reshape(batch, seq, heads).transpose(0, 2, 1)
    dq_col = dcum[..., None]
    dk_row = dcum[:, :, None, :]

    tq = _tile(seq, 512)
    hb = bw // HEAD_DIM
    o_a = _attention(
        functools.partial(_attn_diff_kernel, tq=tq, seq=seq, lam_init=lam_init),
        qk_a, 0, qk_a, hb, mid, 0, heads, batch, seq,
        pre=(p["diff_lambda"], p["diff_norm_g"][None, :]),
        pre_specs=(pl.BlockSpec(p["diff_lambda"].shape, lambda b, h: (0, 0)),
                   pl.BlockSpec((1, HEAD_DIM), lambda b, h: (0, 0))),
        name="attn_diff")
    o_b = _attention(
        functools.partial(_attn_forget_kernel, tq=tq, seq=seq),
        mid, hb, mid, 2 * hb, mid, 3 * hb, heads, batch, seq,
        pre=(dq_col, dk_row),
        pre_specs=(pl.BlockSpec((None, None, seq, 1), lambda b, h: (b, h, 0, 0)),
                   pl.BlockSpec((None, None, 1, seq), lambda b, h: (b, h, 0, 0))),
        name="attn_forget")
    o_c = _attention(functools.partial(_attn_stick_kernel, tq=tq, seq=seq),
                     qkv_c, 0, qkv_c, hb, qkv_c, 2 * hb, heads, batch, seq, name="attn_stick")
    o_d = _attn_sparse(q_d, small, heads, batch, seq, _tile(seq, 256), min(TOPK_MAX, seq // 4),
                       2 * HEAD_DIM, ik_col, iw_col)

    n_br = p["w_gate"].shape[1]
    nd = d // tn
    gates = _matmul(h, p["w_gate"],
                    pl.BlockSpec((None, None, d, tn), lambda j, i: (layer, j // nd, 0, j % nd)),
                    d, n_br * d, p["b_gate"].reshape(1, n_br * d), tm=tm, tn=tn, out_dtype=F32,
                    epilogue=jax.nn.sigmoid, name="gates")
    merged = _merge((o_a, o_b, o_c, o_d), gates, p["w_branch"], layer, _tile(seq, 512), tn)
    per_b = seq // tm
    x = _matmul(merged, p["w_out"], _layer_col_spec(layer, d, tn), d, d, zero(d), tm=tm, tn=tn,
                out_dtype=F32, epilogue=lambda acc, xr, g: xr + g * acc, extra=(x, m4),
                extra_specs=(pl.BlockSpec((tm, tn), lambda j, i: (i, j)),
                             pl.BlockSpec((None, None, 1, tn), lambda j, i: (i // per_b, 2, 0, j))),
                name="out_proj")

    h_bf, h_rows = _ada_norm(x, p["norm2_g"], m4, 4, 3, seq, (BF16, F32), token_rows=(False, True))
    return _moe(h_bf, h_rows, x, m4, p["w_router"], p["b_router"], p["w_gu"], p["b_gu"],
                p["w_dn"], p["b_dn"], layer, seq)


@jax.jit
def _forward(x, c, positions, w_mod, b_mod, ada_table, norm1_g, w_in, b_forget, diff_lambda,
             diff_norm_g, w_branch, w_gate, b_gate, w_out, norm2_g, w_router, b_router,
             w_gu, b_gu, w_dn, b_dn, final_norm_g):
    batch, seq, d = x.shape
    depth = w_in.shape[0]
    t = batch * seq
    xt = x.reshape(t, d)

    c8 = jnp.pad(c, ((0, 8 - batch), (0, 0)))
    n_mod = w_mod.shape[1]
    mod = _matmul(c8, w_mod, _col_spec(d, _tile(n_mod, 512)), d, n_mod, b_mod[None, :], tm=8,
                  tn=_tile(n_mod, 512), out_dtype=F32, prologue=lambda v: v * jax.nn.sigmoid(v),
                  name="mod_proj")[:batch]
    mod = mod.reshape(batch, N_MOD, d)

    pos = positions.reshape(t).astype(F32)
    tabs = (_rope_table(pos, HEAD_DIM // 2, LANES),
            _rope_table(pos, HEAD_DIM, LANES),
            jnp.concatenate([_rope_table(pos, HEAD_DIM, LANES), _rope_table(pos, IDX_DIM, LANES),
                             _rope_table(pos, IDX_DIM, IDX_DIM)], axis=1))

    w_in_t = jnp.swapaxes(w_in, 1, 2)
    for l in range(depth):
        m4 = (mod + ada_table[l][None])[:, :, None, :]
        lam_init = 0.8 - 0.6 * float(np.exp(-0.3 * l))
        p = dict(norm1_g=norm1_g[l][None, :], w_in_t=w_in_t, b_forget=b_forget[l],
                 diff_lambda=diff_lambda[l], diff_norm_g=diff_norm_g[l], w_branch=w_branch,
                 w_gate=w_gate, b_gate=b_gate[l], w_out=w_out, norm2_g=norm2_g[l][None, :],
                 w_router=w_router[l], b_router=b_router[l], w_gu=w_gu, b_gu=b_gu,
                 w_dn=w_dn, b_dn=b_dn)
        xt = _layer(xt, m4, tabs, lam_init, batch, seq, l, p)

    zeros4 = jnp.zeros((batch, 1, 1, d), F32)
    (out,) = _ada_norm(xt, final_norm_g[None, :], zeros4, 0, 0, seq, (F32,))
    return out.reshape(batch, seq, d)


def kernel(x, c, positions, w_mod, b_mod, ada_table, norm1_g, w_in, b_forget, diff_lambda,
           diff_norm_g, w_branch, w_gate, b_gate, w_out, norm2_g, w_router, b_router,
           w_gu, b_gu, w_dn, b_dn, final_norm_g):
    return _forward(x, c, positions, w_mod, b_mod, ada_table, norm1_g, w_in, b_forget, diff_lambda,
                    diff_norm_g, w_branch, w_gate, b_gate, w_out, norm2_g, w_router, b_router,
                    w_gu, b_gu, w_dn, b_dn, final_norm_g)
```

```python
import functools

import numpy as np
import jax
import jax.numpy as jnp
from jax import lax
from jax.experimental import pallas as pl
from jax.experimental.pallas import tpu as pltpu

F32 = jnp.float32
BF16 = jnp.bfloat16
I32 = jnp.int32

LANES = 128
HEAD_DIM = 128
CHUNK = 64
IDX_HEADS = 8
IDX_DIM = 64
TOPK_MAX = 256
ROPE_THETA = 500000.0
ROPE_FRACTION = 4
TOP_K = 4
SWIGLU_LIMIT = 7.0
SWIGLU_ALPHA = 1.702
RMS_EPS = 1e-5
N_MOD = 6
NEG = -1e30
LOG2E = 1.4426950408889634
INT_MIN = -(2 ** 31)
ROW_PITCH_PAD = 4
VMEM_LIMIT = 56 * 1024 * 1024


def _params(n_axes, vmem=VMEM_LIMIT):
    return pltpu.CompilerParams(dimension_semantics=("arbitrary",) * n_axes,
                                vmem_limit_bytes=vmem)


def _tile(n, pref):
    t = min(n, pref)
    while n % t:
        t //= 2
    return t


def _row_pitch(nseg):
    return nseg + ROW_PITCH_PAD


def _store_token_rows(o_ref, val, tm):
    nseg = val.shape[1] // LANES
    pitch = _row_pitch(nseg)
    for s in range(nseg):
        o_ref[pl.ds(s, tm, stride=pitch), :] = val[:, s * LANES:(s + 1) * LANES]
    for s in range(nseg, pitch):
        o_ref[pl.ds(s, tm, stride=pitch), :] = jnp.zeros((tm, LANES), o_ref.dtype)


def _load_token_rows(ref, tm, nseg, lead=()):
    return [ref[(*lead, pl.ds(s, tm, stride=_row_pitch(nseg)), slice(None))] for s in range(nseg)]


def _norm_kernel(x_ref, g_ref, sc_ref, sh_ref, *o_refs, tm, token_rows):
    x = x_ref[...]
    var = jnp.mean(x * x, axis=-1, keepdims=True)
    y = x * lax.rsqrt(var + RMS_EPS) * g_ref[...]
    h = y * (1.0 + sc_ref[...]) + sh_ref[...]
    for o, rows in zip(o_refs, token_rows):
        if rows:
            _store_token_rows(o, h, tm)
        else:
            o[...] = h.astype(o.dtype)


def _ada_norm(x, g, m4, sc_idx, sh_idx, seq, out_dtypes, token_rows=None):
    t, d = x.shape
    tm = _tile(seq, 256)
    per_b = seq // tm
    pitch = _row_pitch(d // LANES)
    token_rows = token_rows or (False,) * len(out_dtypes)
    out_specs, out_shape = [], []
    for dt, rows in zip(out_dtypes, token_rows):
        if rows:
            out_specs.append(pl.BlockSpec((tm * pitch, LANES), lambda i: (i, 0)))
            out_shape.append(jax.ShapeDtypeStruct((t * pitch, LANES), F32))
        else:
            out_specs.append(pl.BlockSpec((tm, d), lambda i: (i, 0)))
            out_shape.append(jax.ShapeDtypeStruct((t, d), dt))
    return pl.pallas_call(
        functools.partial(_norm_kernel, tm=tm, token_rows=token_rows),
        grid=(t // tm,),
        in_specs=[
            pl.BlockSpec((tm, d), lambda i: (i, 0)),
            pl.BlockSpec((1, d), lambda i: (0, 0)),
            pl.BlockSpec((None, None, 1, d), lambda i: (i // per_b, sc_idx, 0, 0)),
            pl.BlockSpec((None, None, 1, d), lambda i: (i // per_b, sh_idx, 0, 0)),
        ],
        out_specs=out_specs,
        out_shape=out_shape,
        compiler_params=_params(1),
        name="ada_norm",
    )(x, g, m4, m4)


def _mm_kernel(*refs, n_extra, epilogue, prologue, cast_rows, w_rows_are_outputs):
    x_ref, w_ref, b_ref = refs[:3]
    extra = refs[3:3 + n_extra]
    o_ref = refs[3 + n_extra]
    wbf_ref = refs[4 + n_extra]
    if len(w_ref.shape) == 3:
        w_ref = w_ref.at[0]

    @pl.when(pl.program_id(1) == 0)
    def _():
        def body(r, c):
            rows = pl.ds(pl.multiple_of(r * cast_rows, cast_rows), cast_rows)
            wbf_ref[rows, :] = w_ref[rows, :].astype(BF16)
            return c

        lax.fori_loop(0, w_ref.shape[0] // cast_rows, body, 0)

    x = x_ref[...]
    if prologue is not None:
        x = prologue(x)
    x = x.astype(BF16)
    if w_rows_are_outputs:
        acc = _qk(x, wbf_ref[...])
    else:
        acc = jnp.dot(x, wbf_ref[...], preferred_element_type=F32)
    o_ref[...] = epilogue(acc + b_ref[...], *[e[...] for e in extra]).astype(o_ref.dtype)


def _matmul(x, w, w_spec, k, n, bias, *, tm, tn, out_dtype, epilogue=None, extra=(),
            extra_specs=(), prologue=None, w_rows_are_outputs=False, name="matmul"):
    m = x.shape[0]
    if epilogue is None:
        epilogue = lambda acc: acc
    w_block = (tn, k) if w_rows_are_outputs else (k, tn)
    kern = functools.partial(_mm_kernel, n_extra=len(extra), epilogue=epilogue, prologue=prologue,
                             cast_rows=_tile(w_block[0], 256), w_rows_are_outputs=w_rows_are_outputs)
    return pl.pallas_call(
        kern,
        grid=(n // tn, m // tm),
        in_specs=[pl.BlockSpec((tm, k), lambda j, i: (i, 0)),
                  w_spec,
                  pl.BlockSpec((1, tn), lambda j, i: (0, j)),
                  *extra_specs],
        out_specs=pl.BlockSpec((tm, tn), lambda j, i: (i, j)),
        out_shape=jax.ShapeDtypeStruct((m, n), out_dtype),
        scratch_shapes=[pltpu.VMEM(w_block, BF16)],
        compiler_params=_params(2),
        name=name,
    )(x, w, bias, *extra)


def _rope_epilogue(acc, tab, *, groups):
    outs = []
    for g, kind in enumerate(groups):
        xg = acc[:, g * LANES:(g + 1) * LANES]
        if kind is not None:
            slot, half = kind
            base = 3 * slot * LANES
            cos = tab[:, base:base + LANES]
            sin_lo = tab[:, base + LANES:base + 2 * LANES]
            sin_hi = tab[:, base + 2 * LANES:base + 3 * LANES]
            xg = (xg * cos + pltpu.roll(xg, half, 1) * sin_lo
                  + pltpu.roll(xg, LANES - half, 1) * sin_hi)
        outs.append(xg)
    return outs[0] if len(outs) == 1 else jnp.concatenate(outs, axis=1)


def _rope_table(pos, period, active):
    rot = period // ROPE_FRACTION
    half = rot // 2
    inv_freq = jnp.float32(ROPE_THETA) ** (-jnp.arange(half, dtype=F32) / half)
    lane = np.arange(LANES)
    r = lane % period
    on = lane < active
    first = on & (r < half)
    second = on & (r >= half) & (r < rot)
    fidx = np.where(first, r, np.where(second, r - half, 0))
    ang = pos[:, None] * inv_freq[fidx][None, :]
    cos, sin = jnp.cos(ang), jnp.sin(ang)
    c = jnp.where((first | second)[None, :], cos, 1.0)
    s_lo = jnp.where(second[None, :], sin, 0.0)
    s_hi = jnp.where(first[None, :], -sin, 0.0)
    return jnp.concatenate([c, s_lo, s_hi], axis=1)


def _qk(q, k):
    return lax.dot_general(q, k, (((1,), (1,)), ((), ())), preferred_element_type=F32)


def _softmax_tile(carry, s, v):
    m, l, acc = carry
    m_new = jnp.maximum(m, jnp.max(s, axis=-1, keepdims=True))
    alpha = jnp.exp2(m - m_new)
    p = jnp.exp2(s - m_new)
    l = alpha * l + jnp.sum(p, axis=-1, keepdims=True)
    acc = alpha * acc + jnp.dot(p.astype(BF16), v, preferred_element_type=F32)
    return m_new, l, acc


def _softmax_init(rows):
    return (jnp.full((rows, 1), NEG, F32), jnp.zeros((rows, 1), F32),
            jnp.zeros((rows, HEAD_DIM), F32))


def _rows(ref, j, t):
    return ref[pl.ds(pl.multiple_of(j * t, t), t), :]


def _key_chunks(i, tq, tk):
    full = [(s, min(tk, i * tq - s), False) for s in range(0, i * tq, tk)]
    return full + [(i * tq, tq, True)]


def _attn_diff_kernel(dl_ref, g_ref, q_ref, k_ref, v_ref, o_ref, *, tq, tk, seq, lam_init):
    half = HEAD_DIM // 2
    lane = lax.broadcasted_iota(I32, (tq, HEAD_DIM), 1)
    r = lax.broadcasted_iota(I32, (2 * tq, tq), 0)
    r = jnp.where(r >= tq, r - tq, r)
    c = lax.broadcasted_iota(I32, (2 * tq, tq), 1)
    vis = c < (r // CHUNK + 1) * CHUNK
    dl = dl_ref[...]
    lam = (jnp.exp(jnp.sum(dl[0:1] * dl[1:2], axis=-1, keepdims=True))
           - jnp.exp(jnp.sum(dl[2:3] * dl[3:4], axis=-1, keepdims=True)) + lam_init)

    for i in range(seq // tq):
        q = q_ref[i * tq:(i + 1) * tq, :]
        zero = jnp.zeros_like(q)
        qz = jnp.concatenate([jnp.where(lane < half, q, zero), jnp.where(lane >= half, q, zero)], axis=0)
        carry = _softmax_init(2 * tq)
        for k0, kw, diag in _key_chunks(i, tq, tk):
            s = _qk(qz, k_ref[k0:k0 + kw, :])
            if diag:
                s = jnp.where(vis, s, NEG)
            carry = _softmax_tile(carry, s, v_ref[k0:k0 + kw, :])
        _, l, acc = carry
        o = acc / l
        o = o[:tq] - lam * o[tq:]
        var = jnp.mean(o * o, axis=-1, keepdims=True)
        o_ref[i * tq:(i + 1) * tq, :] = (o * lax.rsqrt(var + RMS_EPS) * g_ref[...]
                                         * (1.0 - lam_init)).astype(o_ref.dtype)


def _attn_forget_kernel(dq_ref, dk_ref, q_ref, k_ref, v_ref, o_ref, *, tq, tk, seq):
    r = lax.broadcasted_iota(I32, (tq, tq), 0)
    c = lax.broadcasted_iota(I32, (tq, tq), 1)
    for i in range(seq // tq):
        q = q_ref[i * tq:(i + 1) * tq, :]
        dq2 = dq_ref[i * tq:(i + 1) * tq, :] * LOG2E
        carry = _softmax_init(tq)
        for k0, kw, diag in _key_chunks(i, tq, tk):
            dk2 = dk_ref[:, k0:k0 + kw] * LOG2E
            s = _qk(q, k_ref[k0:k0 + kw, :]) + dq2 - dk2
            if diag:
                s = jnp.where(c <= r, s, NEG)
            carry = _softmax_tile(carry, s, v_ref[k0:k0 + kw, :])
        _, l, acc = carry
        o_ref[i * tq:(i + 1) * tq, :] = (acc / l).astype(o_ref.dtype)


def _log_sigmoid(z):
    return jnp.minimum(z, 0.0) - jnp.log(1.0 + jnp.exp(-jnp.abs(z)))


def _attn_stick_kernel(q_ref, k_ref, v_ref, o_ref, *, tq, tk, seq):
    r = lax.broadcasted_iota(I32, (tq, tq), 0)
    c = lax.broadcasted_iota(I32, (tq, tq), 1)
    cb = min(tq, 256)
    later = jnp.where(lax.broadcasted_iota(I32, (cb, cb), 0) > lax.broadcasted_iota(I32, (cb, cb), 1),
                      1.0, 0.0).astype(BF16)

    def tile(q, k0, kw, carry, mask):
        suffix, acc = carry
        z = _qk(q, k_ref[k0:k0 + kw, :])
        log_beta = _log_sigmoid(z)
        log_keep = log_beta - z
        if mask is not None:
            log_keep = jnp.where(mask, log_keep, 0.0)
        parts = [None] * (kw // cb)
        for blk in reversed(range(kw // cb)):
            lk = log_keep[:, blk * cb:(blk + 1) * cb]
            hi = lk.astype(BF16)
            lo = (lk - hi.astype(F32)).astype(BF16)
            parts[blk] = (jnp.dot(hi, later, preferred_element_type=F32)
                          + jnp.dot(lo, later, preferred_element_type=F32) + suffix)
            suffix = suffix + jnp.sum(lk, axis=-1, keepdims=True)
        after = parts[0] if len(parts) == 1 else jnp.concatenate(parts, axis=1)
        w = jnp.exp(log_beta + after)
        if mask is not None:
            w = jnp.where(mask, w, 0.0)
        acc = acc + jnp.dot(w.astype(BF16), v_ref[k0:k0 + kw, :], preferred_element_type=F32)
        return suffix, acc

    for i in range(seq // tq):
        q = q_ref[i * tq:(i + 1) * tq, :]
        carry = (jnp.zeros((tq, 1), F32), jnp.zeros((tq, HEAD_DIM), F32))
        for k0, kw, diag in reversed(_key_chunks(i, tq, tk)):
            carry = tile(q, k0, kw, carry, (c < r) if diag else None)
        o_ref[i * tq:(i + 1) * tq, :] = carry[1].astype(o_ref.dtype)


def _attention(kernel, q_arr, q_col, k_arr, k_col, v_arr, v_col, heads, batch, seq,
               pre=(), pre_specs=(), name="attention"):
    t = batch * seq
    return pl.pallas_call(
        kernel,
        grid=(batch, heads),
        in_specs=[*pre_specs,
                  pl.BlockSpec((seq, HEAD_DIM), lambda b, h: (b, q_col + h)),
                  pl.BlockSpec((seq, HEAD_DIM), lambda b, h: (b, k_col + h)),
                  pl.BlockSpec((seq, HEAD_DIM), lambda b, h: (b, v_col + h))],
        out_specs=pl.BlockSpec((seq, HEAD_DIM), lambda b, h: (b, h)),
        out_shape=jax.ShapeDtypeStruct((t, heads * HEAD_DIM), BF16),
        compiler_params=_params(2),
        name=name,
    )(*pre, q_arr, k_arr, v_arr)


def _sortable(x):
    bits = pltpu.bitcast(x + 0.0, I32)
    return jnp.where(bits < 0, bits ^ jnp.int32(0x7FFFFFFF), bits)


def _attn_sparse_kernel(q_ref, sq_ref, kv_ref, ik_ref, o_ref, key_ref, thr_ref, *, tq, seq, heads,
                        topk, iq_col, iw_col):
    i = pl.program_id(1)
    n_tiles = seq // tq
    q0 = i * tq

    sq = sq_ref[...]
    lane_k = lax.broadcasted_iota(I32, (tq, LANES), 1)
    iq_groups = [sq[:, iq_col + g * LANES: iq_col + (g + 1) * LANES].astype(BF16)
                 for g in range(IDX_HEADS * IDX_DIM // LANES)]
    iw = [jnp.broadcast_to(sq[:, iw_col + h: iw_col + h + 1], (tq, tq)) for h in range(IDX_HEADS)]
    row = lax.broadcasted_iota(I32, (tq, tq), 0)
    col = lax.broadcasted_iota(I32, (tq, tq), 1)
    chunk_end = ((q0 + row) // CHUNK + 1) * CHUNK

    key_ref[...] = jnp.full((tq, seq), INT_MIN, I32)

    def score_tile(j, c):
        ikt = _rows(ik_ref, j, tq)
        ik_lo = jnp.where(lane_k < IDX_DIM, ikt, 0.0)
        ik_hi = pltpu.roll(ik_lo, IDX_DIM, 1)
        ik_lo = ik_lo.astype(BF16)
        ik_hi = ik_hi.astype(BF16)
        score = jnp.zeros((tq, tq), F32)
        for h in range(IDX_HEADS):
            s = _qk(iq_groups[h // 2], ik_hi if h % 2 else ik_lo)
            score = score + iw[h] * jnp.maximum(s, 0.0)
        vis = (j * tq + col) < chunk_end
        key_ref[:, pl.ds(pl.multiple_of(j * tq, tq), tq)] = jnp.where(vis, _sortable(score), INT_MIN)
        return c

    lax.fori_loop(0, i + 1, score_tile, 0)

    def count_ge(cand, ncols=seq):
        return jnp.sum(jnp.where(key_ref[:, :ncols] >= cand, 1.0, 0.0), axis=-1, keepdims=True)

    kf = jnp.float32(topk)

    def find_threshold(ncols):
        thr = jnp.where(count_ge(jnp.zeros((tq, 1), I32), ncols) >= kf, 0, INT_MIN).astype(I32)

        def bisect(b, thr):
            cand = thr | lax.shift_left(jnp.int32(1), 30 - b)
            return jnp.where(count_ge(cand, ncols) >= kf, cand, thr)

        thr = lax.fori_loop(0, jnp.where((i + 1) * tq <= topk, 0, 31), bisect, thr)
        thr_ref[...] = jnp.maximum(thr, INT_MIN + 1)

    lo = 0
    for hi in sorted({min(n_tiles, v) for v in range(2, n_tiles + 2, 2)}):
        pl.when((i + 1 > lo) & (i + 1 <= hi))(functools.partial(find_threshold, hi * tq))
        lo = hi
    thr = thr_ref[...]
    n_ge = count_ge(thr)

    @pl.when(jnp.max(n_ge) > kf)
    def _():
        need = kf - jnp.sum(jnp.where(key_ref[...] > thr, 1.0, 0.0), axis=-1, keepdims=True)
        upto = jnp.where(row <= col, 1.0, 0.0).astype(BF16)

        def fix(j, seen):
            cols = pl.ds(pl.multiple_of(j * tq, tq), tq)
            kt = key_ref[:, cols]
            eq = kt == thr
            rank = jnp.dot(jnp.where(eq, 1.0, 0.0).astype(BF16), upto, preferred_element_type=F32) + seen
            key_ref[:, cols] = jnp.where(eq & (rank > need), INT_MIN, kt)
            return seen + jnp.sum(jnp.where(eq, 1.0, 0.0), axis=-1, keepdims=True)

        lax.fori_loop(0, n_tiles, fix, jnp.zeros((tq, 1), F32))

    qs = jnp.concatenate([q_ref[:, h * HEAD_DIM:(h + 1) * HEAD_DIM] for h in range(heads)], axis=0)

    def attn_tile(j, carry):
        kvt = _rows(kv_ref, j, tq)
        kt = kvt[:, :HEAD_DIM].astype(BF16)
        vt = kvt[:, HEAD_DIM:].astype(BF16)
        bias = jnp.where(key_ref[:, pl.ds(pl.multiple_of(j * tq, tq), tq)] >= thr, 0.0, NEG)
        s = _qk(qs, kt).reshape(heads, tq, tq) + bias[None]
        return _softmax_tile(carry, s.reshape(heads * tq, tq), vt)

    _, l, acc = lax.fori_loop(0, i + 1, attn_tile, _softmax_init(heads * tq))
    o = acc / l
    for h in range(heads):
        o_ref[:, h * HEAD_DIM:(h + 1) * HEAD_DIM] = o[h * tq:(h + 1) * tq].astype(o_ref.dtype)


def _attn_sparse(dq, small, heads, batch, seq, tq, topk, iq_col, ik_col, iw_col):
    nq = seq // tq
    t = batch * seq
    bw = heads * HEAD_DIM
    w = small.shape[1]
    kern = functools.partial(_attn_sparse_kernel, tq=tq, seq=seq, heads=heads, topk=topk,
                             iq_col=iq_col, iw_col=iw_col)
    return pl.pallas_call(
        kern,
        grid=(batch, nq),
        in_specs=[pl.BlockSpec((tq, bw), lambda b, i: (b * nq + i, 0)),
                  pl.BlockSpec((tq, w), lambda b, i: (b * nq + i, 0)),
                  pl.BlockSpec((seq, 2 * HEAD_DIM), lambda b, i: (b, 0)),
                  pl.BlockSpec((seq, LANES), lambda b, i: (b, ik_col // LANES))],
        out_specs=pl.BlockSpec((tq, bw), lambda b, i: (b * nq + i, 0)),
        out_shape=jax.ShapeDtypeStruct((t, bw), BF16),
        scratch_shapes=[pltpu.VMEM((tq, seq), I32), pltpu.VMEM((tq, 1), I32)],
        compiler_params=_params(2),
        name="attn_sparse",
    )(dq, small, small, small)


def _forget_cumsum_kernel(x_ref, b_ref, o_ref, *, seq, blk):
    r = lax.broadcasted_iota(I32, (blk, blk), 0)
    c = lax.broadcasted_iota(I32, (blk, blk), 1)
    tri = jnp.where(c <= r, 1.0, 0.0).astype(BF16)
    carry = jnp.zeros((1, LANES), F32)
    for s in range(seq // blk):
        logf = _log_sigmoid(x_ref[s * blk:(s + 1) * blk, :] + b_ref[...])
        hi = logf.astype(BF16)
        rem = logf - hi.astype(F32)
        mid = rem.astype(BF16)
        lo = (rem - mid.astype(F32)).astype(BF16)
        local = (jnp.dot(tri, hi, preferred_element_type=F32)
                 + jnp.dot(tri, mid, preferred_element_type=F32)
                 + jnp.dot(tri, lo, preferred_element_type=F32)) + carry
        o_ref[s * blk:(s + 1) * blk, :] = local
        carry = local[blk - 1:blk, :]


def _forget_cumsum(small, bias_row, batch, seq, col):
    return pl.pallas_call(
        functools.partial(_forget_cumsum_kernel, seq=seq, blk=_tile(seq, 256)),
        grid=(batch,),
        in_specs=[pl.BlockSpec((seq, LANES), lambda b: (b, col // LANES)),
                  pl.BlockSpec((1, LANES), lambda b: (0, 0))],
        out_specs=pl.BlockSpec((seq, LANES), lambda b: (b, 0)),
        out_shape=jax.ShapeDtypeStruct((batch * seq, LANES), F32),
        compiler_params=_params(1),
        name="forget_cumsum",
    )(small, bias_row)


def _merge_kernel(*refs, n_br, cast_rows):
    o_refs = refs[:n_br]
    g_refs = refs[n_br:2 * n_br]
    w_ref = refs[2 * n_br]
    out_ref = refs[2 * n_br + 1]
    wbf_ref = refs[2 * n_br + 2]

    @pl.when(pl.program_id(1) == 0)
    def _():
        for b in range(n_br):
            def body(r, c, b=b):
                rows = pl.ds(pl.multiple_of(r * cast_rows, cast_rows), cast_rows)
                wbf_ref[b, rows, :] = w_ref[b, rows, :].astype(BF16)
                return c
            lax.fori_loop(0, w_ref.shape[1] // cast_rows, body, 0)

    acc = None
    for b in range(n_br):
        term = g_refs[b][...] * jnp.dot(o_refs[b][...], wbf_ref[b], preferred_element_type=F32)
        acc = term if acc is None else acc + term
    out_ref[...] = acc.astype(out_ref.dtype)


def _merge(outs, gates, w_branch, layer, tm, tn):
    _, n_br, bw, d = w_branch.shape
    t = outs[0].shape[0]
    nd = d // tn
    kern = functools.partial(_merge_kernel, n_br=n_br, cast_rows=_tile(bw, 256))
    return pl.pallas_call(
        kern,
        grid=(nd, t // tm),
        in_specs=[*[pl.BlockSpec((tm, bw), lambda j, i: (i, 0)) for _ in range(n_br)],
                  *[pl.BlockSpec((tm, tn), lambda j, i, b=b: (i, b * nd + j)) for b in range(n_br)],
                  pl.BlockSpec((None, n_br, bw, tn), lambda j, i: (layer, 0, 0, j),
                               pipeline_mode=pl.Buffered(1))],
        out_specs=pl.BlockSpec((tm, tn), lambda j, i: (i, j)),
        out_shape=jax.ShapeDtypeStruct((t, d), BF16),
        scratch_shapes=[pltpu.VMEM((n_br, bw, tn), BF16)],
        compiler_params=_params(2),
        name="gated_merge",
    )(*outs, *([gates] * n_br), w_branch)


def _router_kernel(h_ref, w_ref, b_ref, meta_i_ref, meta_w_ref, cnt_ref, run_ref, *, tm, n_exp):
    @pl.when(pl.program_id(0) == 0)
    def _():
        run_ref[...] = jnp.zeros_like(run_ref)

    lane = lax.broadcasted_iota(I32, (tm, LANES), 1)
    lane_f = lane.astype(F32)
    logits = jnp.dot(h_ref[...], w_ref[...].astype(BF16), preferred_element_type=F32) + b_ref[...]
    logits = jnp.where(lane < n_exp, logits, -jnp.inf)

    vals, idxs = [], []
    work = logits
    for _ in range(TOP_K):
        v = jnp.max(work, axis=-1, keepdims=True)
        ix = jnp.min(jnp.where(work == v, lane_f, float(LANES)), axis=-1, keepdims=True).astype(I32)
        vals.append(v)
        idxs.append(ix)
        work = jnp.where(lane == ix, -jnp.inf, work)
    es = [jnp.exp(v - vals[0]) for v in vals]
    denom = es[0] + es[1] + es[2] + es[3]

    onehots = [jnp.where(lane == ix, 1.0, 0.0) for ix in idxs]
    sel = onehots[0] + onehots[1] + onehots[2] + onehots[3]
    r = lax.broadcasted_iota(I32, (tm, tm), 0)
    c = lax.broadcasted_iota(I32, (tm, tm), 1)
    before = jnp.where(c < r, 1.0, 0.0).astype(BF16)
    rank = jnp.dot(before, sel.astype(BF16), preferred_element_type=F32) + run_ref[...]
    run_ref[...] = run_ref[...] + jnp.sum(sel, axis=0, keepdims=True)

    meta_i = jnp.zeros((tm, LANES), I32)
    meta_w = jnp.zeros((tm, LANES), F32)
    for k in range(TOP_K):
        rk = jnp.sum(onehots[k] * rank, axis=-1, keepdims=True).astype(I32)
        meta_i = jnp.where(lane == k, idxs[k], meta_i)
        meta_i = jnp.where(lane == TOP_K + k, rk, meta_i)
        meta_w = jnp.where(lane == k, es[k] / denom, meta_w)
    meta_i_ref[...] = meta_i
    meta_w_ref[...] = meta_w
    cnt_ref[...] = run_ref[...]


def _router(h, w_router_pad, b_router_pad, n_exp):
    t, d = h.shape
    tm = _tile(t, 256)
    kern = functools.partial(_router_kernel, tm=tm, n_exp=n_exp)
    return pl.pallas_call(
        kern,
        grid=(t // tm,),
        in_specs=[pl.BlockSpec((tm, d), lambda i: (i, 0)),
                  pl.BlockSpec((d, LANES), lambda i: (0, 0)),
                  pl.BlockSpec((1, LANES), lambda i: (0, 0))],
        out_specs=[pl.BlockSpec((tm, LANES), lambda i: (i, 0)),
                   pl.BlockSpec((tm, LANES), lambda i: (i, 0)),
                   pl.BlockSpec((1, LANES), lambda i: (0, 0))],
        out_shape=[jax.ShapeDtypeStruct((t, LANES), I32),
                   jax.ShapeDtypeStruct((t, LANES), F32),
                   jax.ShapeDtypeStruct((1, LANES), F32)],
        scratch_shapes=[pltpu.VMEM((1, LANES), F32)],
        compiler_params=_params(1),
        name="router",
    )(h, w_router_pad, b_router_pad)


def _ffn_kernel(te_ref, nu_ref, tok_ref, h_ref, wgu_hbm, bgu_ref, wdn_hbm, bdn_ref, o_ref,
                xbuf, wgu_buf, wdn_buf, pick_ref, sem, wsem, *, tm, nseg, kc, nc, layer):
    i = pl.program_id(0)
    n_used = nu_ref[0]
    d, ff = wgu_buf.shape[0], wdn_buf.shape[0]
    pitch = _row_pitch(nseg)
    n_kc, n_nc = d // kc, d // nc

    def fetch(tile, slot):
        def body(r2, c):
            for lane_q in range(2):
                r = 2 * r2 + lane_q
                tok = tok_ref[tile * tm + r]
                pltpu.make_async_copy(h_ref.at[pl.ds(tok * pitch, nseg), :],
                                      xbuf.at[slot, pl.ds(r * pitch, nseg), :],
                                      sem.at[slot]).start(priority=lane_q)
            return c
        lax.fori_loop(0, tm // 2, body, 0, unroll=2)

    def gu_copy(e, c):
        rows = pl.ds(c * kc, kc)
        return pltpu.make_async_copy(wgu_hbm.at[layer, e, rows, :], wgu_buf.at[rows, :], wsem.at[c])

    def dn_copy(e, c):
        cols = pl.ds(c * nc, nc)
        return pltpu.make_async_copy(wdn_hbm.at[layer, e, :, cols], wdn_buf.at[:, cols],
                                     wsem.at[n_kc + c])

    active = i < n_used
    new_expert = (i == 0) | (te_ref[i] != te_ref[jnp.maximum(i - 1, 0)])

    @pl.when(active & new_expert)
    def _():
        for c in range(n_kc):
            gu_copy(te_ref[i], c).start()
        for c in range(n_nc):
            dn_copy(te_ref[i], c).start()

    @pl.when(i == 0)
    def _():
        r = lax.broadcasted_iota(I32, (2 * ff, ff), 0)
        c = lax.broadcasted_iota(I32, (2 * ff, ff), 1)
        pick_ref[...] = jnp.where(r == 2 * c, 1.0, 0.0).astype(BF16)
        fetch(0, 0)

    @pl.when(i + 1 < n_used)
    def _():
        fetch(i + 1, (i + 1) % 2)

    def compute(arriving):
        slot = i % 2
        done = xbuf.at[slot, pl.ds(0, tm * nseg), :]
        pltpu.make_async_copy(done, done, sem.at[slot]).wait()
        parts = _load_token_rows(xbuf, tm, nseg, lead=(slot,))
        per = kc // LANES
        gu = bgu_ref[...]
        for c in range(n_kc):
            if arriving is not None:
                gu_copy(arriving, c).wait()
            xc = jnp.concatenate(parts[c * per:(c + 1) * per], axis=1).astype(BF16)
            gu = gu + jnp.dot(xc, wgu_buf[c * kc:(c + 1) * kc, :].astype(BF16),
                              preferred_element_type=F32)
        lin = pltpu.roll(gu, 2 * ff - 1, 1)
        glu = jnp.minimum(gu, SWIGLU_LIMIT)
        lin = jnp.clip(lin, -SWIGLU_LIMIT, SWIGLU_LIMIT)
        act = glu * jax.nn.sigmoid(SWIGLU_ALPHA * glu) * (lin + 1.0)
        act = jnp.dot(act.astype(BF16), pick_ref[...], preferred_element_type=F32).astype(BF16)
        for c in range(n_nc):
            if arriving is not None:
                dn_copy(arriving, c).wait()
            cols = slice(c * nc, (c + 1) * nc)
            o_ref[:, cols] = (jnp.dot(act, wdn_buf[:, cols].astype(BF16), preferred_element_type=F32)
                              + bdn_ref[:, cols])

    pl.when(active & new_expert)(lambda: compute(te_ref[i]))
    pl.when(active & jnp.logical_not(new_expert))(lambda: compute(None))

    @pl.when(i >= n_used)
    def _():
        o_ref[...] = jnp.zeros_like(o_ref)


def _expert_ffn(h_rows, tile_expert, n_used, token_of_row, w_gu, b_gu, w_dn, b_dn, layer, tm):
    _, n_exp, d, ff2 = w_gu.shape
    ff = ff2 // 2
    nseg = d // LANES
    n_tiles = token_of_row.shape[0] // tm
    kc, nc = _tile(d, 512), _tile(d, 1024)

    def wsel(i, te, nu, tok):
        return (layer, te[i], 0, 0)

    kern = functools.partial(_ffn_kernel, tm=tm, nseg=nseg, kc=kc, nc=nc, layer=layer)
    return pl.pallas_call(
        kern,
        grid_spec=pltpu.PrefetchScalarGridSpec(
            num_scalar_prefetch=3,
            grid=(n_tiles,),
            in_specs=[pl.BlockSpec(memory_space=pl.ANY),
                      pl.BlockSpec(memory_space=pl.ANY),
                      pl.BlockSpec((None, None, 1, ff2), wsel),
                      pl.BlockSpec(memory_space=pl.ANY),
                      pl.BlockSpec((None, None, 1, d), wsel)],
            out_specs=pl.BlockSpec((tm, d), lambda i, te, nu, tok: (i, 0)),
            scratch_shapes=[pltpu.VMEM((2, tm * _row_pitch(nseg), LANES), F32),
                            pltpu.VMEM((d, ff2), F32),
                            pltpu.VMEM((ff, d), F32),
                            pltpu.VMEM((ff2, ff), BF16),
                            pltpu.SemaphoreType.DMA((2,)),
                            pltpu.SemaphoreType.DMA((d // kc + d // nc,))]),
        out_shape=jax.ShapeDtypeStruct((n_tiles * tm, d), F32),
        compiler_params=_params(1),
        name="expert_ffn",
    )(tile_expert, n_used, token_of_row, h_rows, w_gu, b_gu, w_dn, b_dn)


def _combine_kernel(pos_ref, ys_ref, x_ref, g_ref, w_ref, o_ref, buf, sem, *, tm):
    i = pl.program_id(0)

    def fetch(tile, slot):
        def body(t, c):
            for k in range(TOP_K):
                row = pos_ref[(tile * tm + t) * TOP_K + k]
                pltpu.make_async_copy(ys_ref.at[pl.ds(row, 1), :],
                                      buf.at[slot, k, pl.ds(t, 1), :],
                                      sem.at[slot]).start(priority=k % 2)
            return c
        lax.fori_loop(0, tm, body, 0, unroll=2)

    @pl.when(i == 0)
    def _():
        fetch(0, 0)

    @pl.when(i + 1 < pl.num_programs(0))
    def _():
        fetch(i + 1, (i + 1) % 2)

    slot = i % 2
    pltpu.make_async_copy(buf.at[slot], buf.at[slot], sem.at[slot]).wait()
    w = w_ref[...]
    y = w[:, 0:1] * buf[slot, 0]
    for k in range(1, TOP_K):
        y = y + w[:, k:k + 1] * buf[slot, k]
    o_ref[...] = x_ref[...] + g_ref[...] * y


def _combine(ys, pos_flat, x, m4, gate_idx, meta_w, seq, tm):
    t, d = x.shape
    per_b = seq // tm
    return pl.pallas_call(
        functools.partial(_combine_kernel, tm=tm),
        grid_spec=pltpu.PrefetchScalarGridSpec(
            num_scalar_prefetch=1,
            grid=(t // tm,),
            in_specs=[pl.BlockSpec(memory_space=pl.ANY),
                      pl.BlockSpec((tm, d), lambda i, p: (i, 0)),
                      pl.BlockSpec((None, None, 1, d), lambda i, p: (i // per_b, gate_idx, 0, 0)),
                      pl.BlockSpec((tm, LANES), lambda i, p: (i, 0))],
            out_specs=pl.BlockSpec((tm, d), lambda i, p: (i, 0)),
            scratch_shapes=[pltpu.VMEM((2, TOP_K, tm, d), F32),
                            pltpu.SemaphoreType.DMA((2,))]),
        out_shape=jax.ShapeDtypeStruct((t, d), F32),
        compiler_params=_params(1),
        name="moe_combine",
    )(pos_flat, ys, x, m4, meta_w)


def _moe(h_bf, h_rows, x, m4, w_router, b_router, w_gu, b_gu, w_dn, b_dn, layer, seq):
    t, d = x.shape
    n_exp = w_router.shape[1]
    tm = _tile(t, 256)
    wr = jnp.pad(w_router, ((0, 0), (0, LANES - n_exp)))
    br = jnp.pad(b_router, (0, LANES - n_exp))[None, :]
    meta_i, meta_w, counts = _router(h_bf, wr, br, n_exp)

    counts = counts[0, :n_exp].astype(I32)
    padded = ((counts + tm - 1) // tm) * tm
    ends = jnp.cumsum(padded)
    starts = ends - padded
    n_rows = t * TOP_K + n_exp * tm
    n_tiles = n_rows // tm
    experts = meta_i[:, :TOP_K]
    pos = starts[experts] + meta_i[:, TOP_K:2 * TOP_K]
    tile_start = jnp.arange(n_tiles, dtype=I32) * tm
    tile_expert = jnp.minimum(jnp.sum((ends[None, :] <= tile_start[:, None]).astype(I32), axis=1),
                              n_exp - 1)
    n_used = (ends[-1] // tm).astype(I32)[None]
    token_of_row = jnp.zeros((n_rows,), I32).at[pos.reshape(-1)].set(
        jnp.repeat(jnp.arange(t, dtype=I32), TOP_K))

    ys = _expert_ffn(h_rows, tile_expert, n_used, token_of_row, w_gu, b_gu[:, :, None, :], w_dn,
                     b_dn[:, :, None, :], layer, tm)
    return _combine(ys, pos.reshape(-1), x, m4, 5, meta_w, seq, _tile(seq, 64))


def _col_spec(k, tn, off_blocks=0):
    return pl.BlockSpec((k, tn), lambda j, i: (0, j + off_blocks))


def _layer_col_spec(layer, k, tn, off_blocks=0):
    return pl.BlockSpec((None, k, tn), lambda j, i: (layer, 0, j + off_blocks))


def _layer(x, m4, tabs, lam_init, batch, seq, layer, p):
    t, d = x.shape
    bw = p["w_branch"].shape[2]
    heads = bw // HEAD_DIM
    tm = _tile(seq, 1024)
    tn = _tile(bw, 512)
    w_in_t = p["w_in_t"]
    zero = lambda n: jnp.zeros((1, n), F32)

    (h,) = _ada_norm(x, p["norm1_g"], m4, 1, 0, seq, (BF16,))

    tab8, tab16, tab_small = tabs
    tab_spec = lambda w: pl.BlockSpec((tm, w), lambda j, i: (i, 0))
    rope8 = functools.partial(_rope_epilogue, groups=((0, 8),) * (tn // LANES))
    rope16 = functools.partial(_rope_epilogue, groups=((0, 16),) * (tn // LANES))

    def in_proj(row0, n, name, **kw):
        spec = pl.BlockSpec((pl.Element(1), pl.Element(tn), pl.Element(d)),
                            lambda j, i: (layer, pl.multiple_of(row0 + j * tn, 8), 0))
        return _matmul(h, w_in_t, spec, d, n, zero(n), tm=tm, tn=tn, out_dtype=BF16,
                       w_rows_are_outputs=True, name=name, **kw)

    sm_scale = HEAD_DIM ** -0.5
    col_spec = pl.BlockSpec((1, tn), lambda j, i: (0, j))

    def col_factors(*pieces):
        return jnp.concatenate([jnp.full((1, n), v, F32) for v, n in pieces], axis=1)

    c0 = 6 * bw + heads
    qk_a = in_proj(0, 2 * bw, "proj_a_qk", epilogue=lambda acc, tab, f: rope8(acc, tab) * f,
                   extra=(tab8, col_factors(((HEAD_DIM // 2) ** -0.5 * LOG2E, bw), (1.0, bw))),
                   extra_specs=(tab_spec(3 * LANES), col_spec))
    mid = in_proj(2 * bw, 4 * bw, "proj_av_b", epilogue=lambda acc, f: acc * f,
                  extra=(col_factors((1.0, bw), (sm_scale * LOG2E, bw), (1.0, 2 * bw)),),
                  extra_specs=(col_spec,))
    qkv_c = in_proj(c0, 3 * bw, "proj_c", epilogue=lambda acc, f: acc * f,
                    extra=(col_factors((sm_scale, bw), (1.0, 2 * bw)),), extra_specs=(col_spec,))
    q_d = in_proj(c0 + 3 * bw, bw, "proj_d_q",
                  epilogue=lambda acc, tab: rope16(acc, tab) * (sm_scale * LOG2E),
                  extra=(tab16,), extra_specs=(tab_spec(3 * LANES),))
    s0 = c0 + 4 * bw
    n_iq = IDX_HEADS * IDX_DIM
    used = 2 * HEAD_DIM + n_iq + IDX_DIM + IDX_HEADS + heads
    ws = -(-used // LANES) * LANES
    w_small_t = jnp.concatenate(
        [w_in_t[layer, s0:s0 + 2 * HEAD_DIM + n_iq + IDX_DIM + IDX_HEADS],
         w_in_t[layer, 6 * bw:6 * bw + heads], jnp.zeros((ws - used, d), F32)], axis=0)
    ik_col = 2 * HEAD_DIM + n_iq
    iw_col = ik_col + IDX_DIM
    fl_col = iw_col + IDX_HEADS
    groups_small = ((0, 16), None) + ((1, 8),) * (n_iq // LANES) + ((2, 8),)
    tm_s = _tile(seq, 512)
    small = _matmul(h, w_small_t, pl.BlockSpec((ws, d), lambda j, i: (0, 0), pipeline_mode=pl.Buffered(1)),
                    d, ws, zero(ws), tm=tm_s, tn=ws, out_dtype=F32, w_rows_are_outputs=True,
                    epilogue=functools.partial(_rope_epilogue, groups=groups_small), extra=(tab_small,),
                    extra_specs=(pl.BlockSpec((tm_s, 9 * LANES), lambda j, i: (i, 0)),), name="proj_small")

    fl_lane = fl_col % LANES
    bias_row = jnp.zeros((1, LANES), F32).at[0, fl_lane:fl_lane + heads].set(p["b_forget"])
    dcum = _forget_cumsum(small, bias_row, batch, seq, ik_col)[:, fl_lane:fl_lane + heads]
    dcum = dcum.reshape(batch, seq, heads).transpose(0, 2, 1)
    dq_col = dcum[..., None]
    dk_row = dcum[:, :, None, :]

    tq = tk = _tile(seq, 512)
    hb = bw // HEAD_DIM
    o_a = _attention(
        functools.partial(_attn_diff_kernel, tq=tq, tk=tk, seq=seq, lam_init=lam_init),
        qk_a, 0, qk_a, hb, mid, 0, heads, batch, seq,
        pre=(p["diff_lambda"], p["diff_norm_g"][None, :]),
        pre_specs=(pl.BlockSpec(p["diff_lambda"].shape, lambda b, h: (0, 0)),
                   pl.BlockSpec((1, HEAD_DIM), lambda b, h: (0, 0))),
        name="attn_diff")
    o_b = _attention(
        functools.partial(_attn_forget_kernel, tq=tq, tk=tk, seq=seq),
        mid, hb, mid, 2 * hb, mid, 3 * hb, heads, batch, seq,
        pre=(dq_col, dk_row),
        pre_specs=(pl.BlockSpec((None, None, seq, 1), lambda b, h: (b, h, 0, 0)),
                   pl.BlockSpec((None, None, 1, seq), lambda b, h: (b, h, 0, 0))),
        name="attn_forget")
    o_c = _attention(functools.partial(_attn_stick_kernel, tq=tq, tk=tk, seq=seq),
                     qkv_c, 0, qkv_c, hb, qkv_c, 2 * hb, heads, batch, seq, name="attn_stick")
    o_d = _attn_sparse(q_d, small, heads, batch, seq, _tile(seq, 256), min(TOPK_MAX, seq // 4),
                       2 * HEAD_DIM, ik_col, iw_col)

    n_br = p["w_gate"].shape[1]
    nd = d // tn
    gates = _matmul(h, p["w_gate"],
                    pl.BlockSpec((None, None, d, tn), lambda j, i: (layer, j // nd, 0, j % nd)),
                    d, n_br * d, p["b_gate"].reshape(1, n_br * d), tm=tm, tn=tn, out_dtype=F32,
                    epilogue=jax.nn.sigmoid, name="gates")
    merged = _merge((o_a, o_b, o_c, o_d), gates, p["w_branch"], layer, tm, tn)
    per_b = seq // tm
    x = _matmul(merged, p["w_out"], _layer_col_spec(layer, d, tn), d, d, zero(d), tm=tm, tn=tn,
                out_dtype=F32, epilogue=lambda acc, xr, g: xr + g * acc, extra=(x, m4),
                extra_specs=(pl.BlockSpec((tm, tn), lambda j, i: (i, j)),
                             pl.BlockSpec((None, None, 1, tn), lambda j, i: (i // per_b, 2, 0, j))),
                name="out_proj")

    h_bf, h_rows = _ada_norm(x, p["norm2_g"], m4, 4, 3, seq, (BF16, F32), token_rows=(False, True))
    return _moe(h_bf, h_rows, x, m4, p["w_router"], p["b_router"], p["w_gu"], p["b_gu"],
                p["w_dn"], p["b_dn"], layer, seq)


@jax.jit
def _forward(x, c, positions, w_mod, b_mod, ada_table, norm1_g, w_in, b_forget, diff_lambda,
             diff_norm_g, w_branch, w_gate, b_gate, w_out, norm2_g, w_router, b_router,
             w_gu, b_gu, w_dn, b_dn, final_norm_g):
    batch, seq, d = x.shape
    depth = w_in.shape[0]
    t = batch * seq
    xt = x.reshape(t, d)

    c8 = jnp.pad(c, ((0, 8 - batch), (0, 0)))
    n_mod = w_mod.shape[1]
    mod = _matmul(c8, w_mod, _col_spec(d, _tile(n_mod, 512)), d, n_mod, b_mod[None, :], tm=8,
                  tn=_tile(n_mod, 512), out_dtype=F32, prologue=lambda v: v * jax.nn.sigmoid(v),
                  name="mod_proj")[:batch]
    mod = mod.reshape(batch, N_MOD, d)

    pos = positions.reshape(t).astype(F32)
    tabs = (_rope_table(pos, HEAD_DIM // 2, LANES),
            _rope_table(pos, HEAD_DIM, LANES),
            jnp.concatenate([_rope_table(pos, HEAD_DIM, LANES), _rope_table(pos, IDX_DIM, LANES),
                             _rope_table(pos, IDX_DIM, IDX_DIM)], axis=1))

    w_in_t = jnp.swapaxes(w_in, 1, 2)
    for l in range(depth):
        m4 = (mod + ada_table[l][None])[:, :, None, :]
        lam_init = 0.8 - 0.6 * float(np.exp(-0.3 * l))
        p = dict(norm1_g=norm1_g[l][None, :], w_in_t=w_in_t, b_forget=b_forget[l],
                 diff_lambda=diff_lambda[l], diff_norm_g=diff_norm_g[l], w_branch=w_branch,
                 w_gate=w_gate, b_gate=b_gate[l], w_out=w_out, norm2_g=norm2_g[l][None, :],
                 w_router=w_router[l], b_router=b_router[l], w_gu=w_gu, b_gu=b_gu,
                 w_dn=w_dn, b_dn=b_dn)
        xt = _layer(xt, m4, tabs, lam_init, batch, seq, l, p)

    zeros4 = jnp.zeros((batch, 1, 1, d), F32)
    (out,) = _ada_norm(xt, final_norm_g[None, :], zeros4, 0, 0, seq, (F32,))
    return out.reshape(batch, seq, d)


def kernel(x, c, positions, w_mod, b_mod, ada_table, norm1_g, w_in, b_forget, diff_lambda,
           diff_norm_g, w_branch, w_gate, b_gate, w_out, norm2_g, w_router, b_router,
           w_gu, b_gu, w_dn, b_dn, final_norm_g):
    return _forward(x, c, positions, w_mod, b_mod, ada_table, norm1_g, w_in, b_forget, diff_lambda,
                    diff_norm_g, w_branch, w_gate, b_gate, w_out, norm2_g, w_router, b_router,
                    w_gu, b_gu, w_dn, b_dn, final_norm_g)
```

```python
import functools

import numpy as np
import jax
import jax.numpy as jnp
from jax import lax
from jax.experimental import pallas as pl
from jax.experimental.pallas import tpu as pltpu

F32 = jnp.float32
BF16 = jnp.bfloat16
I32 = jnp.int32

LANES = 128
HEAD_DIM = 128
CHUNK = 64
IDX_HEADS = 8
IDX_DIM = 64
TOPK_MAX = 256
ROPE_THETA = 500000.0
ROPE_FRACTION = 4
TOP_K = 4
SWIGLU_LIMIT = 7.0
SWIGLU_ALPHA = 1.702
RMS_EPS = 1e-5
N_MOD = 6
NEG = -1e30
LOG2E = 1.4426950408889634
INT_MIN = -(2 ** 31)
ROW_PITCH_PAD = 4
VMEM_LIMIT = 56 * 1024 * 1024


def _params(n_axes, vmem=VMEM_LIMIT):
    return pltpu.CompilerParams(dimension_semantics=("arbitrary",) * n_axes,
                                vmem_limit_bytes=vmem)


def _tile(n, pref):
    t = min(n, pref)
    while n % t:
        t //= 2
    return t


def _row_pitch(nseg):
    return nseg + ROW_PITCH_PAD


def _store_token_rows(o_ref, val, tm):
    nseg = val.shape[1] // LANES
    pitch = _row_pitch(nseg)
    for s in range(nseg):
        o_ref[pl.ds(s, tm, stride=pitch), :] = val[:, s * LANES:(s + 1) * LANES]
    for s in range(nseg, pitch):
        o_ref[pl.ds(s, tm, stride=pitch), :] = jnp.zeros((tm, LANES), o_ref.dtype)


def _load_token_rows(ref, tm, nseg, lead=()):
    return [ref[(*lead, pl.ds(s, tm, stride=_row_pitch(nseg)), slice(None))] for s in range(nseg)]


def _norm_kernel(x_ref, g_ref, sc_ref, sh_ref, *o_refs, tm, token_rows):
    x = x_ref[...]
    var = jnp.mean(x * x, axis=-1, keepdims=True)
    y = x * lax.rsqrt(var + RMS_EPS) * g_ref[...]
    h = y * (1.0 + sc_ref[...]) + sh_ref[...]
    for o, rows in zip(o_refs, token_rows):
        if rows:
            _store_token_rows(o, h, tm)
        else:
            o[...] = h.astype(o.dtype)


def _ada_norm(x, g, m4, sc_idx, sh_idx, seq, out_dtypes, token_rows=None):
    t, d = x.shape
    tm = _tile(seq, 256)
    per_b = seq // tm
    pitch = _row_pitch(d // LANES)
    token_rows = token_rows or (False,) * len(out_dtypes)
    out_specs, out_shape = [], []
    for dt, rows in zip(out_dtypes, token_rows):
        if rows:
            out_specs.append(pl.BlockSpec((tm * pitch, LANES), lambda i: (i, 0)))
            out_shape.append(jax.ShapeDtypeStruct((t * pitch, LANES), F32))
        else:
            out_specs.append(pl.BlockSpec((tm, d), lambda i: (i, 0)))
            out_shape.append(jax.ShapeDtypeStruct((t, d), dt))
    return pl.pallas_call(
        functools.partial(_norm_kernel, tm=tm, token_rows=token_rows),
        grid=(t // tm,),
        in_specs=[
            pl.BlockSpec((tm, d), lambda i: (i, 0)),
            pl.BlockSpec((1, d), lambda i: (0, 0)),
            pl.BlockSpec((None, None, 1, d), lambda i: (i // per_b, sc_idx, 0, 0)),
            pl.BlockSpec((None, None, 1, d), lambda i: (i // per_b, sh_idx, 0, 0)),
        ],
        out_specs=out_specs,
        out_shape=out_shape,
        compiler_params=_params(1),
        name="ada_norm",
    )(x, g, m4, m4)


def _mm_kernel(*refs, n_extra, epilogue, prologue, cast_rows, w_rows_are_outputs):
    x_ref, w_ref, b_ref = refs[:3]
    extra = refs[3:3 + n_extra]
    o_ref = refs[3 + n_extra]
    wbf_ref = refs[4 + n_extra]
    if len(w_ref.shape) == 3:
        w_ref = w_ref.at[0]

    @pl.when(pl.program_id(1) == 0)
    def _():
        def body(r, c):
            rows = pl.ds(pl.multiple_of(r * cast_rows, cast_rows), cast_rows)
            wbf_ref[rows, :] = w_ref[rows, :].astype(BF16)
            return c

        lax.fori_loop(0, w_ref.shape[0] // cast_rows, body, 0)

    x = x_ref[...]
    if prologue is not None:
        x = prologue(x)
    x = x.astype(BF16)
    if w_rows_are_outputs:
        acc = _qk(x, wbf_ref[...])
    else:
        acc = jnp.dot(x, wbf_ref[...], preferred_element_type=F32)
    o_ref[...] = epilogue(acc + b_ref[...], *[e[...] for e in extra]).astype(o_ref.dtype)


def _matmul(x, w, w_spec, k, n, bias, *, tm, tn, out_dtype, epilogue=None, extra=(),
            extra_specs=(), prologue=None, w_rows_are_outputs=False, name="matmul"):
    m = x.shape[0]
    if epilogue is None:
        epilogue = lambda acc: acc
    w_block = (tn, k) if w_rows_are_outputs else (k, tn)
    kern = functools.partial(_mm_kernel, n_extra=len(extra), epilogue=epilogue, prologue=prologue,
                             cast_rows=_tile(w_block[0], 256), w_rows_are_outputs=w_rows_are_outputs)
    return pl.pallas_call(
        kern,
        grid=(n // tn, m // tm),
        in_specs=[pl.BlockSpec((tm, k), lambda j, i: (i, 0)),
                  w_spec,
                  pl.BlockSpec((1, tn), lambda j, i: (0, j)),
                  *extra_specs],
        out_specs=pl.BlockSpec((tm, tn), lambda j, i: (i, j)),
        out_shape=jax.ShapeDtypeStruct((m, n), out_dtype),
        scratch_shapes=[pltpu.VMEM(w_block, BF16)],
        compiler_params=_params(2),
        name=name,
    )(x, w, bias, *extra)


def _rope_epilogue(acc, tab, *, groups):
    outs = []
    for g, kind in enumerate(groups):
        xg = acc[:, g * LANES:(g + 1) * LANES]
        if kind is not None:
            slot, half = kind
            base = 3 * slot * LANES
            cos = tab[:, base:base + LANES]
            sin_lo = tab[:, base + LANES:base + 2 * LANES]
            sin_hi = tab[:, base + 2 * LANES:base + 3 * LANES]
            xg = (xg * cos + pltpu.roll(xg, half, 1) * sin_lo
                  + pltpu.roll(xg, LANES - half, 1) * sin_hi)
        outs.append(xg)
    return outs[0] if len(outs) == 1 else jnp.concatenate(outs, axis=1)


def _rope_table(pos, period, active):
    rot = period // ROPE_FRACTION
    half = rot // 2
    inv_freq = jnp.float32(ROPE_THETA) ** (-jnp.arange(half, dtype=F32) / half)
    lane = np.arange(LANES)
    r = lane % period
    on = lane < active
    first = on & (r < half)
    second = on & (r >= half) & (r < rot)
    fidx = np.where(first, r, np.where(second, r - half, 0))
    ang = pos[:, None] * inv_freq[fidx][None, :]
    cos, sin = jnp.cos(ang), jnp.sin(ang)
    c = jnp.where((first | second)[None, :], cos, 1.0)
    s_lo = jnp.where(second[None, :], sin, 0.0)
    s_hi = jnp.where(first[None, :], -sin, 0.0)
    return jnp.concatenate([c, s_lo, s_hi], axis=1)


def _qk(q, k):
    return lax.dot_general(q, k, (((1,), (1,)), ((), ())), preferred_element_type=F32)


def _softmax_tile(carry, s, v):
    m, l, acc = carry
    m_new = jnp.maximum(m, jnp.max(s, axis=-1, keepdims=True))
    alpha = jnp.exp2(m - m_new)
    p = jnp.exp2(s - m_new)
    l = alpha * l + jnp.sum(p, axis=-1, keepdims=True)
    acc = alpha * acc + jnp.dot(p.astype(BF16), v, preferred_element_type=F32)
    return m_new, l, acc


def _softmax_init(rows):
    return (jnp.full((rows, 1), NEG, F32), jnp.zeros((rows, 1), F32),
            jnp.zeros((rows, HEAD_DIM), F32))


def _rows(ref, j, t):
    return ref[pl.ds(pl.multiple_of(j * t, t), t), :]


def _key_chunks(i, tq, tk):
    full = [(s, min(tk, i * tq - s), False) for s in range(0, i * tq, tk)]
    return full + [(i * tq, tq, True)]


def _attn_diff_kernel(dl_ref, g_ref, q_ref, k_ref, v_ref, o_ref, *, tq, tk, seq, lam_init):
    half = HEAD_DIM // 2
    lane = lax.broadcasted_iota(I32, (tq, HEAD_DIM), 1)
    r = lax.broadcasted_iota(I32, (2 * tq, tq), 0)
    r = jnp.where(r >= tq, r - tq, r)
    c = lax.broadcasted_iota(I32, (2 * tq, tq), 1)
    vis = c < (r // CHUNK + 1) * CHUNK
    dl = dl_ref[...]
    lam = (jnp.exp(jnp.sum(dl[0:1] * dl[1:2], axis=-1, keepdims=True))
           - jnp.exp(jnp.sum(dl[2:3] * dl[3:4], axis=-1, keepdims=True)) + lam_init)

    for i in range(seq // tq):
        q = q_ref[i * tq:(i + 1) * tq, :]
        zero = jnp.zeros_like(q)
        qz = jnp.concatenate([jnp.where(lane < half, q, zero), jnp.where(lane >= half, q, zero)], axis=0)
        carry = _softmax_init(2 * tq)
        for k0, kw, diag in _key_chunks(i, tq, tk):
            s = _qk(qz, k_ref[k0:k0 + kw, :])
            if diag:
                s = jnp.where(vis, s, NEG)
            carry = _softmax_tile(carry, s, v_ref[k0:k0 + kw, :])
        _, l, acc = carry
        o = acc / l
        o = o[:tq] - lam * o[tq:]
        var = jnp.mean(o * o, axis=-1, keepdims=True)
        o_ref[i * tq:(i + 1) * tq, :] = (o * lax.rsqrt(var + RMS_EPS) * g_ref[...]
                                         * (1.0 - lam_init)).astype(o_ref.dtype)


def _attn_forget_kernel(dq_ref, dk_ref, q_ref, k_ref, v_ref, o_ref, *, tq, tk, seq):
    r = lax.broadcasted_iota(I32, (tq, tq), 0)
    c = lax.broadcasted_iota(I32, (tq, tq), 1)
    for i in range(seq // tq):
        q = q_ref[i * tq:(i + 1) * tq, :]
        dq2 = dq_ref[i * tq:(i + 1) * tq, :] * LOG2E
        carry = _softmax_init(tq)
        for k0, kw, diag in _key_chunks(i, tq, tk):
            dk2 = dk_ref[:, k0:k0 + kw] * LOG2E
            s = _qk(q, k_ref[k0:k0 + kw, :]) + dq2 - dk2
            if diag:
                s = jnp.where(c <= r, s, NEG)
            carry = _softmax_tile(carry, s, v_ref[k0:k0 + kw, :])
        _, l, acc = carry
        o_ref[i * tq:(i + 1) * tq, :] = (acc / l).astype(o_ref.dtype)


def _log_sigmoid(z):
    return jnp.minimum(z, 0.0) - jnp.log(1.0 + jnp.exp(-jnp.abs(z)))


def _attn_stick_kernel(q_ref, k_ref, v_ref, o_ref, *, tq, tk, seq):
    r = lax.broadcasted_iota(I32, (tq, tq), 0)
    c = lax.broadcasted_iota(I32, (tq, tq), 1)
    cb = min(tq, 256)
    later = jnp.where(lax.broadcasted_iota(I32, (cb, cb), 0) > lax.broadcasted_iota(I32, (cb, cb), 1),
                      1.0, 0.0).astype(BF16)

    def tile(q, k0, kw, carry, mask):
        suffix, acc = carry
        z = _qk(q, k_ref[k0:k0 + kw, :])
        log_beta = _log_sigmoid(z)
        log_keep = log_beta - z
        if mask is not None:
            log_keep = jnp.where(mask, log_keep, 0.0)
        parts = [None] * (kw // cb)
        for blk in reversed(range(kw // cb)):
            lk = log_keep[:, blk * cb:(blk + 1) * cb]
            hi = lk.astype(BF16)
            lo = (lk - hi.astype(F32)).astype(BF16)
            parts[blk] = (jnp.dot(hi, later, preferred_element_type=F32)
                          + jnp.dot(lo, later, preferred_element_type=F32) + suffix)
            suffix = suffix + jnp.sum(lk, axis=-1, keepdims=True)
        after = parts[0] if len(parts) == 1 else jnp.concatenate(parts, axis=1)
        w = jnp.exp(log_beta + after)
        if mask is not None:
            w = jnp.where(mask, w, 0.0)
        acc = acc + jnp.dot(w.astype(BF16), v_ref[k0:k0 + kw, :], preferred_element_type=F32)
        return suffix, acc

    for i in range(seq // tq):
        q = q_ref[i * tq:(i + 1) * tq, :]
        carry = (jnp.zeros((tq, 1), F32), jnp.zeros((tq, HEAD_DIM), F32))
        for k0, kw, diag in reversed(_key_chunks(i, tq, tk)):
            carry = tile(q, k0, kw, carry, (c < r) if diag else None)
        o_ref[i * tq:(i + 1) * tq, :] = carry[1].astype(o_ref.dtype)


def _attention(kernel, q_arr, q_col, k_arr, k_col, v_arr, v_col, heads, batch, seq,
               pre=(), pre_specs=(), name="attention"):
    t = batch * seq
    return pl.pallas_call(
        kernel,
        grid=(batch, heads),
        in_specs=[*pre_specs,
                  pl.BlockSpec((seq, HEAD_DIM), lambda b, h: (b, q_col + h)),
                  pl.BlockSpec((seq, HEAD_DIM), lambda b, h: (b, k_col + h)),
                  pl.BlockSpec((seq, HEAD_DIM), lambda b, h: (b, v_col + h))],
        out_specs=pl.BlockSpec((seq, HEAD_DIM), lambda b, h: (b, h)),
        out_shape=jax.ShapeDtypeStruct((t, heads * HEAD_DIM), BF16),
        compiler_params=_params(2),
        name=name,
    )(*pre, q_arr, k_arr, v_arr)


def _sortable(x):
    bits = pltpu.bitcast(x + 0.0, I32)
    return jnp.where(bits < 0, bits ^ jnp.int32(0x7FFFFFFF), bits)


def _attn_sparse_kernel(q_ref, sq_ref, kv_ref, ik_ref, o_ref, key_ref, thr_ref, *, tq, seq, heads,
                        topk, iq_col, iw_col):
    i = pl.program_id(1)
    n_tiles = seq // tq
    q0 = i * tq

    sq = sq_ref[...]
    lane_k = lax.broadcasted_iota(I32, (tq, LANES), 1)
    iq_groups = [sq[:, iq_col + g * LANES: iq_col + (g + 1) * LANES].astype(BF16)
                 for g in range(IDX_HEADS * IDX_DIM // LANES)]
    iw = [jnp.broadcast_to(sq[:, iw_col + h: iw_col + h + 1], (tq, tq)) for h in range(IDX_HEADS)]
    row = lax.broadcasted_iota(I32, (tq, tq), 0)
    col = lax.broadcasted_iota(I32, (tq, tq), 1)
    chunk_end = ((q0 + row) // CHUNK + 1) * CHUNK

    key_ref[...] = jnp.full((tq, seq), INT_MIN, I32)

    def score_tile(j, c):
        ikt = _rows(ik_ref, j, tq)
        ik_lo = jnp.where(lane_k < IDX_DIM, ikt, 0.0)
        ik_hi = pltpu.roll(ik_lo, IDX_DIM, 1)
        ik_lo = ik_lo.astype(BF16)
        ik_hi = ik_hi.astype(BF16)
        score = jnp.zeros((tq, tq), F32)
        for h in range(IDX_HEADS):
            s = _qk(iq_groups[h // 2], ik_hi if h % 2 else ik_lo)
            score = score + iw[h] * jnp.maximum(s, 0.0)
        vis = (j * tq + col) < chunk_end
        key_ref[:, pl.ds(pl.multiple_of(j * tq, tq), tq)] = jnp.where(vis, _sortable(score), INT_MIN)
        return c

    lax.fori_loop(0, i + 1, score_tile, 0)

    def count_ge(cand, ncols=seq):
        return jnp.sum(jnp.where(key_ref[:, :ncols] >= cand, 1.0, 0.0), axis=-1, keepdims=True)

    kf = jnp.float32(topk)

    def find_threshold(ncols):
        thr = jnp.where(count_ge(jnp.zeros((tq, 1), I32), ncols) >= kf, 0, INT_MIN).astype(I32)

        def bisect(b, thr):
            cand = thr | lax.shift_left(jnp.int32(1), 30 - b)
            return jnp.where(count_ge(cand, ncols) >= kf, cand, thr)

        thr = lax.fori_loop(0, jnp.where((i + 1) * tq <= topk, 0, 31), bisect, thr)
        thr_ref[...] = jnp.maximum(thr, INT_MIN + 1)

    lo = 0
    for hi in sorted({min(n_tiles, v) for v in range(2, n_tiles + 2, 2)}):
        pl.when((i + 1 > lo) & (i + 1 <= hi))(functools.partial(find_threshold, hi * tq))
        lo = hi
    thr = thr_ref[...]
    n_ge = count_ge(thr)

    @pl.when(jnp.max(n_ge) > kf)
    def _():
        need = kf - jnp.sum(jnp.where(key_ref[...] > thr, 1.0, 0.0), axis=-1, keepdims=True)
        upto = jnp.where(row <= col, 1.0, 0.0).astype(BF16)

        def fix(j, seen):
            cols = pl.ds(pl.multiple_of(j * tq, tq), tq)
            kt = key_ref[:, cols]
            eq = kt == thr
            rank = jnp.dot(jnp.where(eq, 1.0, 0.0).astype(BF16), upto, preferred_element_type=F32) + seen
            key_ref[:, cols] = jnp.where(eq & (rank > need), INT_MIN, kt)
            return seen + jnp.sum(jnp.where(eq, 1.0, 0.0), axis=-1, keepdims=True)

        lax.fori_loop(0, n_tiles, fix, jnp.zeros((tq, 1), F32))

    qs = jnp.concatenate([q_ref[:, h * HEAD_DIM:(h + 1) * HEAD_DIM] for h in range(heads)], axis=0)

    def attn_tile(j, carry):
        kvt = _rows(kv_ref, j, tq)
        kt = kvt[:, :HEAD_DIM].astype(BF16)
        vt = kvt[:, HEAD_DIM:].astype(BF16)
        bias = jnp.where(key_ref[:, pl.ds(pl.multiple_of(j * tq, tq), tq)] >= thr, 0.0, NEG)
        s = _qk(qs, kt).reshape(heads, tq, tq) + bias[None]
        return _softmax_tile(carry, s.reshape(heads * tq, tq), vt)

    _, l, acc = lax.fori_loop(0, i + 1, attn_tile, _softmax_init(heads * tq))
    o = acc / l
    for h in range(heads):
        o_ref[:, h * HEAD_DIM:(h + 1) * HEAD_DIM] = o[h * tq:(h + 1) * tq].astype(o_ref.dtype)


def _attn_sparse(dq, small, heads, batch, seq, tq, topk, iq_col, ik_col, iw_col):
    nq = seq // tq
    t = batch * seq
    bw = heads * HEAD_DIM
    w = small.shape[1]
    kern = functools.partial(_attn_sparse_kernel, tq=tq, seq=seq, heads=heads, topk=topk,
                             iq_col=iq_col, iw_col=iw_col)
    return pl.pallas_call(
        kern,
        grid=(batch, nq),
        in_specs=[pl.BlockSpec((tq, bw), lambda b, i: (b * nq + i, 0)),
                  pl.BlockSpec((tq, w), lambda b, i: (b * nq + i, 0)),
                  pl.BlockSpec((seq, 2 * HEAD_DIM), lambda b, i: (b, 0)),
                  pl.BlockSpec((seq, LANES), lambda b, i: (b, ik_col // LANES))],
        out_specs=pl.BlockSpec((tq, bw), lambda b, i: (b * nq + i, 0)),
        out_shape=jax.ShapeDtypeStruct((t, bw), BF16),
        scratch_shapes=[pltpu.VMEM((tq, seq), I32), pltpu.VMEM((tq, 1), I32)],
        compiler_params=_params(2),
        name="attn_sparse",
    )(dq, small, small, small)


def _forget_cumsum_kernel(x_ref, b_ref, o_ref, *, seq, blk):
    r = lax.broadcasted_iota(I32, (blk, blk), 0)
    c = lax.broadcasted_iota(I32, (blk, blk), 1)
    tri = jnp.where(c <= r, 1.0, 0.0).astype(BF16)
    carry = jnp.zeros((1, LANES), F32)
    for s in range(seq // blk):
        logf = _log_sigmoid(x_ref[s * blk:(s + 1) * blk, :] + b_ref[...])
        hi = logf.astype(BF16)
        rem = logf - hi.astype(F32)
        mid = rem.astype(BF16)
        lo = (rem - mid.astype(F32)).astype(BF16)
        local = (jnp.dot(tri, hi, preferred_element_type=F32)
                 + jnp.dot(tri, mid, preferred_element_type=F32)
                 + jnp.dot(tri, lo, preferred_element_type=F32)) + carry
        o_ref[s * blk:(s + 1) * blk, :] = local
        carry = local[blk - 1:blk, :]


def _forget_cumsum(small, bias_row, batch, seq, col):
    return pl.pallas_call(
        functools.partial(_forget_cumsum_kernel, seq=seq, blk=_tile(seq, 256)),
        grid=(batch,),
        in_specs=[pl.BlockSpec((seq, LANES), lambda b: (b, col // LANES)),
                  pl.BlockSpec((1, LANES), lambda b: (0, 0))],
        out_specs=pl.BlockSpec((seq, LANES), lambda b: (b, 0)),
        out_shape=jax.ShapeDtypeStruct((batch * seq, LANES), F32),
        compiler_params=_params(1),
        name="forget_cumsum",
    )(small, bias_row)


def _merge_kernel(*refs, n_br, cast_rows):
    o_refs = refs[:n_br]
    g_refs = refs[n_br:2 * n_br]
    w_ref = refs[2 * n_br]
    out_ref = refs[2 * n_br + 1]
    wbf_ref = refs[2 * n_br + 2]

    @pl.when(pl.program_id(1) == 0)
    def _():
        for b in range(n_br):
            def body(r, c, b=b):
                rows = pl.ds(pl.multiple_of(r * cast_rows, cast_rows), cast_rows)
                wbf_ref[b, rows, :] = w_ref[b, rows, :].astype(BF16)
                return c
            lax.fori_loop(0, w_ref.shape[1] // cast_rows, body, 0)

    acc = None
    for b in range(n_br):
        term = g_refs[b][...] * jnp.dot(o_refs[b][...], wbf_ref[b], preferred_element_type=F32)
        acc = term if acc is None else acc + term
    out_ref[...] = acc.astype(out_ref.dtype)


def _merge(outs, gates, w_branch, layer, tm, tn):
    _, n_br, bw, d = w_branch.shape
    t = outs[0].shape[0]
    nd = d // tn
    kern = functools.partial(_merge_kernel, n_br=n_br, cast_rows=_tile(bw, 256))
    return pl.pallas_call(
        kern,
        grid=(nd, t // tm),
        in_specs=[*[pl.BlockSpec((tm, bw), lambda j, i: (i, 0)) for _ in range(n_br)],
                  *[pl.BlockSpec((tm, tn), lambda j, i, b=b: (i, b * nd + j)) for b in range(n_br)],
                  pl.BlockSpec((None, n_br, bw, tn), lambda j, i: (layer, 0, 0, j),
                               pipeline_mode=pl.Buffered(1))],
        out_specs=pl.BlockSpec((tm, tn), lambda j, i: (i, j)),
        out_shape=jax.ShapeDtypeStruct((t, d), BF16),
        scratch_shapes=[pltpu.VMEM((n_br, bw, tn), BF16)],
        compiler_params=_params(2),
        name="gated_merge",
    )(*outs, *([gates] * n_br), w_branch)


def _router_kernel(h_ref, w_ref, b_ref, meta_i_ref, meta_w_ref, cnt_ref, run_ref, *, tm, n_exp):
    @pl.when(pl.program_id(0) == 0)
    def _():
        run_ref[...] = jnp.zeros_like(run_ref)

    lane = lax.broadcasted_iota(I32, (tm, LANES), 1)
    lane_f = lane.astype(F32)
    logits = jnp.dot(h_ref[...], w_ref[...].astype(BF16), preferred_element_type=F32) + b_ref[...]
    logits = jnp.where(lane < n_exp, logits, -jnp.inf)

    vals, idxs = [], []
    work = logits
    for _ in range(TOP_K):
        v = jnp.max(work, axis=-1, keepdims=True)
        ix = jnp.min(jnp.where(work == v, lane_f, float(LANES)), axis=-1, keepdims=True).astype(I32)
        vals.append(v)
        idxs.append(ix)
        work = jnp.where(lane == ix, -jnp.inf, work)
    es = [jnp.exp(v - vals[0]) for v in vals]
    denom = es[0] + es[1] + es[2] + es[3]

    onehots = [jnp.where(lane == ix, 1.0, 0.0) for ix in idxs]
    sel = onehots[0] + onehots[1] + onehots[2] + onehots[3]
    r = lax.broadcasted_iota(I32, (tm, tm), 0)
    c = lax.broadcasted_iota(I32, (tm, tm), 1)
    before = jnp.where(c < r, 1.0, 0.0).astype(BF16)
    rank = jnp.dot(before, sel.astype(BF16), preferred_element_type=F32) + run_ref[...]
    run_ref[...] = run_ref[...] + jnp.sum(sel, axis=0, keepdims=True)

    meta_i = jnp.zeros((tm, LANES), I32)
    meta_w = jnp.zeros((tm, LANES), F32)
    for k in range(TOP_K):
        rk = jnp.sum(onehots[k] * rank, axis=-1, keepdims=True).astype(I32)
        meta_i = jnp.where(lane == k, idxs[k], meta_i)
        meta_i = jnp.where(lane == TOP_K + k, rk, meta_i)
        meta_w = jnp.where(lane == k, es[k] / denom, meta_w)
    meta_i_ref[...] = meta_i
    meta_w_ref[...] = meta_w
    cnt_ref[...] = run_ref[...]


def _router(h, w_router_pad, b_router_pad, n_exp):
    t, d = h.shape
    tm = _tile(t, 256)
    kern = functools.partial(_router_kernel, tm=tm, n_exp=n_exp)
    return pl.pallas_call(
        kern,
        grid=(t // tm,),
        in_specs=[pl.BlockSpec((tm, d), lambda i: (i, 0)),
                  pl.BlockSpec((d, LANES), lambda i: (0, 0)),
                  pl.BlockSpec((1, LANES), lambda i: (0, 0))],
        out_specs=[pl.BlockSpec((tm, LANES), lambda i: (i, 0)),
                   pl.BlockSpec((tm, LANES), lambda i: (i, 0)),
                   pl.BlockSpec((1, LANES), lambda i: (0, 0))],
        out_shape=[jax.ShapeDtypeStruct((t, LANES), I32),
                   jax.ShapeDtypeStruct((t, LANES), F32),
                   jax.ShapeDtypeStruct((1, LANES), F32)],
        scratch_shapes=[pltpu.VMEM((1, LANES), F32)],
        compiler_params=_params(1),
        name="router",
    )(h, w_router_pad, b_router_pad)


def _ffn_kernel(te_ref, nu_ref, tok_ref, h_ref, wgu_hbm, bgu_ref, wdn_hbm, bdn_ref, o_ref,
                xbuf, wgu_buf, wdn_buf, pick_ref, sem, wsem, *, tm, nseg, kc, nc, layer):
    i = pl.program_id(0)
    n_used = nu_ref[0]
    d, ff = wgu_buf.shape[0], wdn_buf.shape[0]
    pitch = _row_pitch(nseg)
    n_kc, n_nc = d // kc, d // nc

    def row_copy(tile, slot, r):
        tok = tok_ref[tile * tm + r]
        return pltpu.make_async_copy(h_ref.at[pl.ds(tok * pitch, nseg), :],
                                     xbuf.at[slot, pl.ds(r * pitch, nseg), :], sem.at[slot])

    def wait_rows(slot):
        done = xbuf.at[slot, pl.ds(0, tm * nseg), :]
        pltpu.make_async_copy(done, done, sem.at[slot]).wait()

    def gu_copy(e, c):
        rows = pl.ds(c * kc, kc)
        return pltpu.make_async_copy(wgu_hbm.at[layer, e, rows, :], wgu_buf.at[rows, :], wsem.at[c])

    def dn_copy(e, c):
        cols = pl.ds(c * nc, nc)
        return pltpu.make_async_copy(wdn_hbm.at[layer, e, :, cols], wdn_buf.at[:, cols],
                                     wsem.at[n_kc + c])

    active = i < n_used
    new_expert = (i == 0) | (te_ref[i] != te_ref[jnp.maximum(i - 1, 0)])

    @pl.when(active & new_expert)
    def _():
        for c in range(n_kc):
            gu_copy(te_ref[i], c).start()
        for c in range(n_nc):
            dn_copy(te_ref[i], c).start()

    @pl.when(i == 0)
    def _():
        r = lax.broadcasted_iota(I32, (2 * ff, ff), 0)
        c = lax.broadcasted_iota(I32, (2 * ff, ff), 1)
        pick_ref[...] = jnp.where(r == 2 * c, 1.0, 0.0).astype(BF16)
        lax.fori_loop(0, tm, lambda r, c: (row_copy(0, 0, r).start(), c)[1], 0, unroll=4)

    def compute(arriving):
        slot = i % 2
        wait_rows(slot)
        nxt = jnp.minimum(i + 1, n_used - 1)
        per_chunk = tm // n_kc
        parts = _load_token_rows(xbuf, tm, nseg, lead=(slot,))
        per = kc // LANES
        gu = bgu_ref[...]
        for c in range(n_kc):
            if arriving is not None:
                gu_copy(arriving, c).wait()
            xc = jnp.concatenate(parts[c * per:(c + 1) * per], axis=1).astype(BF16)
            gu = gu + jnp.dot(xc, wgu_buf[c * kc:(c + 1) * kc, :].astype(BF16),
                              preferred_element_type=F32)
            for r in range(c * per_chunk, (c + 1) * per_chunk):
                row_copy(nxt, 1 - slot, r).start()
        lin = pltpu.roll(gu, 2 * ff - 1, 1)
        glu = jnp.minimum(gu, SWIGLU_LIMIT)
        lin = jnp.clip(lin, -SWIGLU_LIMIT, SWIGLU_LIMIT)
        act = glu * jax.nn.sigmoid(SWIGLU_ALPHA * glu) * (lin + 1.0)
        act = jnp.dot(act.astype(BF16), pick_ref[...], preferred_element_type=F32).astype(BF16)
        for c in range(n_nc):
            if arriving is not None:
                dn_copy(arriving, c).wait()
            cols = slice(c * nc, (c + 1) * nc)
            o_ref[:, cols] = (jnp.dot(act, wdn_buf[:, cols].astype(BF16), preferred_element_type=F32)
                              + bdn_ref[:, cols])

    pl.when(active & new_expert)(lambda: compute(te_ref[i]))
    pl.when(active & jnp.logical_not(new_expert))(lambda: compute(None))

    @pl.when(i == n_used - 1)
    def _():
        wait_rows((i + 1) % 2)

    @pl.when(i >= n_used)
    def _():
        o_ref[...] = jnp.zeros_like(o_ref)


def _expert_ffn(h_rows, tile_expert, n_used, token_of_row, w_gu, b_gu, w_dn, b_dn, layer, tm):
    _, n_exp, d, ff2 = w_gu.shape
    ff = ff2 // 2
    nseg = d // LANES
    n_tiles = token_of_row.shape[0] // tm
    kc, nc = _tile(d, 512), _tile(d, 1024)

    def wsel(i, te, nu, tok):
        return (layer, te[i], 0, 0)

    kern = functools.partial(_ffn_kernel, tm=tm, nseg=nseg, kc=kc, nc=nc, layer=layer)
    return pl.pallas_call(
        kern,
        grid_spec=pltpu.PrefetchScalarGridSpec(
            num_scalar_prefetch=3,
            grid=(n_tiles,),
            in_specs=[pl.BlockSpec(memory_space=pl.ANY),
                      pl.BlockSpec(memory_space=pl.ANY),
                      pl.BlockSpec((None, None, 1, ff2), wsel),
                      pl.BlockSpec(memory_space=pl.ANY),
                      pl.BlockSpec((None, None, 1, d), wsel)],
            out_specs=pl.BlockSpec((tm, d), lambda i, te, nu, tok: (i, 0)),
            scratch_shapes=[pltpu.VMEM((2, tm * _row_pitch(nseg), LANES), F32),
                            pltpu.VMEM((d, ff2), F32),
                            pltpu.VMEM((ff, d), F32),
                            pltpu.VMEM((ff2, ff), BF16),
                            pltpu.SemaphoreType.DMA((2,)),
                            pltpu.SemaphoreType.DMA((d // kc + d // nc,))]),
        out_shape=jax.ShapeDtypeStruct((n_tiles * tm, d), F32),
        compiler_params=_params(1),
        name="expert_ffn",
    )(tile_expert, n_used, token_of_row, h_rows, w_gu, b_gu, w_dn, b_dn)


def _combine_kernel(pos_ref, ys_ref, x_ref, g_ref, w_ref, o_ref, buf, sem, *, tm):
    i = pl.program_id(0)
    last = pl.num_programs(0) - 1

    def row_copy(tile, slot, t, k):
        row = pos_ref[(tile * tm + t) * TOP_K + k]
        return pltpu.make_async_copy(ys_ref.at[pl.ds(row, 1), :], buf.at[slot, k, pl.ds(t, 1), :],
                                     sem.at[slot])

    def wait_rows(slot):
        pltpu.make_async_copy(buf.at[slot], buf.at[slot], sem.at[slot]).wait()

    @pl.when(i == 0)
    def _():
        def body(t, c):
            for k in range(TOP_K):
                row_copy(0, 0, t, k).start()
            return c
        lax.fori_loop(0, tm, body, 0, unroll=2)

    slot = i % 2
    nxt = jnp.minimum(i + 1, last)
    wait_rows(slot)
    for t in range(tm):
        for k in range(TOP_K):
            row_copy(nxt, 1 - slot, t, k).start()
    w = w_ref[...]
    y = w[:, 0:1] * buf[slot, 0]
    for k in range(1, TOP_K):
        y = y + w[:, k:k + 1] * buf[slot, k]
    o_ref[...] = x_ref[...] + g_ref[...] * y

    @pl.when(i == last)
    def _():
        wait_rows(1 - slot)


def _combine(ys, pos_flat, x, m4, gate_idx, meta_w, seq, tm):
    t, d = x.shape
    per_b = seq // tm
    return pl.pallas_call(
        functools.partial(_combine_kernel, tm=tm),
        grid_spec=pltpu.PrefetchScalarGridSpec(
            num_scalar_prefetch=1,
            grid=(t // tm,),
            in_specs=[pl.BlockSpec(memory_space=pl.ANY),
                      pl.BlockSpec((tm, d), lambda i, p: (i, 0)),
                      pl.BlockSpec((None, None, 1, d), lambda i, p: (i // per_b, gate_idx, 0, 0)),
                      pl.BlockSpec((tm, LANES), lambda i, p: (i, 0))],
            out_specs=pl.BlockSpec((tm, d), lambda i, p: (i, 0)),
            scratch_shapes=[pltpu.VMEM((2, TOP_K, tm, d), F32),
                            pltpu.SemaphoreType.DMA((2,))]),
        out_shape=jax.ShapeDtypeStruct((t, d), F32),
        compiler_params=_params(1),
        name="moe_combine",
    )(pos_flat, ys, x, m4, meta_w)


def _moe(h_bf, h_rows, x, m4, w_router, b_router, w_gu, b_gu, w_dn, b_dn, layer, seq):
    t, d = x.shape
    n_exp = w_router.shape[1]
    tm = _tile(t, 256)
    wr = jnp.pad(w_router, ((0, 0), (0, LANES - n_exp)))
    br = jnp.pad(b_router, (0, LANES - n_exp))[None, :]
    meta_i, meta_w, counts = _router(h_bf, wr, br, n_exp)

    counts = counts[0, :n_exp].astype(I32)
    padded = ((counts + tm - 1) // tm) * tm
    ends = jnp.cumsum(padded)
    starts = ends - padded
    n_rows = t * TOP_K + n_exp * tm
    n_tiles = n_rows // tm
    experts = meta_i[:, :TOP_K]
    pos = starts[experts] + meta_i[:, TOP_K:2 * TOP_K]
    tile_start = jnp.arange(n_tiles, dtype=I32) * tm
    tile_expert = jnp.minimum(jnp.sum((ends[None, :] <= tile_start[:, None]).astype(I32), axis=1),
                              n_exp - 1)
    n_used = (ends[-1] // tm).astype(I32)[None]
    token_of_row = jnp.zeros((n_rows,), I32).at[pos.reshape(-1)].set(
        jnp.repeat(jnp.arange(t, dtype=I32), TOP_K))

    ys = _expert_ffn(h_rows, tile_expert, n_used, token_of_row, w_gu, b_gu[:, :, None, :], w_dn,
                     b_dn[:, :, None, :], layer, tm)
    return _combine(ys, pos.reshape(-1), x, m4, 5, meta_w, seq, _tile(seq, 64))


def _col_spec(k, tn, off_blocks=0):
    return pl.BlockSpec((k, tn), lambda j, i: (0, j + off_blocks))


def _layer_col_spec(layer, k, tn, off_blocks=0):
    return pl.BlockSpec((None, k, tn), lambda j, i: (layer, 0, j + off_blocks))


def _layer(x, m4, tabs, lam_init, batch, seq, layer, p):
    t, d = x.shape
    bw = p["w_branch"].shape[2]
    heads = bw // HEAD_DIM
    tm = _tile(seq, 1024)
    tn = _tile(bw, 512)
    w_in_t = p["w_in_t"]
    zero = lambda n: jnp.zeros((1, n), F32)

    (h,) = _ada_norm(x, p["norm1_g"], m4, 1, 0, seq, (BF16,))

    tab8, tab16, tab_small = tabs
    tab_spec = lambda w: pl.BlockSpec((tm, w), lambda j, i: (i, 0))
    rope8 = functools.partial(_rope_epilogue, groups=((0, 8),) * (tn // LANES))
    rope16 = functools.partial(_rope_epilogue, groups=((0, 16),) * (tn // LANES))

    def in_proj(row0, n, name, **kw):
        spec = pl.BlockSpec((pl.Element(1), pl.Element(tn), pl.Element(d)),
                            lambda j, i: (layer, pl.multiple_of(row0 + j * tn, 8), 0))
        return _matmul(h, w_in_t, spec, d, n, zero(n), tm=tm, tn=tn, out_dtype=BF16,
                       w_rows_are_outputs=True, name=name, **kw)

    sm_scale = HEAD_DIM ** -0.5
    col_spec = pl.BlockSpec((1, tn), lambda j, i: (0, j))

    def col_factors(*pieces):
        return jnp.concatenate([jnp.full((1, n), v, F32) for v, n in pieces], axis=1)

    c0 = 6 * bw + heads
    qk_a = in_proj(0, 2 * bw, "proj_a_qk", epilogue=lambda acc, tab, f: rope8(acc, tab) * f,
                   extra=(tab8, col_factors(((HEAD_DIM // 2) ** -0.5 * LOG2E, bw), (1.0, bw))),
                   extra_specs=(tab_spec(3 * LANES), col_spec))
    mid = in_proj(2 * bw, 4 * bw, "proj_av_b", epilogue=lambda acc, f: acc * f,
                  extra=(col_factors((1.0, bw), (sm_scale * LOG2E, bw), (1.0, 2 * bw)),),
                  extra_specs=(col_spec,))
    qkv_c = in_proj(c0, 3 * bw, "proj_c", epilogue=lambda acc, f: acc * f,
                    extra=(col_factors((sm_scale, bw), (1.0, 2 * bw)),), extra_specs=(col_spec,))
    q_d = in_proj(c0 + 3 * bw, bw, "proj_d_q",
                  epilogue=lambda acc, tab: rope16(acc, tab) * (sm_scale * LOG2E),
                  extra=(tab16,), extra_specs=(tab_spec(3 * LANES),))
    s0 = c0 + 4 * bw
    n_iq = IDX_HEADS * IDX_DIM
    used = 2 * HEAD_DIM + n_iq + IDX_DIM + IDX_HEADS + heads
    ws = -(-used // LANES) * LANES
    w_small_t = jnp.concatenate(
        [w_in_t[layer, s0:s0 + 2 * HEAD_DIM + n_iq + IDX_DIM + IDX_HEADS],
         w_in_t[layer, 6 * bw:6 * bw + heads], jnp.zeros((ws - used, d), F32)], axis=0)
    ik_col = 2 * HEAD_DIM + n_iq
    iw_col = ik_col + IDX_DIM
    fl_col = iw_col + IDX_HEADS
    groups_small = ((0, 16), None) + ((1, 8),) * (n_iq // LANES) + ((2, 8),)
    tm_s = _tile(seq, 512)
    small = _matmul(h, w_small_t, pl.BlockSpec((ws, d), lambda j, i: (0, 0), pipeline_mode=pl.Buffered(1)),
                    d, ws, zero(ws), tm=tm_s, tn=ws, out_dtype=F32, w_rows_are_outputs=True,
                    epilogue=functools.partial(_rope_epilogue, groups=groups_small), extra=(tab_small,),
                    extra_specs=(pl.BlockSpec((tm_s, 9 * LANES), lambda j, i: (i, 0)),), name="proj_small")

    fl_lane = fl_col % LANES
    bias_row = jnp.zeros((1, LANES), F32).at[0, fl_lane:fl_lane + heads].set(p["b_forget"])
    dcum = _forget_cumsum(small, bias_row, batch, seq, ik_col)[:, fl_lane:fl_lane + heads]
    dcum = dcum.reshape(batch, seq, heads).transpose(0, 2, 1)
    dq_col = dcum[..., None]
    dk_row = dcum[:, :, None, :]

    tq = tk = _tile(seq, 512)
    hb = bw // HEAD_DIM
    o_a = _attention(
        functools.partial(_attn_diff_kernel, tq=tq, tk=tk, seq=seq, lam_init=lam_init),
        qk_a, 0, qk_a, hb, mid, 0, heads, batch, seq,
        pre=(p["diff_lambda"], p["diff_norm_g"][None, :]),
        pre_specs=(pl.BlockSpec(p["diff_lambda"].shape, lambda b, h: (0, 0)),
                   pl.BlockSpec((1, HEAD_DIM), lambda b, h: (0, 0))),
        name="attn_diff")
    o_b = _attention(
        functools.partial(_attn_forget_kernel, tq=tq, tk=tk, seq=seq),
        mid, hb, mid, 2 * hb, mid, 3 * hb, heads, batch, seq,
        pre=(dq_col, dk_row),
        pre_specs=(pl.BlockSpec((None, None, seq, 1), lambda b, h: (b, h, 0, 0)),
                   pl.BlockSpec((None, None, 1, seq), lambda b, h: (b, h, 0, 0))),
        name="attn_forget")
    o_c = _attention(functools.partial(_attn_stick_kernel, tq=tq, tk=tk, seq=seq),
                     qkv_c, 0, qkv_c, hb, qkv_c, 2 * hb, heads, batch, seq, name="attn_stick")
    o_d = _attn_sparse(q_d, small, heads, batch, seq, _tile(seq, 256), min(TOPK_MAX, seq // 4),
                       2 * HEAD_DIM, ik_col, iw_col)

    n_br = p["w_gate"].shape[1]
    nd = d // tn
    gates = _matmul(h, p["w_gate"],
                    pl.BlockSpec((None, None, d, tn), lambda j, i: (layer, j // nd, 0, j % nd)),
                    d, n_br * d, p["b_gate"].reshape(1, n_br * d), tm=tm, tn=tn, out_dtype=F32,
                    epilogue=jax.nn.sigmoid, name="gates")
    merged = _merge((o_a, o_b, o_c, o_d), gates, p["w_branch"], layer, tm, tn)
    per_b = seq // tm
    x = _matmul(merged, p["w_out"], _layer_col_spec(layer, d, tn), d, d, zero(d), tm=tm, tn=tn,
                out_dtype=F32, epilogue=lambda acc, xr, g: xr + g * acc, extra=(x, m4),
                extra_specs=(pl.BlockSpec((tm, tn), lambda j, i: (i, j)),
                             pl.BlockSpec((None, None, 1, tn), lambda j, i: (i // per_b, 2, 0, j))),
                name="out_proj")

    h_bf, h_rows = _ada_norm(x, p["norm2_g"], m4, 4, 3, seq, (BF16, F32), token_rows=(False, True))
    return _moe(h_bf, h_rows, x, m4, p["w_router"], p["b_router"], p["w_gu"], p["b_gu"],
                p["w_dn"], p["b_dn"], layer, seq)


@jax.jit
def _forward(x, c, positions, w_mod, b_mod, ada_table, norm1_g, w_in, b_forget, diff_lambda,
             diff_norm_g, w_branch, w_gate, b_gate, w_out, norm2_g, w_router, b_router,
             w_gu, b_gu, w_dn, b_dn, final_norm_g):
    batch, seq, d = x.shape
    depth = w_in.shape[0]
    t = batch * seq
    xt = x.reshape(t, d)

    c8 = jnp.pad(c, ((0, 8 - batch), (0, 0)))
    n_mod = w_mod.shape[1]
    mod = _matmul(c8, w_mod, _col_spec(d, _tile(n_mod, 512)), d, n_mod, b_mod[None, :], tm=8,
                  tn=_tile(n_mod, 512), out_dtype=F32, prologue=lambda v: v * jax.nn.sigmoid(v),
                  name="mod_proj")[:batch]
    mod = mod.reshape(batch, N_MOD, d)

    pos = positions.reshape(t).astype(F32)
    tabs = (_rope_table(pos, HEAD_DIM // 2, LANES),
            _rope_table(pos, HEAD_DIM, LANES),
            jnp.concatenate([_rope_table(pos, HEAD_DIM, LANES), _rope_table(pos, IDX_DIM, LANES),
                             _rope_table(pos, IDX_DIM, IDX_DIM)], axis=1))

    w_in_t = jnp.swapaxes(w_in, 1, 2)
    for l in range(depth):
        m4 = (mod + ada_table[l][None])[:, :, None, :]
        lam_init = 0.8 - 0.6 * float(np.exp(-0.3 * l))
        p = dict(norm1_g=norm1_g[l][None, :], w_in_t=w_in_t, b_forget=b_forget[l],
                 diff_lambda=diff_lambda[l], diff_norm_g=diff_norm_g[l], w_branch=w_branch,
                 w_gate=w_gate, b_gate=b_gate[l], w_out=w_out, norm2_g=norm2_g[l][None, :],
                 w_router=w_router[l], b_router=b_router[l], w_gu=w_gu, b_gu=b_gu,
                 w_dn=w_dn, b_dn=b_dn)
        xt = _layer(xt, m4, tabs, lam_init, batch, seq, l, p)

    zeros4 = jnp.zeros((batch, 1, 1, d), F32)
    (out,) = _ada_norm(xt, final_norm_g[None, :], zeros4, 0, 0, seq, (F32,))
    return out.reshape(batch, seq, d)


def kernel(x, c, positions, w_mod, b_mod, ada_table, norm1_g, w_in, b_forget, diff_lambda,
           diff_norm_g, w_branch, w_gate, b_gate, w_out, norm2_g, w_router, b_router,
           w_gu, b_gu, w_dn, b_dn, final_norm_g):
    return _forward(x, c, positions, w_mod, b_mod, ada_table, norm1_g, w_in, b_forget, diff_lambda,
                    diff_norm_g, w_branch, w_gate, b_gate, w_out, norm2_g, w_router, b_router,
                    w_gu, b_gu, w_dn, b_dn, final_norm_g)
```

```python
import functools

import numpy as np
import jax
import jax.numpy as jnp
from jax import lax
from jax.experimental import pallas as pl
from jax.experimental.pallas import tpu as pltpu

F32 = jnp.float32
BF16 = jnp.bfloat16
I32 = jnp.int32

LANES = 128
HEAD_DIM = 128
CHUNK = 64
IDX_HEADS = 8
IDX_DIM = 64
TOPK_MAX = 256
ROPE_THETA = 500000.0
ROPE_FRACTION = 4
TOP_K = 4
SWIGLU_LIMIT = 7.0
SWIGLU_ALPHA = 1.702
RMS_EPS = 1e-5
N_MOD = 6
NEG = -1e30
LOG2E = 1.4426950408889634
INT_MIN = -(2 ** 31)
ROW_PITCH_PAD = 8
VMEM_LIMIT = 56 * 1024 * 1024


def _params(n_axes, vmem=VMEM_LIMIT):
    return pltpu.CompilerParams(dimension_semantics=("arbitrary",) * n_axes,
                                vmem_limit_bytes=vmem)


def _tile(n, pref):
    t = min(n, pref)
    while n % t:
        t //= 2
    return t


def _row_pitch(nseg):
    return nseg + ROW_PITCH_PAD


def _store_token_rows(o_ref, val, tm):
    nseg = val.shape[1] // LANES
    pitch = _row_pitch(nseg)
    for s in range(nseg):
        o_ref[pl.ds(s, tm, stride=pitch), :] = val[:, s * LANES:(s + 1) * LANES]
    for s in range(nseg, pitch):
        o_ref[pl.ds(s, tm, stride=pitch), :] = jnp.zeros((tm, LANES), o_ref.dtype)


def _load_token_rows(ref, tm, nseg, lead=()):
    return [ref[(*lead, pl.ds(s, tm, stride=_row_pitch(nseg)), slice(None))] for s in range(nseg)]


def _norm_kernel(x_ref, g_ref, sc_ref, sh_ref, *o_refs, tm, token_rows):
    x = x_ref[...]
    var = jnp.mean(x * x, axis=-1, keepdims=True)
    y = x * lax.rsqrt(var + RMS_EPS) * g_ref[...]
    h = y * (1.0 + sc_ref[...]) + sh_ref[...]
    for o, rows in zip(o_refs, token_rows):
        if rows:
            _store_token_rows(o, h, tm)
        else:
            o[...] = h.astype(o.dtype)


def _ada_norm(x, g, m4, sc_idx, sh_idx, seq, out_dtypes, token_rows=None):
    t, d = x.shape
    tm = _tile(seq, 256)
    per_b = seq // tm
    pitch = _row_pitch(d // LANES)
    token_rows = token_rows or (False,) * len(out_dtypes)
    out_specs, out_shape = [], []
    for dt, rows in zip(out_dtypes, token_rows):
        if rows:
            out_specs.append(pl.BlockSpec((tm * pitch, LANES), lambda i: (i, 0)))
            out_shape.append(jax.ShapeDtypeStruct((t * pitch, LANES), F32))
        else:
            out_specs.append(pl.BlockSpec((tm, d), lambda i: (i, 0)))
            out_shape.append(jax.ShapeDtypeStruct((t, d), dt))
    return pl.pallas_call(
        functools.partial(_norm_kernel, tm=tm, token_rows=token_rows),
        grid=(t // tm,),
        in_specs=[
            pl.BlockSpec((tm, d), lambda i: (i, 0)),
            pl.BlockSpec((1, d), lambda i: (0, 0)),
            pl.BlockSpec((None, None, 1, d), lambda i: (i // per_b, sc_idx, 0, 0)),
            pl.BlockSpec((None, None, 1, d), lambda i: (i // per_b, sh_idx, 0, 0)),
        ],
        out_specs=out_specs,
        out_shape=out_shape,
        compiler_params=_params(1),
        name="ada_norm",
    )(x, g, m4, m4)


def _mm_kernel(*refs, n_extra, epilogue, prologue, cast_rows, w_rows_are_outputs):
    x_ref, w_ref, b_ref = refs[:3]
    extra = refs[3:3 + n_extra]
    o_ref = refs[3 + n_extra]
    wbf_ref = refs[4 + n_extra]
    if len(w_ref.shape) == 3:
        w_ref = w_ref.at[0]

    @pl.when(pl.program_id(1) == 0)
    def _():
        def body(r, c):
            rows = pl.ds(pl.multiple_of(r * cast_rows, cast_rows), cast_rows)
            wbf_ref[rows, :] = w_ref[rows, :].astype(BF16)
            return c

        lax.fori_loop(0, w_ref.shape[0] // cast_rows, body, 0)

    x = x_ref[...]
    if prologue is not None:
        x = prologue(x)
    x = x.astype(BF16)
    if w_rows_are_outputs:
        acc = _qk(x, wbf_ref[...])
    else:
        acc = jnp.dot(x, wbf_ref[...], preferred_element_type=F32)
    o_ref[...] = epilogue(acc + b_ref[...], *[e[...] for e in extra]).astype(o_ref.dtype)


def _matmul(x, w, w_spec, k, n, bias, *, tm, tn, out_dtype, epilogue=None, extra=(),
            extra_specs=(), prologue=None, w_rows_are_outputs=False, name="matmul"):
    m = x.shape[0]
    if epilogue is None:
        epilogue = lambda acc: acc
    w_block = (tn, k) if w_rows_are_outputs else (k, tn)
    kern = functools.partial(_mm_kernel, n_extra=len(extra), epilogue=epilogue, prologue=prologue,
                             cast_rows=_tile(w_block[0], 256), w_rows_are_outputs=w_rows_are_outputs)
    return pl.pallas_call(
        kern,
        grid=(n // tn, m // tm),
        in_specs=[pl.BlockSpec((tm, k), lambda j, i: (i, 0)),
                  w_spec,
                  pl.BlockSpec((1, tn), lambda j, i: (0, j)),
                  *extra_specs],
        out_specs=pl.BlockSpec((tm, tn), lambda j, i: (i, j)),
        out_shape=jax.ShapeDtypeStruct((m, n), out_dtype),
        scratch_shapes=[pltpu.VMEM(w_block, BF16)],
        compiler_params=_params(2),
        name=name,
    )(x, w, bias, *extra)


def _rope_epilogue(acc, tab, *, groups):
    outs = []
    for g, kind in enumerate(groups):
        xg = acc[:, g * LANES:(g + 1) * LANES]
        if kind is not None:
            slot, half = kind
            base = 3 * slot * LANES
            cos = tab[:, base:base + LANES]
            sin_lo = tab[:, base + LANES:base + 2 * LANES]
            sin_hi = tab[:, base + 2 * LANES:base + 3 * LANES]
            xg = (xg * cos + pltpu.roll(xg, half, 1) * sin_lo
                  + pltpu.roll(xg, LANES - half, 1) * sin_hi)
        outs.append(xg)
    return outs[0] if len(outs) == 1 else jnp.concatenate(outs, axis=1)


def _rope_table(pos, period, active):
    rot = period // ROPE_FRACTION
    half = rot // 2
    inv_freq = jnp.float32(ROPE_THETA) ** (-jnp.arange(half, dtype=F32) / half)
    lane = np.arange(LANES)
    r = lane % period
    on = lane < active
    first = on & (r < half)
    second = on & (r >= half) & (r < rot)
    fidx = np.where(first, r, np.where(second, r - half, 0))
    ang = pos[:, None] * inv_freq[fidx][None, :]
    cos, sin = jnp.cos(ang), jnp.sin(ang)
    c = jnp.where((first | second)[None, :], cos, 1.0)
    s_lo = jnp.where(second[None, :], sin, 0.0)
    s_hi = jnp.where(first[None, :], -sin, 0.0)
    return jnp.concatenate([c, s_lo, s_hi], axis=1)


def _qk(q, k):
    return lax.dot_general(q, k, (((1,), (1,)), ((), ())), preferred_element_type=F32)


def _softmax_tile(carry, s, v):
    m, l, acc = carry
    m_new = jnp.maximum(m, jnp.max(s, axis=-1, keepdims=True))
    alpha = jnp.exp2(m - m_new)
    p = jnp.exp2(s - m_new)
    l = alpha * l + jnp.sum(p, axis=-1, keepdims=True)
    acc = alpha * acc + jnp.dot(p.astype(BF16), v, preferred_element_type=F32)
    return m_new, l, acc


def _softmax_init(rows):
    return (jnp.full((rows, 1), NEG, F32), jnp.zeros((rows, 1), F32),
            jnp.zeros((rows, HEAD_DIM), F32))


def _rows(ref, j, t):
    return ref[pl.ds(pl.multiple_of(j * t, t), t), :]


def _key_chunks(i, tq, tk):
    full = [(s, min(tk, i * tq - s), False) for s in range(0, i * tq, tk)]
    return full + [(i * tq, tq, True)]


def _attn_diff_kernel(dl_ref, g_ref, q_ref, k_ref, v_ref, o_ref, *, tq, tk, seq, lam_init):
    half = HEAD_DIM // 2
    lane = lax.broadcasted_iota(I32, (tq, HEAD_DIM), 1)
    r = lax.broadcasted_iota(I32, (2 * tq, tq), 0)
    r = jnp.where(r >= tq, r - tq, r)
    c = lax.broadcasted_iota(I32, (2 * tq, tq), 1)
    vis = c < (r // CHUNK + 1) * CHUNK
    dl = dl_ref[...]
    lam = (jnp.exp(jnp.sum(dl[0:1] * dl[1:2], axis=-1, keepdims=True))
           - jnp.exp(jnp.sum(dl[2:3] * dl[3:4], axis=-1, keepdims=True)) + lam_init)

    for i in range(seq // tq):
        q = q_ref[i * tq:(i + 1) * tq, :]
        zero = jnp.zeros_like(q)
        qz = jnp.concatenate([jnp.where(lane < half, q, zero), jnp.where(lane >= half, q, zero)], axis=0)
        carry = _softmax_init(2 * tq)
        for k0, kw, diag in _key_chunks(i, tq, tk):
            s = _qk(qz, k_ref[k0:k0 + kw, :])
            if diag:
                s = jnp.where(vis, s, NEG)
            carry = _softmax_tile(carry, s, v_ref[k0:k0 + kw, :])
        _, l, acc = carry
        o = acc / l
        o = o[:tq] - lam * o[tq:]
        var = jnp.mean(o * o, axis=-1, keepdims=True)
        o_ref[i * tq:(i + 1) * tq, :] = (o * lax.rsqrt(var + RMS_EPS) * g_ref[...]
                                         * (1.0 - lam_init)).astype(o_ref.dtype)


def _attn_forget_kernel(dq_ref, dk_ref, q_ref, k_ref, v_ref, o_ref, *, tq, tk, seq):
    r = lax.broadcasted_iota(I32, (tq, tq), 0)
    c = lax.broadcasted_iota(I32, (tq, tq), 1)
    for i in range(seq // tq):
        q = q_ref[i * tq:(i + 1) * tq, :]
        dq2 = dq_ref[i * tq:(i + 1) * tq, :] * LOG2E
        carry = _softmax_init(tq)
        for k0, kw, diag in _key_chunks(i, tq, tk):
            dk2 = dk_ref[:, k0:k0 + kw] * LOG2E
            s = _qk(q, k_ref[k0:k0 + kw, :]) + dq2 - dk2
            if diag:
                s = jnp.where(c <= r, s, NEG)
            carry = _softmax_tile(carry, s, v_ref[k0:k0 + kw, :])
        _, l, acc = carry
        o_ref[i * tq:(i + 1) * tq, :] = (acc / l).astype(o_ref.dtype)


def _log_sigmoid(z):
    return jnp.minimum(z, 0.0) - jnp.log(1.0 + jnp.exp(-jnp.abs(z)))


def _attn_stick_kernel(q_ref, k_ref, v_ref, o_ref, *, tq, tk, seq):
    r = lax.broadcasted_iota(I32, (tq, tq), 0)
    c = lax.broadcasted_iota(I32, (tq, tq), 1)
    cb = min(tq, 256)
    later = jnp.where(lax.broadcasted_iota(I32, (cb, cb), 0) > lax.broadcasted_iota(I32, (cb, cb), 1),
                      1.0, 0.0).astype(BF16)

    def tile(q, k0, kw, carry, mask):
        suffix, acc = carry
        z = _qk(q, k_ref[k0:k0 + kw, :])
        log_beta = _log_sigmoid(z)
        log_keep = log_beta - z
        if mask is not None:
            log_keep = jnp.where(mask, log_keep, 0.0)
        parts = [None] * (kw // cb)
        for blk in reversed(range(kw // cb)):
            lk = log_keep[:, blk * cb:(blk + 1) * cb]
            hi = lk.astype(BF16)
            lo = (lk - hi.astype(F32)).astype(BF16)
            parts[blk] = (jnp.dot(hi, later, preferred_element_type=F32)
                          + jnp.dot(lo, later, preferred_element_type=F32) + suffix)
            suffix = suffix + jnp.sum(lk, axis=-1, keepdims=True)
        after = parts[0] if len(parts) == 1 else jnp.concatenate(parts, axis=1)
        w = jnp.exp(log_beta + after)
        if mask is not None:
            w = jnp.where(mask, w, 0.0)
        acc = acc + jnp.dot(w.astype(BF16), v_ref[k0:k0 + kw, :], preferred_element_type=F32)
        return suffix, acc

    for i in range(seq // tq):
        q = q_ref[i * tq:(i + 1) * tq, :]
        carry = (jnp.zeros((tq, 1), F32), jnp.zeros((tq, HEAD_DIM), F32))
        for k0, kw, diag in reversed(_key_chunks(i, tq, tk)):
            carry = tile(q, k0, kw, carry, (c < r) if diag else None)
        o_ref[i * tq:(i + 1) * tq, :] = carry[1].astype(o_ref.dtype)


def _attention(kernel, q_arr, q_col, k_arr, k_col, v_arr, v_col, heads, batch, seq,
               pre=(), pre_specs=(), name="attention"):
    t = batch * seq
    return pl.pallas_call(
        kernel,
        grid=(batch, heads),
        in_specs=[*pre_specs,
                  pl.BlockSpec((seq, HEAD_DIM), lambda b, h: (b, q_col + h)),
                  pl.BlockSpec((seq, HEAD_DIM), lambda b, h: (b, k_col + h)),
                  pl.BlockSpec((seq, HEAD_DIM), lambda b, h: (b, v_col + h))],
        out_specs=pl.BlockSpec((seq, HEAD_DIM), lambda b, h: (b, h)),
        out_shape=jax.ShapeDtypeStruct((t, heads * HEAD_DIM), BF16),
        compiler_params=_params(2),
        name=name,
    )(*pre, q_arr, k_arr, v_arr)


def _sortable(x):
    bits = pltpu.bitcast(x + 0.0, I32)
    return jnp.where(bits < 0, bits ^ jnp.int32(0x7FFFFFFF), bits)


def _attn_sparse_kernel(q_ref, sq_ref, kv_ref, ik_ref, o_ref, key_ref, thr_ref, *, tq, seq, heads,
                        topk, iq_col, iw_col):
    i = pl.program_id(1)
    n_tiles = seq // tq
    q0 = i * tq

    sq = sq_ref[...]
    lane_k = lax.broadcasted_iota(I32, (tq, LANES), 1)
    iq_groups = [sq[:, iq_col + g * LANES: iq_col + (g + 1) * LANES].astype(BF16)
                 for g in range(IDX_HEADS * IDX_DIM // LANES)]
    iw = [jnp.broadcast_to(sq[:, iw_col + h: iw_col + h + 1], (tq, tq)) for h in range(IDX_HEADS)]
    row = lax.broadcasted_iota(I32, (tq, tq), 0)
    col = lax.broadcasted_iota(I32, (tq, tq), 1)
    chunk_end = ((q0 + row) // CHUNK + 1) * CHUNK

    key_ref[...] = jnp.full((tq, seq), INT_MIN, I32)

    def score_tile(j, c):
        ikt = _rows(ik_ref, j, tq)
        ik_lo = jnp.where(lane_k < IDX_DIM, ikt, 0.0)
        ik_hi = pltpu.roll(ik_lo, IDX_DIM, 1)
        ik_lo = ik_lo.astype(BF16)
        ik_hi = ik_hi.astype(BF16)
        score = jnp.zeros((tq, tq), F32)
        for h in range(IDX_HEADS):
            s = _qk(iq_groups[h // 2], ik_hi if h % 2 else ik_lo)
            score = score + iw[h] * jnp.maximum(s, 0.0)
        vis = (j * tq + col) < chunk_end
        key_ref[:, pl.ds(pl.multiple_of(j * tq, tq), tq)] = jnp.where(vis, _sortable(score), INT_MIN)
        return c

    lax.fori_loop(0, i + 1, score_tile, 0)

    def count_ge(cand, ncols=seq):
        return jnp.sum(jnp.where(key_ref[:, :ncols] >= cand, 1.0, 0.0), axis=-1, keepdims=True)

    kf = jnp.float32(topk)

    def find_threshold(ncols):
        thr = jnp.where(count_ge(jnp.zeros((tq, 1), I32), ncols) >= kf, 0, INT_MIN).astype(I32)

        def bisect(b, thr):
            cand = thr | lax.shift_left(jnp.int32(1), 30 - b)
            return jnp.where(count_ge(cand, ncols) >= kf, cand, thr)

        thr = lax.fori_loop(0, jnp.where((i + 1) * tq <= topk, 0, 31), bisect, thr)
        thr_ref[...] = jnp.maximum(thr, INT_MIN + 1)

    lo = 0
    for hi in sorted({min(n_tiles, v) for v in range(2, n_tiles + 2, 2)}):
        pl.when((i + 1 > lo) & (i + 1 <= hi))(functools.partial(find_threshold, hi * tq))
        lo = hi
    thr = thr_ref[...]
    n_ge = count_ge(thr)

    @pl.when(jnp.max(n_ge) > kf)
    def _():
        need = kf - jnp.sum(jnp.where(key_ref[...] > thr, 1.0, 0.0), axis=-1, keepdims=True)
        upto = jnp.where(row <= col, 1.0, 0.0).astype(BF16)

        def fix(j, seen):
            cols = pl.ds(pl.multiple_of(j * tq, tq), tq)
            kt = key_ref[:, cols]
            eq = kt == thr
            rank = jnp.dot(jnp.where(eq, 1.0, 0.0).astype(BF16), upto, preferred_element_type=F32) + seen
            key_ref[:, cols] = jnp.where(eq & (rank > need), INT_MIN, kt)
            return seen + jnp.sum(jnp.where(eq, 1.0, 0.0), axis=-1, keepdims=True)

        lax.fori_loop(0, n_tiles, fix, jnp.zeros((tq, 1), F32))

    qs = jnp.concatenate([q_ref[:, h * HEAD_DIM:(h + 1) * HEAD_DIM] for h in range(heads)], axis=0)

    def attn_tile(j, carry):
        kvt = _rows(kv_ref, j, tq)
        kt = kvt[:, :HEAD_DIM].astype(BF16)
        vt = kvt[:, HEAD_DIM:].astype(BF16)
        bias = jnp.where(key_ref[:, pl.ds(pl.multiple_of(j * tq, tq), tq)] >= thr, 0.0, NEG)
        s = _qk(qs, kt).reshape(heads, tq, tq) + bias[None]
        return _softmax_tile(carry, s.reshape(heads * tq, tq), vt)

    _, l, acc = lax.fori_loop(0, i + 1, attn_tile, _softmax_init(heads * tq))
    o = acc / l
    for h in range(heads):
        o_ref[:, h * HEAD_DIM:(h + 1) * HEAD_DIM] = o[h * tq:(h + 1) * tq].astype(o_ref.dtype)


def _attn_sparse(dq, small, heads, batch, seq, tq, topk, iq_col, ik_col, iw_col):
    nq = seq // tq
    t = batch * seq
    bw = heads * HEAD_DIM
    w = small.shape[1]
    kern = functools.partial(_attn_sparse_kernel, tq=tq, seq=seq, heads=heads, topk=topk,
                             iq_col=iq_col, iw_col=iw_col)
    return pl.pallas_call(
        kern,
        grid=(batch, nq),
        in_specs=[pl.BlockSpec((tq, bw), lambda b, i: (b * nq + i, 0)),
                  pl.BlockSpec((tq, w), lambda b, i: (b * nq + i, 0)),
                  pl.BlockSpec((seq, 2 * HEAD_DIM), lambda b, i: (b, 0)),
                  pl.BlockSpec((seq, LANES), lambda b, i: (b, ik_col // LANES))],
        out_specs=pl.BlockSpec((tq, bw), lambda b, i: (b * nq + i, 0)),
        out_shape=jax.ShapeDtypeStruct((t, bw), BF16),
        scratch_shapes=[pltpu.VMEM((tq, seq), I32), pltpu.VMEM((tq, 1), I32)],
        compiler_params=_params(2),
        name="attn_sparse",
    )(dq, small, small, small)


def _forget_cumsum_kernel(x_ref, b_ref, o_ref, *, seq, blk):
    r = lax.broadcasted_iota(I32, (blk, blk), 0)
    c = lax.broadcasted_iota(I32, (blk, blk), 1)
    tri = jnp.where(c <= r, 1.0, 0.0).astype(BF16)
    carry = jnp.zeros((1, LANES), F32)
    for s in range(seq // blk):
        logf = _log_sigmoid(x_ref[s * blk:(s + 1) * blk, :] + b_ref[...])
        hi = logf.astype(BF16)
        rem = logf - hi.astype(F32)
        mid = rem.astype(BF16)
        lo = (rem - mid.astype(F32)).astype(BF16)
        local = (jnp.dot(tri, hi, preferred_element_type=F32)
                 + jnp.dot(tri, mid, preferred_element_type=F32)
                 + jnp.dot(tri, lo, preferred_element_type=F32)) + carry
        o_ref[s * blk:(s + 1) * blk, :] = local
        carry = local[blk - 1:blk, :]


def _forget_cumsum(small, bias_row, batch, seq, col):
    return pl.pallas_call(
        functools.partial(_forget_cumsum_kernel, seq=seq, blk=_tile(seq, 256)),
        grid=(batch,),
        in_specs=[pl.BlockSpec((seq, LANES), lambda b: (b, col // LANES)),
                  pl.BlockSpec((1, LANES), lambda b: (0, 0))],
        out_specs=pl.BlockSpec((seq, LANES), lambda b: (b, 0)),
        out_shape=jax.ShapeDtypeStruct((batch * seq, LANES), F32),
        compiler_params=_params(1),
        name="forget_cumsum",
    )(small, bias_row)


def _merge_kernel(*refs, n_br, cast_rows):
    o_refs = refs[:n_br]
    g_refs = refs[n_br:2 * n_br]
    w_ref = refs[2 * n_br]
    out_ref = refs[2 * n_br + 1]
    wbf_ref = refs[2 * n_br + 2]

    @pl.when(pl.program_id(1) == 0)
    def _():
        for b in range(n_br):
            def body(r, c, b=b):
                rows = pl.ds(pl.multiple_of(r * cast_rows, cast_rows), cast_rows)
                wbf_ref[b, rows, :] = w_ref[b, rows, :].astype(BF16)
                return c
            lax.fori_loop(0, w_ref.shape[1] // cast_rows, body, 0)

    acc = None
    for b in range(n_br):
        term = g_refs[b][...] * jnp.dot(o_refs[b][...], wbf_ref[b], preferred_element_type=F32)
        acc = term if acc is None else acc + term
    out_ref[...] = acc.astype(out_ref.dtype)


def _merge(outs, gates, w_branch, layer, tm, tn):
    _, n_br, bw, d = w_branch.shape
    t = outs[0].shape[0]
    nd = d // tn
    kern = functools.partial(_merge_kernel, n_br=n_br, cast_rows=_tile(bw, 256))
    return pl.pallas_call(
        kern,
        grid=(nd, t // tm),
        in_specs=[*[pl.BlockSpec((tm, bw), lambda j, i: (i, 0)) for _ in range(n_br)],
                  *[pl.BlockSpec((tm, tn), lambda j, i, b=b: (i, b * nd + j)) for b in range(n_br)],
                  pl.BlockSpec((None, n_br, bw, tn), lambda j, i: (layer, 0, 0, j),
                               pipeline_mode=pl.Buffered(1))],
        out_specs=pl.BlockSpec((tm, tn), lambda j, i: (i, j)),
        out_shape=jax.ShapeDtypeStruct((t, d), BF16),
        scratch_shapes=[pltpu.VMEM((n_br, bw, tn), BF16)],
        compiler_params=_params(2),
        name="gated_merge",
    )(*outs, *([gates] * n_br), w_branch)


def _router_kernel(h_ref, w_ref, b_ref, meta_i_ref, meta_w_ref, cnt_ref, run_ref, *, tm, n_exp):
    @pl.when(pl.program_id(0) == 0)
    def _():
        run_ref[...] = jnp.zeros_like(run_ref)

    lane = lax.broadcasted_iota(I32, (tm, LANES), 1)
    lane_f = lane.astype(F32)
    logits = jnp.dot(h_ref[...], w_ref[...].astype(BF16), preferred_element_type=F32) + b_ref[...]
    logits = jnp.where(lane < n_exp, logits, -jnp.inf)

    vals, idxs = [], []
    work = logits
    for _ in range(TOP_K):
        v = jnp.max(work, axis=-1, keepdims=True)
        ix = jnp.min(jnp.where(work == v, lane_f, float(LANES)), axis=-1, keepdims=True).astype(I32)
        vals.append(v)
        idxs.append(ix)
        work = jnp.where(lane == ix, -jnp.inf, work)
    es = [jnp.exp(v - vals[0]) for v in vals]
    denom = es[0] + es[1] + es[2] + es[3]

    onehots = [jnp.where(lane == ix, 1.0, 0.0) for ix in idxs]
    sel = onehots[0] + onehots[1] + onehots[2] + onehots[3]
    r = lax.broadcasted_iota(I32, (tm, tm), 0)
    c = lax.broadcasted_iota(I32, (tm, tm), 1)
    before = jnp.where(c < r, 1.0, 0.0).astype(BF16)
    rank = jnp.dot(before, sel.astype(BF16), preferred_element_type=F32) + run_ref[...]
    run_ref[...] = run_ref[...] + jnp.sum(sel, axis=0, keepdims=True)

    meta_i = jnp.zeros((tm, LANES), I32)
    meta_w = jnp.zeros((tm, LANES), F32)
    for k in range(TOP_K):
        rk = jnp.sum(onehots[k] * rank, axis=-1, keepdims=True).astype(I32)
        meta_i = jnp.where(lane == k, idxs[k], meta_i)
        meta_i = jnp.where(lane == TOP_K + k, rk, meta_i)
        meta_w = jnp.where(lane == k, es[k] / denom, meta_w)
    meta_i_ref[...] = meta_i
    meta_w_ref[...] = meta_w
    cnt_ref[...] = run_ref[...]


def _router(h, w_router_pad, b_router_pad, n_exp):
    t, d = h.shape
    tm = _tile(t, 256)
    kern = functools.partial(_router_kernel, tm=tm, n_exp=n_exp)
    return pl.pallas_call(
        kern,
        grid=(t // tm,),
        in_specs=[pl.BlockSpec((tm, d), lambda i: (i, 0)),
                  pl.BlockSpec((d, LANES), lambda i: (0, 0)),
                  pl.BlockSpec((1, LANES), lambda i: (0, 0))],
        out_specs=[pl.BlockSpec((tm, LANES), lambda i: (i, 0)),
                   pl.BlockSpec((tm, LANES), lambda i: (i, 0)),
                   pl.BlockSpec((1, LANES), lambda i: (0, 0))],
        out_shape=[jax.ShapeDtypeStruct((t, LANES), I32),
                   jax.ShapeDtypeStruct((t, LANES), F32),
                   jax.ShapeDtypeStruct((1, LANES), F32)],
        scratch_shapes=[pltpu.VMEM((1, LANES), F32)],
        compiler_params=_params(1),
        name="router",
    )(h, w_router_pad, b_router_pad)


def _ffn_kernel(te_ref, nu_ref, tok_ref, h_ref, wgu_hbm, bgu_ref, wdn_hbm, bdn_ref, o_ref,
                xbuf, wgu_buf, wdn_buf, pick_ref, sem, wsem, *, tm, nseg, kc, nc, layer):
    i = pl.program_id(0)
    n_used = nu_ref[0]
    d, ff = wgu_buf.shape[0], wdn_buf.shape[0]
    pitch = _row_pitch(nseg)
    n_kc, n_nc = d // kc, d // nc

    def row_copy(tile, slot, r):
        tok = tok_ref[tile * tm + r]
        return pltpu.make_async_copy(h_ref.at[pl.ds(tok * pitch, nseg), :],
                                     xbuf.at[slot, pl.ds(r * pitch, nseg), :], sem.at[slot])

    def fetch(tile, slot):
        def body(r, c):
            row_copy(tile, slot, r).start()
            return c
        lax.fori_loop(0, tm, body, 0, unroll=4)

    def wait_rows(slot):
        done = xbuf.at[slot, pl.ds(0, tm * nseg), :]
        pltpu.make_async_copy(done, done, sem.at[slot]).wait()

    def gu_copy(e, c):
        rows = pl.ds(c * kc, kc)
        return pltpu.make_async_copy(wgu_hbm.at[layer, e, rows, :], wgu_buf.at[rows, :], wsem.at[c])

    def dn_copy(e, c):
        cols = pl.ds(c * nc, nc)
        return pltpu.make_async_copy(wdn_hbm.at[layer, e, :, cols], wdn_buf.at[:, cols],
                                     wsem.at[n_kc + c])

    active = i < n_used
    new_expert = (i == 0) | (te_ref[i] != te_ref[jnp.maximum(i - 1, 0)])

    @pl.when(active & new_expert)
    def _():
        for c in range(n_kc):
            gu_copy(te_ref[i], c).start()
        for c in range(n_nc):
            dn_copy(te_ref[i], c).start()

    @pl.when(i == 0)
    def _():
        r = lax.broadcasted_iota(I32, (2 * ff, ff), 0)
        c = lax.broadcasted_iota(I32, (2 * ff, ff), 1)
        pick_ref[...] = jnp.where(r == 2 * c, 1.0, 0.0).astype(BF16)
        fetch(0, 0)

    @pl.when(i + 1 < n_used)
    def _():
        fetch(i + 1, (i + 1) % 2)

    def compute(arriving):
        slot = i % 2
        wait_rows(slot)
        parts = _load_token_rows(xbuf, tm, nseg, lead=(slot,))
        per = kc // LANES
        gu = bgu_ref[...]
        for c in range(n_kc):
            if arriving is not None:
                gu_copy(arriving, c).wait()
            xc = jnp.concatenate(parts[c * per:(c + 1) * per], axis=1).astype(BF16)
            gu = gu + jnp.dot(xc, wgu_buf[c * kc:(c + 1) * kc, :].astype(BF16),
                              preferred_element_type=F32)
        lin = pltpu.roll(gu, 2 * ff - 1, 1)
        glu = jnp.minimum(gu, SWIGLU_LIMIT)
        lin = jnp.clip(lin, -SWIGLU_LIMIT, SWIGLU_LIMIT)
        act = glu * jax.nn.sigmoid(SWIGLU_ALPHA * glu) * (lin + 1.0)
        act = jnp.dot(act.astype(BF16), pick_ref[...], preferred_element_type=F32).astype(BF16)
        for c in range(n_nc):
            if arriving is not None:
                dn_copy(arriving, c).wait()
            cols = slice(c * nc, (c + 1) * nc)
            o_ref[:, cols] = (jnp.dot(act, wdn_buf[:, cols].astype(BF16), preferred_element_type=F32)
                              + bdn_ref[:, cols])

    pl.when(active & new_expert)(lambda: compute(te_ref[i]))
    pl.when(active & jnp.logical_not(new_expert))(lambda: compute(None))

    @pl.when(i >= n_used)
    def _():
        o_ref[...] = jnp.zeros_like(o_ref)


def _expert_ffn(h_rows, tile_expert, n_used, token_of_row, w_gu, b_gu, w_dn, b_dn, layer, tm):
    _, n_exp, d, ff2 = w_gu.shape
    ff = ff2 // 2
    nseg = d // LANES
    n_tiles = token_of_row.shape[0] // tm
    kc, nc = _tile(d, 512), _tile(d, 1024)

    def wsel(i, te, nu, tok):
        return (layer, te[i], 0, 0)

    kern = functools.partial(_ffn_kernel, tm=tm, nseg=nseg, kc=kc, nc=nc, layer=layer)
    return pl.pallas_call(
        kern,
        grid_spec=pltpu.PrefetchScalarGridSpec(
            num_scalar_prefetch=3,
            grid=(n_tiles,),
            in_specs=[pl.BlockSpec(memory_space=pl.ANY),
                      pl.BlockSpec(memory_space=pl.ANY),
                      pl.BlockSpec((None, None, 1, ff2), wsel),
                      pl.BlockSpec(memory_space=pl.ANY),
                      pl.BlockSpec((None, None, 1, d), wsel)],
            out_specs=pl.BlockSpec((tm, d), lambda i, te, nu, tok: (i, 0)),
            scratch_shapes=[pltpu.VMEM((2, tm * _row_pitch(nseg), LANES), F32),
                            pltpu.VMEM((d, ff2), F32),
                            pltpu.VMEM((ff, d), F32),
                            pltpu.VMEM((ff2, ff), BF16),
                            pltpu.SemaphoreType.DMA((2,)),
                            pltpu.SemaphoreType.DMA((d // kc + d // nc,))]),
        out_shape=jax.ShapeDtypeStruct((n_tiles * tm, d), F32),
        compiler_params=_params(1),
        name="expert_ffn",
    )(tile_expert, n_used, token_of_row, h_rows, w_gu, b_gu, w_dn, b_dn)


def _combine_kernel(pos_ref, ys_ref, x_ref, g_ref, w_ref, o_ref, buf, sem, *, tm):
    i = pl.program_id(0)
    last = pl.num_programs(0) - 1

    def row_copy(tile, slot, t, k):
        row = pos_ref[(tile * tm + t) * TOP_K + k]
        return pltpu.make_async_copy(ys_ref.at[pl.ds(row, 1), :], buf.at[slot, k, pl.ds(t, 1), :],
                                     sem.at[slot])

    def wait_rows(slot):
        pltpu.make_async_copy(buf.at[slot], buf.at[slot], sem.at[slot]).wait()

    def fetch(tile, slot):
        def body(t, c):
            for k in range(TOP_K):
                row_copy(tile, slot, t, k).start()
            return c
        lax.fori_loop(0, tm, body, 0, unroll=2)

    @pl.when(i == 0)
    def _():
        fetch(0, 0)

    @pl.when(i < last)
    def _():
        fetch(i + 1, (i + 1) % 2)

    slot = i % 2
    wait_rows(slot)
    w = w_ref[...]
    y = w[:, 0:1] * buf[slot, 0]
    for k in range(1, TOP_K):
        y = y + w[:, k:k + 1] * buf[slot, k]
    o_ref[...] = x_ref[...] + g_ref[...] * y


def _combine(ys, pos_flat, x, m4, gate_idx, meta_w, seq, tm):
    t, d = x.shape
    per_b = seq // tm
    return pl.pallas_call(
        functools.partial(_combine_kernel, tm=tm),
        grid_spec=pltpu.PrefetchScalarGridSpec(
            num_scalar_prefetch=1,
            grid=(t // tm,),
            in_specs=[pl.BlockSpec(memory_space=pl.ANY),
                      pl.BlockSpec((tm, d), lambda i, p: (i, 0)),
                      pl.BlockSpec((None, None, 1, d), lambda i, p: (i // per_b, gate_idx, 0, 0)),
                      pl.BlockSpec((tm, LANES), lambda i, p: (i, 0))],
            out_specs=pl.BlockSpec((tm, d), lambda i, p: (i, 0)),
            scratch_shapes=[pltpu.VMEM((2, TOP_K, tm, d), F32),
                            pltpu.SemaphoreType.DMA((2,))]),
        out_shape=jax.ShapeDtypeStruct((t, d), F32),
        compiler_params=_params(1),
        name="moe_combine",
    )(pos_flat, ys, x, m4, meta_w)


def _moe(h_bf, h_rows, x, m4, w_router, b_router, w_gu, b_gu, w_dn, b_dn, layer, seq):
    t, d = x.shape
    n_exp = w_router.shape[1]
    tm = _tile(t, 256)
    wr = jnp.pad(w_router, ((0, 0), (0, LANES - n_exp)))
    br = jnp.pad(b_router, (0, LANES - n_exp))[None, :]
    meta_i, meta_w, counts = _router(h_bf, wr, br, n_exp)

    counts = counts[0, :n_exp].astype(I32)
    padded = ((counts + tm - 1) // tm) * tm
    ends = jnp.cumsum(padded)
    starts = ends - padded
    n_rows = t * TOP_K + n_exp * tm
    n_tiles = n_rows // tm
    experts = meta_i[:, :TOP_K]
    pos = starts[experts] + meta_i[:, TOP_K:2 * TOP_K]
    tile_start = jnp.arange(n_tiles, dtype=I32) * tm
    tile_expert = jnp.minimum(jnp.sum((ends[None, :] <= tile_start[:, None]).astype(I32), axis=1),
                              n_exp - 1)
    n_used = (ends[-1] // tm).astype(I32)[None]
    token_of_row = jnp.zeros((n_rows,), I32).at[pos.reshape(-1)].set(
        jnp.repeat(jnp.arange(t, dtype=I32), TOP_K))

    ys = _expert_ffn(h_rows, tile_expert, n_used, token_of_row, w_gu, b_gu[:, :, None, :], w_dn,
                     b_dn[:, :, None, :], layer, tm)
    return _combine(ys, pos.reshape(-1), x, m4, 5, meta_w, seq, _tile(seq, 64))


def _col_spec(k, tn, off_blocks=0):
    return pl.BlockSpec((k, tn), lambda j, i: (0, j + off_blocks))


def _layer_col_spec(layer, k, tn, off_blocks=0):
    return pl.BlockSpec((None, k, tn), lambda j, i: (layer, 0, j + off_blocks))


def _layer(x, m4, tabs, lam_init, batch, seq, layer, p):
    t, d = x.shape
    bw = p["w_branch"].shape[2]
    heads = bw // HEAD_DIM
    tm = _tile(seq, 1024)
    tn = _tile(bw, 512)
    w_in_t = p["w_in_t"]
    zero = lambda n: jnp.zeros((1, n), F32)

    (h,) = _ada_norm(x, p["norm1_g"], m4, 1, 0, seq, (BF16,))

    tab8, tab16, tab_small = tabs
    tab_spec = lambda w: pl.BlockSpec((tm, w), lambda j, i: (i, 0))
    rope8 = functools.partial(_rope_epilogue, groups=((0, 8),) * (tn // LANES))
    rope16 = functools.partial(_rope_epilogue, groups=((0, 16),) * (tn // LANES))

    def in_proj(row0, n, name, **kw):
        spec = pl.BlockSpec((pl.Element(1), pl.Element(tn), pl.Element(d)),
                            lambda j, i: (layer, pl.multiple_of(row0 + j * tn, 8), 0))
        return _matmul(h, w_in_t, spec, d, n, zero(n), tm=tm, tn=tn, out_dtype=BF16,
                       w_rows_are_outputs=True, name=name, **kw)

    sm_scale = HEAD_DIM ** -0.5
    col_spec = pl.BlockSpec((1, tn), lambda j, i: (0, j))

    def col_factors(*pieces):
        return jnp.concatenate([jnp.full((1, n), v, F32) for v, n in pieces], axis=1)

    c0 = 6 * bw + heads
    qk_a = in_proj(0, 2 * bw, "proj_a_qk", epilogue=lambda acc, tab, f: rope8(acc, tab) * f,
                   extra=(tab8, col_factors(((HEAD_DIM // 2) ** -0.5 * LOG2E, bw), (1.0, bw))),
                   extra_specs=(tab_spec(3 * LANES), col_spec))
    mid = in_proj(2 * bw, 4 * bw, "proj_av_b", epilogue=lambda acc, f: acc * f,
                  extra=(col_factors((1.0, bw), (sm_scale * LOG2E, bw), (1.0, 2 * bw)),),
                  extra_specs=(col_spec,))
    qkv_c = in_proj(c0, 3 * bw, "proj_c", epilogue=lambda acc, f: acc * f,
                    extra=(col_factors((sm_scale, bw), (1.0, 2 * bw)),), extra_specs=(col_spec,))
    q_d = in_proj(c0 + 3 * bw, bw, "proj_d_q",
                  epilogue=lambda acc, tab: rope16(acc, tab) * (sm_scale * LOG2E),
                  extra=(tab16,), extra_specs=(tab_spec(3 * LANES),))
    s0 = c0 + 4 * bw
    n_iq = IDX_HEADS * IDX_DIM
    used = 2 * HEAD_DIM + n_iq + IDX_DIM + IDX_HEADS + heads
    ws = -(-used // LANES) * LANES
    w_small_t = jnp.concatenate(
        [w_in_t[layer, s0:s0 + 2 * HEAD_DIM + n_iq + IDX_DIM + IDX_HEADS],
         w_in_t[layer, 6 * bw:6 * bw + heads], jnp.zeros((ws - used, d), F32)], axis=0)
    ik_col = 2 * HEAD_DIM + n_iq
    iw_col = ik_col + IDX_DIM
    fl_col = iw_col + IDX_HEADS
    groups_small = ((0, 16), None) + ((1, 8),) * (n_iq // LANES) + ((2, 8),)
    tm_s = _tile(seq, 512)
    small = _matmul(h, w_small_t, pl.BlockSpec((ws, d), lambda j, i: (0, 0), pipeline_mode=pl.Buffered(1)),
                    d, ws, zero(ws), tm=tm_s, tn=ws, out_dtype=F32, w_rows_are_outputs=True,
                    epilogue=functools.partial(_rope_epilogue, groups=groups_small), extra=(tab_small,),
                    extra_specs=(pl.BlockSpec((tm_s, 9 * LANES), lambda j, i: (i, 0)),), name="proj_small")

    fl_lane = fl_col % LANES
    bias_row = jnp.zeros((1, LANES), F32).at[0, fl_lane:fl_lane + heads].set(p["b_forget"])
    dcum = _forget_cumsum(small, bias_row, batch, seq, ik_col)[:, fl_lane:fl_lane + heads]
    dcum = dcum.reshape(batch, seq, heads).transpose(0, 2, 1)
    dq_col = dcum[..., None]
    dk_row = dcum[:, :, None, :]

    tq = tk = _tile(seq, 512)
    hb = bw // HEAD_DIM
    o_a = _attention(
        functools.partial(_attn_diff_kernel, tq=tq, tk=tk, seq=seq, lam_init=lam_init),
        qk_a, 0, qk_a, hb, mid, 0, heads, batch, seq,
        pre=(p["diff_lambda"], p["diff_norm_g"][None, :]),
        pre_specs=(pl.BlockSpec(p["diff_lambda"].shape, lambda b, h: (0, 0)),
                   pl.BlockSpec((1, HEAD_DIM), lambda b, h: (0, 0))),
        name="attn_diff")
    o_b = _attention(
        functools.partial(_attn_forget_kernel, tq=tq, tk=tk, seq=seq),
        mid, hb, mid, 2 * hb, mid, 3 * hb, heads, batch, seq,
        pre=(dq_col, dk_row),
        pre_specs=(pl.BlockSpec((None, None, seq, 1), lambda b, h: (b, h, 0, 0)),
                   pl.BlockSpec((None, None, 1, seq), lambda b, h: (b, h, 0, 0))),
        name="attn_forget")
    o_c = _attention(functools.partial(_attn_stick_kernel, tq=tq, tk=tk, seq=seq),
                     qkv_c, 0, qkv_c, hb, qkv_c, 2 * hb, heads, batch, seq, name="attn_stick")
    o_d = _attn_sparse(q_d, small, heads, batch, seq, _tile(seq, 256), min(TOPK_MAX, seq // 4),
                       2 * HEAD_DIM, ik_col, iw_col)

    n_br = p["w_gate"].shape[1]
    nd = d // tn
    gates = _matmul(h, p["w_gate"],
                    pl.BlockSpec((None, None, d, tn), lambda j, i: (layer, j // nd, 0, j % nd)),
                    d, n_br * d, p["b_gate"].reshape(1, n_br * d), tm=tm, tn=tn, out_dtype=F32,
                    epilogue=jax.nn.sigmoid, name="gates")
    merged = _merge((o_a, o_b, o_c, o_d), gates, p["w_branch"], layer, tm, tn)
    per_b = seq // tm
    x = _matmul(merged, p["w_out"], _layer_col_spec(layer, d, tn), d, d, zero(d), tm=tm, tn=tn,
                out_dtype=F32, epilogue=lambda acc, xr, g: xr + g * acc, extra=(x, m4),
                extra_specs=(pl.BlockSpec((tm, tn), lambda j, i: (i, j)),
                             pl.BlockSpec((None, None, 1, tn), lambda j, i: (i // per_b, 2, 0, j))),
                name="out_proj")

    h_bf, h_rows = _ada_norm(x, p["norm2_g"], m4, 4, 3, seq, (BF16, F32), token_rows=(False, True))
    return _moe(h_bf, h_rows, x, m4, p["w_router"], p["b_router"], p["w_gu"], p["b_gu"],
                p["w_dn"], p["b_dn"], layer, seq)


@jax.jit
def _forward(x, c, positions, w_mod, b_mod, ada_table, norm1_g, w_in, b_forget, diff_lambda,
             diff_norm_g, w_branch, w_gate, b_gate, w_out, norm2_g, w_router, b_router,
             w_gu, b_gu, w_dn, b_dn, final_norm_g):
    batch, seq, d = x.shape
    depth = w_in.shape[0]
    t = batch * seq
    xt = x.reshape(t, d)

    c8 = jnp.pad(c, ((0, 8 - batch), (0, 0)))
    n_mod = w_mod.shape[1]
    mod = _matmul(c8, w_mod, _col_spec(d, _tile(n_mod, 512)), d, n_mod, b_mod[None, :], tm=8,
                  tn=_tile(n_mod, 512), out_dtype=F32, prologue=lambda v: v * jax.nn.sigmoid(v),
                  name="mod_proj")[:batch]
    mod = mod.reshape(batch, N_MOD, d)

    pos = positions.reshape(t).astype(F32)
    tabs = (_rope_table(pos, HEAD_DIM // 2, LANES),
            _rope_table(pos, HEAD_DIM, LANES),
            jnp.concatenate([_rope_table(pos, HEAD_DIM, LANES), _rope_table(pos, IDX_DIM, LANES),
                             _rope_table(pos, IDX_DIM, IDX_DIM)], axis=1))

    w_in_t = jnp.swapaxes(w_in, 1, 2)
    for l in range(depth):
        m4 = (mod + ada_table[l][None])[:, :, None, :]
        lam_init = 0.8 - 0.6 * float(np.exp(-0.3 * l))
        p = dict(norm1_g=norm1_g[l][None, :], w_in_t=w_in_t, b_forget=b_forget[l],
                 diff_lambda=diff_lambda[l], diff_norm_g=diff_norm_g[l], w_branch=w_branch,
                 w_gate=w_gate, b_gate=b_gate[l], w_out=w_out, norm2_g=norm2_g[l][None, :],
                 w_router=w_router[l], b_router=b_router[l], w_gu=w_gu, b_gu=b_gu,
                 w_dn=w_dn, b_dn=b_dn)
        xt = _layer(xt, m4, tabs, lam_init, batch, seq, l, p)

    zeros4 = jnp.zeros((batch, 1, 1, d), F32)
    (out,) = _ada_norm(xt, final_norm_g[None, :], zeros4, 0, 0, seq, (F32,))
    return out.reshape(batch, seq, d)


def kernel(x, c, positions, w_mod, b_mod, ada_table, norm1_g, w_in, b_forget, diff_lambda,
           diff_norm_g, w_branch, w_gate, b_gate, w_out, norm2_g, w_router, b_router,
           w_gu, b_gu, w_dn, b_dn, final_norm_g):
    return _forward(x, c, positions, w_mod, b_mod, ada_table, norm1_g, w_in, b_forget, diff_lambda,
                    diff_norm_g, w_branch, w_gate, b_gate, w_out, norm2_g, w_router, b_router,
                    w_gu, b_gu, w_dn, b_dn, final_norm_g)
```

```python
import functools

import numpy as np
import jax
import jax.numpy as jnp
from jax import lax
from jax.experimental import pallas as pl
from jax.experimental.pallas import tpu as pltpu

F32 = jnp.float32
BF16 = jnp.bfloat16
I32 = jnp.int32

LANES = 128
HEAD_DIM = 128
CHUNK = 64
IDX_HEADS = 8
IDX_DIM = 64
TOPK_MAX = 256
ROPE_THETA = 500000.0
ROPE_FRACTION = 4
TOP_K = 4
SWIGLU_LIMIT = 7.0
SWIGLU_ALPHA = 1.702
RMS_EPS = 1e-5
N_MOD = 6
NEG = -1e30
LOG2E = 1.4426950408889634
INT_MIN = -(2 ** 31)
ROW_PITCH_PAD = 8
VMEM_LIMIT = 56 * 1024 * 1024


def _params(n_axes, vmem=VMEM_LIMIT):
    return pltpu.CompilerParams(dimension_semantics=("arbitrary",) * n_axes,
                                vmem_limit_bytes=vmem)


def _tile(n, pref):
    t = min(n, pref)
    while n % t:
        t //= 2
    return t


def _row_pitch(nseg):
    return nseg + ROW_PITCH_PAD


def _store_token_rows(o_ref, val, tm):
    nseg = val.shape[1] // LANES
    pitch = _row_pitch(nseg)
    for s in range(nseg):
        o_ref[pl.ds(s, tm, stride=pitch), :] = val[:, s * LANES:(s + 1) * LANES]
    for s in range(nseg, pitch):
        o_ref[pl.ds(s, tm, stride=pitch), :] = jnp.zeros((tm, LANES), o_ref.dtype)


def _load_token_rows(ref, tm, nseg, lead=()):
    return [ref[(*lead, pl.ds(s, tm, stride=_row_pitch(nseg)), slice(None))] for s in range(nseg)]


def _norm_kernel(x_ref, g_ref, sc_ref, sh_ref, *o_refs, tm, token_rows):
    x = x_ref[...]
    var = jnp.mean(x * x, axis=-1, keepdims=True)
    y = x * lax.rsqrt(var + RMS_EPS) * g_ref[...]
    h = y * (1.0 + sc_ref[...]) + sh_ref[...]
    for o, rows in zip(o_refs, token_rows):
        if rows:
            _store_token_rows(o, h, tm)
        else:
            o[...] = h.astype(o.dtype)


def _ada_norm(x, g, m4, sc_idx, sh_idx, seq, out_dtypes, token_rows=None):
    t, d = x.shape
    tm = _tile(seq, 256)
    per_b = seq // tm
    pitch = _row_pitch(d // LANES)
    token_rows = token_rows or (False,) * len(out_dtypes)
    out_specs, out_shape = [], []
    for dt, rows in zip(out_dtypes, token_rows):
        if rows:
            out_specs.append(pl.BlockSpec((tm * pitch, LANES), lambda i: (i, 0)))
            out_shape.append(jax.ShapeDtypeStruct((t * pitch, LANES), F32))
        else:
            out_specs.append(pl.BlockSpec((tm, d), lambda i: (i, 0)))
            out_shape.append(jax.ShapeDtypeStruct((t, d), dt))
    return pl.pallas_call(
        functools.partial(_norm_kernel, tm=tm, token_rows=token_rows),
        grid=(t // tm,),
        in_specs=[
            pl.BlockSpec((tm, d), lambda i: (i, 0)),
            pl.BlockSpec((1, d), lambda i: (0, 0)),
            pl.BlockSpec((None, None, 1, d), lambda i: (i // per_b, sc_idx, 0, 0)),
            pl.BlockSpec((None, None, 1, d), lambda i: (i // per_b, sh_idx, 0, 0)),
        ],
        out_specs=out_specs,
        out_shape=out_shape,
        compiler_params=_params(1),
        name="ada_norm",
    )(x, g, m4, m4)


def _mm_kernel(*refs, n_extra, epilogue, prologue, cast_rows, w_rows_are_outputs):
    x_ref, w_ref, b_ref = refs[:3]
    extra = refs[3:3 + n_extra]
    o_ref = refs[3 + n_extra]
    wbf_ref = refs[4 + n_extra]
    if len(w_ref.shape) == 3:
        w_ref = w_ref.at[0]

    @pl.when(pl.program_id(1) == 0)
    def _():
        def body(r, c):
            rows = pl.ds(pl.multiple_of(r * cast_rows, cast_rows), cast_rows)
            wbf_ref[rows, :] = w_ref[rows, :].astype(BF16)
            return c

        lax.fori_loop(0, w_ref.shape[0] // cast_rows, body, 0)

    x = x_ref[...]
    if prologue is not None:
        x = prologue(x)
    x = x.astype(BF16)
    if w_rows_are_outputs:
        acc = _qk(x, wbf_ref[...])
    else:
        acc = jnp.dot(x, wbf_ref[...], preferred_element_type=F32)
    o_ref[...] = epilogue(acc + b_ref[...], *[e[...] for e in extra]).astype(o_ref.dtype)


def _matmul(x, w, w_spec, k, n, bias, *, tm, tn, out_dtype, epilogue=None, extra=(),
            extra_specs=(), prologue=None, w_rows_are_outputs=False, name="matmul"):
    m = x.shape[0]
    if epilogue is None:
        epilogue = lambda acc: acc
    w_block = (tn, k) if w_rows_are_outputs else (k, tn)
    kern = functools.partial(_mm_kernel, n_extra=len(extra), epilogue=epilogue, prologue=prologue,
                             cast_rows=_tile(w_block[0], 256), w_rows_are_outputs=w_rows_are_outputs)
    return pl.pallas_call(
        kern,
        grid=(n // tn, m // tm),
        in_specs=[pl.BlockSpec((tm, k), lambda j, i: (i, 0)),
                  w_spec,
                  pl.BlockSpec((1, tn), lambda j, i: (0, j)),
                  *extra_specs],
        out_specs=pl.BlockSpec((tm, tn), lambda j, i: (i, j)),
        out_shape=jax.ShapeDtypeStruct((m, n), out_dtype),
        scratch_shapes=[pltpu.VMEM(w_block, BF16)],
        compiler_params=_params(2),
        name=name,
    )(x, w, bias, *extra)


def _rope_epilogue(acc, tab, *, groups):
    outs = []
    for g, kind in enumerate(groups):
        xg = acc[:, g * LANES:(g + 1) * LANES]
        if kind is not None:
            slot, half = kind
            base = 3 * slot * LANES
            cos = tab[:, base:base + LANES]
            sin_lo = tab[:, base + LANES:base + 2 * LANES]
            sin_hi = tab[:, base + 2 * LANES:base + 3 * LANES]
            xg = (xg * cos + pltpu.roll(xg, half, 1) * sin_lo
                  + pltpu.roll(xg, LANES - half, 1) * sin_hi)
        outs.append(xg)
    return outs[0] if len(outs) == 1 else jnp.concatenate(outs, axis=1)


def _rope_table(pos, period, active):
    rot = period // ROPE_FRACTION
    half = rot // 2
    inv_freq = jnp.float32(ROPE_THETA) ** (-jnp.arange(half, dtype=F32) / half)
    ang = pos[:, None] * inv_freq[None, :]
    cos, sin = jnp.cos(ang), jnp.sin(ang)
    t = pos.shape[0]
    one = jnp.ones((t, period - rot), F32)
    zero_h = jnp.zeros((t, half), F32)
    zero_r = jnp.zeros((t, period - rot), F32)

    def lanes(block, idle):
        reps = [block] * (active // period) + [idle] * ((LANES - active) // period)
        return jnp.concatenate(reps, axis=1)

    c = lanes(jnp.concatenate([cos, cos, one], axis=1), jnp.ones((t, period), F32))
    s_lo = lanes(jnp.concatenate([zero_h, sin, zero_r], axis=1), jnp.zeros((t, period), F32))
    s_hi = lanes(jnp.concatenate([-sin, zero_h, zero_r], axis=1), jnp.zeros((t, period), F32))
    return jnp.concatenate([c, s_lo, s_hi], axis=1)


def _qk(q, k):
    return lax.dot_general(q, k, (((1,), (1,)), ((), ())), preferred_element_type=F32)


def _softmax_tile(carry, s, v):
    m, l, acc = carry
    m_new = jnp.maximum(m, jnp.max(s, axis=-1, keepdims=True))
    alpha = jnp.exp2(m - m_new)
    p = jnp.exp2(s - m_new)
    l = alpha * l + jnp.sum(p, axis=-1, keepdims=True)
    acc = alpha * acc + jnp.dot(p.astype(BF16), v, preferred_element_type=F32)
    return m_new, l, acc


def _softmax_init(rows):
    return (jnp.full((rows, 1), NEG, F32), jnp.zeros((rows, 1), F32),
            jnp.zeros((rows, HEAD_DIM), F32))


def _rows(ref, j, t):
    return ref[pl.ds(pl.multiple_of(j * t, t), t), :]


def _key_chunks(i, tq, tk):
    full = [(s, min(tk, i * tq - s), False) for s in range(0, i * tq, tk)]
    return full + [(i * tq, tq, True)]


def _attn_diff_kernel(dl_ref, g_ref, q_ref, k_ref, v_ref, o_ref, *, tq, tk, seq, lam_init):
    half = HEAD_DIM // 2
    lane = lax.broadcasted_iota(I32, (tq, HEAD_DIM), 1)
    r = lax.broadcasted_iota(I32, (2 * tq, tq), 0)
    r = jnp.where(r >= tq, r - tq, r)
    c = lax.broadcasted_iota(I32, (2 * tq, tq), 1)
    vis = c < (r // CHUNK + 1) * CHUNK
    dl = dl_ref[...]
    lam = (jnp.exp(jnp.sum(dl[0:1] * dl[1:2], axis=-1, keepdims=True))
           - jnp.exp(jnp.sum(dl[2:3] * dl[3:4], axis=-1, keepdims=True)) + lam_init)

    for i in range(seq // tq):
        q = q_ref[i * tq:(i + 1) * tq, :]
        zero = jnp.zeros_like(q)
        qz = jnp.concatenate([jnp.where(lane < half, q, zero), jnp.where(lane >= half, q, zero)], axis=0)
        carry = _softmax_init(2 * tq)
        for k0, kw, diag in _key_chunks(i, tq, tk):
            s = _qk(qz, k_ref[k0:k0 + kw, :])
            if diag:
                s = jnp.where(vis, s, NEG)
            carry = _softmax_tile(carry, s, v_ref[k0:k0 + kw, :])
        _, l, acc = carry
        o = acc / l
        o = o[:tq] - lam * o[tq:]
        var = jnp.mean(o * o, axis=-1, keepdims=True)
        o_ref[i * tq:(i + 1) * tq, :] = (o * lax.rsqrt(var + RMS_EPS) * g_ref[...]
                                         * (1.0 - lam_init)).astype(o_ref.dtype)


def _attn_forget_kernel(dq_ref, dk_ref, q_ref, k_ref, v_ref, o_ref, *, tq, tk, seq):
    r = lax.broadcasted_iota(I32, (tq, tq), 0)
    c = lax.broadcasted_iota(I32, (tq, tq), 1)
    for i in range(seq // tq):
        q = q_ref[i * tq:(i + 1) * tq, :]
        dq2 = dq_ref[i * tq:(i + 1) * tq, :] * LOG2E
        carry = _softmax_init(tq)
        for k0, kw, diag in _key_chunks(i, tq, tk):
            dk2 = dk_ref[:, k0:k0 + kw] * LOG2E
            s = _qk(q, k_ref[k0:k0 + kw, :]) + dq2 - dk2
            if diag:
                s = jnp.where(c <= r, s, NEG)
            carry = _softmax_tile(carry, s, v_ref[k0:k0 + kw, :])
        _, l, acc = carry
        o_ref[i * tq:(i + 1) * tq, :] = (acc / l).astype(o_ref.dtype)


def _log_sigmoid(z):
    return jnp.minimum(z, 0.0) - jnp.log(1.0 + jnp.exp(-jnp.abs(z)))


def _attn_stick_kernel(q_ref, k_ref, v_ref, o_ref, *, tq, tk, seq):
    r = lax.broadcasted_iota(I32, (tq, tq), 0)
    c = lax.broadcasted_iota(I32, (tq, tq), 1)
    cb = min(tq, 256)
    later = jnp.where(lax.broadcasted_iota(I32, (cb, cb), 0) > lax.broadcasted_iota(I32, (cb, cb), 1),
                      1.0, 0.0).astype(BF16)

    def tile(q, k0, kw, carry, mask):
        suffix, acc = carry
        z = _qk(q, k_ref[k0:k0 + kw, :])
        log_beta = _log_sigmoid(z)
        log_keep = log_beta - z
        if mask is not None:
            log_keep = jnp.where(mask, log_keep, 0.0)
        parts = [None] * (kw // cb)
        for blk in reversed(range(kw // cb)):
            lk = log_keep[:, blk * cb:(blk + 1) * cb]
            hi = lk.astype(BF16)
            lo = (lk - hi.astype(F32)).astype(BF16)
            parts[blk] = (jnp.dot(hi, later, preferred_element_type=F32)
                          + jnp.dot(lo, later, preferred_element_type=F32) + suffix)
            suffix = suffix + jnp.sum(lk, axis=-1, keepdims=True)
        after = parts[0] if len(parts) == 1 else jnp.concatenate(parts, axis=1)
        w = jnp.exp(log_beta + after)
        if mask is not None:
            w = jnp.where(mask, w, 0.0)
        acc = acc + jnp.dot(w.astype(BF16), v_ref[k0:k0 + kw, :], preferred_element_type=F32)
        return suffix, acc

    for i in range(seq // tq):
        q = q_ref[i * tq:(i + 1) * tq, :]
        carry = (jnp.zeros((tq, 1), F32), jnp.zeros((tq, HEAD_DIM), F32))
        for k0, kw, diag in reversed(_key_chunks(i, tq, tk)):
            carry = tile(q, k0, kw, carry, (c < r) if diag else None)
        o_ref[i * tq:(i + 1) * tq, :] = carry[1].astype(o_ref.dtype)


def _attention(kernel, q_arr, q_col, k_arr, k_col, v_arr, v_col, heads, batch, seq,
               pre=(), pre_specs=(), name="attention"):
    t = batch * seq
    return pl.pallas_call(
        kernel,
        grid=(batch, heads),
        in_specs=[*pre_specs,
                  pl.BlockSpec((seq, HEAD_DIM), lambda b, h: (b, q_col + h)),
                  pl.BlockSpec((seq, HEAD_DIM), lambda b, h: (b, k_col + h)),
                  pl.BlockSpec((seq, HEAD_DIM), lambda b, h: (b, v_col + h))],
        out_specs=pl.BlockSpec((seq, HEAD_DIM), lambda b, h: (b, h)),
        out_shape=jax.ShapeDtypeStruct((t, heads * HEAD_DIM), BF16),
        compiler_params=_params(2),
        name=name,
    )(*pre, q_arr, k_arr, v_arr)


def _sortable(x):
    bits = pltpu.bitcast(x + 0.0, I32)
    return jnp.where(bits < 0, bits ^ jnp.int32(0x7FFFFFFF), bits)


def _attn_sparse_kernel(q_ref, sq_ref, kv_ref, ik_ref, o_ref, key_ref, thr_ref, *, tq, seq, heads,
                        topk, iq_col, iw_col):
    i = pl.program_id(1)
    n_tiles = seq // tq
    q0 = i * tq

    sq = sq_ref[...]
    lane_k = lax.broadcasted_iota(I32, (tq, LANES), 1)
    iq_groups = [sq[:, iq_col + g * LANES: iq_col + (g + 1) * LANES].astype(BF16)
                 for g in range(IDX_HEADS * IDX_DIM // LANES)]
    iw = [jnp.broadcast_to(sq[:, iw_col + h: iw_col + h + 1], (tq, tq)) for h in range(IDX_HEADS)]
    row = lax.broadcasted_iota(I32, (tq, tq), 0)
    col = lax.broadcasted_iota(I32, (tq, tq), 1)
    chunk_end = ((q0 + row) // CHUNK + 1) * CHUNK

    key_ref[...] = jnp.full((tq, seq), INT_MIN, I32)

    def score_tile(j, c):
        ikt = _rows(ik_ref, j, tq)
        ik_lo = jnp.where(lane_k < IDX_DIM, ikt, 0.0)
        ik_hi = pltpu.roll(ik_lo, IDX_DIM, 1)
        ik_lo = ik_lo.astype(BF16)
        ik_hi = ik_hi.astype(BF16)
        score = jnp.zeros((tq, tq), F32)
        for h in range(IDX_HEADS):
            s = _qk(iq_groups[h // 2], ik_hi if h % 2 else ik_lo)
            score = score + iw[h] * jnp.maximum(s, 0.0)
        vis = (j * tq + col) < chunk_end
        key_ref[:, pl.ds(pl.multiple_of(j * tq, tq), tq)] = jnp.where(vis, _sortable(score), INT_MIN)
        return c

    lax.fori_loop(0, i + 1, score_tile, 0)

    def count_ge(cand, ncols=seq):
        return jnp.sum(jnp.where(key_ref[:, :ncols] >= cand, 1.0, 0.0), axis=-1, keepdims=True)

    kf = jnp.float32(topk)

    def find_threshold(ncols):
        thr = jnp.where(count_ge(jnp.zeros((tq, 1), I32), ncols) >= kf, 0, INT_MIN).astype(I32)

        def bisect(b, thr):
            cand = thr | lax.shift_left(jnp.int32(1), 30 - b)
            return jnp.where(count_ge(cand, ncols) >= kf, cand, thr)

        thr = lax.fori_loop(0, jnp.where((i + 1) * tq <= topk, 0, 31), bisect, thr)
        thr_ref[...] = jnp.maximum(thr, INT_MIN + 1)

    lo = 0
    for hi in sorted({min(n_tiles, v) for v in range(2, n_tiles + 2, 2)}):
        pl.when((i + 1 > lo) & (i + 1 <= hi))(functools.partial(find_threshold, hi * tq))
        lo = hi
    thr = thr_ref[...]
    n_ge = count_ge(thr)

    @pl.when(jnp.max(n_ge) > kf)
    def _():
        need = kf - jnp.sum(jnp.where(key_ref[...] > thr, 1.0, 0.0), axis=-1, keepdims=True)
        upto = jnp.where(row <= col, 1.0, 0.0).astype(BF16)

        def fix(j, seen):
            cols = pl.ds(pl.multiple_of(j * tq, tq), tq)
            kt = key_ref[:, cols]
            eq = kt == thr
            rank = jnp.dot(jnp.where(eq, 1.0, 0.0).astype(BF16), upto, preferred_element_type=F32) + seen
            key_ref[:, cols] = jnp.where(eq & (rank > need), INT_MIN, kt)
            return seen + jnp.sum(jnp.where(eq, 1.0, 0.0), axis=-1, keepdims=True)

        lax.fori_loop(0, n_tiles, fix, jnp.zeros((tq, 1), F32))

    qs = jnp.concatenate([q_ref[:, h * HEAD_DIM:(h + 1) * HEAD_DIM] for h in range(heads)], axis=0)

    def attn_tile(j, carry):
        kvt = _rows(kv_ref, j, tq)
        kt = kvt[:, :HEAD_DIM].astype(BF16)
        vt = kvt[:, HEAD_DIM:].astype(BF16)
        bias = jnp.where(key_ref[:, pl.ds(pl.multiple_of(j * tq, tq), tq)] >= thr, 0.0, NEG)
        s = _qk(qs, kt).reshape(heads, tq, tq) + bias[None]
        return _softmax_tile(carry, s.reshape(heads * tq, tq), vt)

    _, l, acc = lax.fori_loop(0, i + 1, attn_tile, _softmax_init(heads * tq))
    o = acc / l
    for h in range(heads):
        o_ref[:, h * HEAD_DIM:(h + 1) * HEAD_DIM] = o[h * tq:(h + 1) * tq].astype(o_ref.dtype)


def _attn_sparse(dq, small, heads, batch, seq, tq, topk, iq_col, ik_col, iw_col):
    nq = seq // tq
    t = batch * seq
    bw = heads * HEAD_DIM
    w = small.shape[1]
    kern = functools.partial(_attn_sparse_kernel, tq=tq, seq=seq, heads=heads, topk=topk,
                             iq_col=iq_col, iw_col=iw_col)
    return pl.pallas_call(
        kern,
        grid=(batch, nq),
        in_specs=[pl.BlockSpec((tq, bw), lambda b, i: (b * nq + i, 0)),
                  pl.BlockSpec((tq, w), lambda b, i: (b * nq + i, 0)),
                  pl.BlockSpec((seq, 2 * HEAD_DIM), lambda b, i: (b, 0)),
                  pl.BlockSpec((seq, LANES), lambda b, i: (b, ik_col // LANES))],
        out_specs=pl.BlockSpec((tq, bw), lambda b, i: (b * nq + i, 0)),
        out_shape=jax.ShapeDtypeStruct((t, bw), BF16),
        scratch_shapes=[pltpu.VMEM((tq, seq), I32), pltpu.VMEM((tq, 1), I32)],
        compiler_params=_params(2),
        name="attn_sparse",
    )(dq, small, small, small)


def _forget_cumsum_kernel(x_ref, b_ref, o_ref, *, seq, blk):
    r = lax.broadcasted_iota(I32, (blk, blk), 0)
    c = lax.broadcasted_iota(I32, (blk, blk), 1)
    tri = jnp.where(c <= r, 1.0, 0.0).astype(BF16)
    carry = jnp.zeros((1, LANES), F32)
    for s in range(seq // blk):
        logf = _log_sigmoid(x_ref[s * blk:(s + 1) * blk, :] + b_ref[...])
        hi = logf.astype(BF16)
        rem = logf - hi.astype(F32)
        mid = rem.astype(BF16)
        lo = (rem - mid.astype(F32)).astype(BF16)
        local = (jnp.dot(tri, hi, preferred_element_type=F32)
                 + jnp.dot(tri, mid, preferred_element_type=F32)
                 + jnp.dot(tri, lo, preferred_element_type=F32)) + carry
        o_ref[s * blk:(s + 1) * blk, :] = local
        carry = local[blk - 1:blk, :]


def _forget_cumsum(small, bias_row, batch, seq, col):
    return pl.pallas_call(
        functools.partial(_forget_cumsum_kernel, seq=seq, blk=_tile(seq, 256)),
        grid=(batch,),
        in_specs=[pl.BlockSpec((seq, LANES), lambda b: (b, col // LANES)),
                  pl.BlockSpec((1, LANES), lambda b: (0, 0))],
        out_specs=pl.BlockSpec((seq, LANES), lambda b: (b, 0)),
        out_shape=jax.ShapeDtypeStruct((batch * seq, LANES), F32),
        compiler_params=_params(1),
        name="forget_cumsum",
    )(small, bias_row)


def _merge_kernel(*refs, n_br, cast_rows):
    o_refs = refs[:n_br]
    g_refs = refs[n_br:2 * n_br]
    w_ref = refs[2 * n_br]
    out_ref = refs[2 * n_br + 1]
    wbf_ref = refs[2 * n_br + 2]

    @pl.when(pl.program_id(1) == 0)
    def _():
        for b in range(n_br):
            def body(r, c, b=b):
                rows = pl.ds(pl.multiple_of(r * cast_rows, cast_rows), cast_rows)
                wbf_ref[b, rows, :] = w_ref[b, rows, :].astype(BF16)
                return c
            lax.fori_loop(0, w_ref.shape[1] // cast_rows, body, 0)

    acc = None
    for b in range(n_br):
        term = g_refs[b][...] * jnp.dot(o_refs[b][...], wbf_ref[b], preferred_element_type=F32)
        acc = term if acc is None else acc + term
    out_ref[...] = acc.astype(out_ref.dtype)


def _merge(outs, gates, w_branch, layer, tm, tn):
    _, n_br, bw, d = w_branch.shape
    t = outs[0].shape[0]
    nd = d // tn
    kern = functools.partial(_merge_kernel, n_br=n_br, cast_rows=_tile(bw, 256))
    return pl.pallas_call(
        kern,
        grid=(nd, t // tm),
        in_specs=[*[pl.BlockSpec((tm, bw), lambda j, i: (i, 0)) for _ in range(n_br)],
                  *[pl.BlockSpec((tm, tn), lambda j, i, b=b: (i, b * nd + j)) for b in range(n_br)],
                  pl.BlockSpec((None, n_br, bw, tn), lambda j, i: (layer, 0, 0, j),
                               pipeline_mode=pl.Buffered(1))],
        out_specs=pl.BlockSpec((tm, tn), lambda j, i: (i, j)),
        out_shape=jax.ShapeDtypeStruct((t, d), BF16),
        scratch_shapes=[pltpu.VMEM((n_br, bw, tn), BF16)],
        compiler_params=_params(2),
        name="gated_merge",
    )(*outs, *([gates] * n_br), w_branch)


def _router_kernel(h_ref, w_ref, b_ref, meta_i_ref, meta_w_ref, cnt_ref, run_ref, *, tm, n_exp):
    @pl.when(pl.program_id(0) == 0)
    def _():
        run_ref[...] = jnp.zeros_like(run_ref)

    lane = lax.broadcasted_iota(I32, (tm, LANES), 1)
    lane_f = lane.astype(F32)
    logits = jnp.dot(h_ref[...], w_ref[...].astype(BF16), preferred_element_type=F32) + b_ref[...]
    logits = jnp.where(lane < n_exp, logits, -jnp.inf)

    vals, idxs = [], []
    work = logits
    for _ in range(TOP_K):
        v = jnp.max(work, axis=-1, keepdims=True)
        ix = jnp.min(jnp.where(work == v, lane_f, float(LANES)), axis=-1, keepdims=True).astype(I32)
        vals.append(v)
        idxs.append(ix)
        work = jnp.where(lane == ix, -jnp.inf, work)
    es = [jnp.exp(v - vals[0]) for v in vals]
    denom = es[0] + es[1] + es[2] + es[3]

    onehots = [jnp.where(lane == ix, 1.0, 0.0) for ix in idxs]
    sel = onehots[0] + onehots[1] + onehots[2] + onehots[3]
    r = lax.broadcasted_iota(I32, (tm, tm), 0)
    c = lax.broadcasted_iota(I32, (tm, tm), 1)
    before = jnp.where(c < r, 1.0, 0.0).astype(BF16)
    rank = jnp.dot(before, sel.astype(BF16), preferred_element_type=F32) + run_ref[...]
    run_ref[...] = run_ref[...] + jnp.sum(sel, axis=0, keepdims=True)

    meta_i = jnp.zeros((tm, LANES), I32)
    meta_w = jnp.zeros((tm, LANES), F32)
    for k in range(TOP_K):
        rk = jnp.sum(onehots[k] * rank, axis=-1, keepdims=True).astype(I32)
        meta_i = jnp.where(lane == k, idxs[k], meta_i)
        meta_i = jnp.where(lane == TOP_K + k, rk, meta_i)
        meta_w = jnp.where(lane == k, es[k] / denom, meta_w)
    meta_i_ref[...] = meta_i
    meta_w_ref[...] = meta_w
    cnt_ref[...] = run_ref[...]


def _router(h, w_router_pad, b_router_pad, n_exp):
    t, d = h.shape
    tm = _tile(t, 256)
    kern = functools.partial(_router_kernel, tm=tm, n_exp=n_exp)
    return pl.pallas_call(
        kern,
        grid=(t // tm,),
        in_specs=[pl.BlockSpec((tm, d), lambda i: (i, 0)),
                  pl.BlockSpec((d, LANES), lambda i: (0, 0)),
                  pl.BlockSpec((1, LANES), lambda i: (0, 0))],
        out_specs=[pl.BlockSpec((tm, LANES), lambda i: (i, 0)),
                   pl.BlockSpec((tm, LANES), lambda i: (i, 0)),
                   pl.BlockSpec((1, LANES), lambda i: (0, 0))],
        out_shape=[jax.ShapeDtypeStruct((t, LANES), I32),
                   jax.ShapeDtypeStruct((t, LANES), F32),
                   jax.ShapeDtypeStruct((1, LANES), F32)],
        scratch_shapes=[pltpu.VMEM((1, LANES), F32)],
        compiler_params=_params(1),
        name="router",
    )(h, w_router_pad, b_router_pad)


def _ffn_kernel(te_ref, nu_ref, tok_ref, h_ref, wgu_hbm, bgu_ref, wdn_hbm, bdn_ref, o_ref,
                xbuf, wgu_buf, wdn_buf, pick_ref, sem, wsem, *, tm, nseg, kc, nc, layer):
    i = pl.program_id(0)
    n_used = nu_ref[0]
    d, ff = wgu_buf.shape[0], wdn_buf.shape[0]
    pitch = _row_pitch(nseg)
    n_kc, n_nc = d // kc, d // nc

    def row_copy(tile, slot, r):
        tok = tok_ref[tile * tm + r]
        return pltpu.make_async_copy(h_ref.at[pl.ds(tok * pitch, nseg), :],
                                     xbuf.at[slot, pl.ds(r * pitch, nseg), :], sem.at[slot])

    def fetch(tile, slot):
        def body(r, c):
            row_copy(tile, slot, r).start()
            return c
        lax.fori_loop(0, tm, body, 0, unroll=4)

    def wait_rows(slot):
        done = xbuf.at[slot, pl.ds(0, tm * nseg), :]
        pltpu.make_async_copy(done, done, sem.at[slot]).wait()

    def gu_copy(e, c):
        rows = pl.ds(c * kc, kc)
        return pltpu.make_async_copy(wgu_hbm.at[layer, e, rows, :], wgu_buf.at[rows, :], wsem.at[c])

    def dn_copy(e, c):
        cols = pl.ds(c * nc, nc)
        return pltpu.make_async_copy(wdn_hbm.at[layer, e, :, cols], wdn_buf.at[:, cols],
                                     wsem.at[n_kc + c])

    active = i < n_used
    new_expert = (i == 0) | (te_ref[i] != te_ref[jnp.maximum(i - 1, 0)])

    @pl.when(active & new_expert)
    def _():
        for c in range(n_kc):
            gu_copy(te_ref[i], c).start()
        for c in range(n_nc):
            dn_copy(te_ref[i], c).start()

    @pl.when(i == 0)
    def _():
        r = lax.broadcasted_iota(I32, (2 * ff, ff), 0)
        c = lax.broadcasted_iota(I32, (2 * ff, ff), 1)
        pick_ref[...] = jnp.where(r == 2 * c, 1.0, 0.0).astype(BF16)
        fetch(0, 0)

    @pl.when(i + 1 < n_used)
    def _():
        fetch(i + 1, (i + 1) % 2)

    def compute(arriving):
        slot = i % 2
        wait_rows(slot)
        parts = _load_token_rows(xbuf, tm, nseg, lead=(slot,))
        per = kc // LANES
        gu = bgu_ref[...]
        for c in range(n_kc):
            if arriving is not None:
                gu_copy(arriving, c).wait()
            xc = jnp.concatenate(parts[c * per:(c + 1) * per], axis=1).astype(BF16)
            gu = gu + jnp.dot(xc, wgu_buf[c * kc:(c + 1) * kc, :].astype(BF16),
                              preferred_element_type=F32)
        lin = pltpu.roll(gu, 2 * ff - 1, 1)
        glu = jnp.minimum(gu, SWIGLU_LIMIT)
        lin = jnp.clip(lin, -SWIGLU_LIMIT, SWIGLU_LIMIT)
        act = glu * jax.nn.sigmoid(SWIGLU_ALPHA * glu) * (lin + 1.0)
        act = jnp.dot(act.astype(BF16), pick_ref[...], preferred_element_type=F32).astype(BF16)
        for c in range(n_nc):
            if arriving is not None:
                dn_copy(arriving, c).wait()
            cols = slice(c * nc, (c + 1) * nc)
            o_ref[:, cols] = (jnp.dot(act, wdn_buf[:, cols].astype(BF16), preferred_element_type=F32)
                              + bdn_ref[:, cols])

    pl.when(active & new_expert)(lambda: compute(te_ref[i]))
    pl.when(active & jnp.logical_not(new_expert))(lambda: compute(None))

    @pl.when(i >= n_used)
    def _():
        o_ref[...] = jnp.zeros_like(o_ref)


def _expert_ffn(h_rows, tile_expert, n_used, token_of_row, w_gu, b_gu, w_dn, b_dn, layer, tm):
    _, n_exp, d, ff2 = w_gu.shape
    ff = ff2 // 2
    nseg = d // LANES
    n_tiles = token_of_row.shape[0] // tm
    kc, nc = _tile(d, 512), _tile(d, 1024)

    def wsel(i, te, nu, tok):
        return (layer, te[i], 0, 0)

    kern = functools.partial(_ffn_kernel, tm=tm, nseg=nseg, kc=kc, nc=nc, layer=layer)
    return pl.pallas_call(
        kern,
        grid_spec=pltpu.PrefetchScalarGridSpec(
            num_scalar_prefetch=3,
            grid=(n_tiles,),
            in_specs=[pl.BlockSpec(memory_space=pl.ANY),
                      pl.BlockSpec(memory_space=pl.ANY),
                      pl.BlockSpec((None, None, 1, ff2), wsel),
                      pl.BlockSpec(memory_space=pl.ANY),
                      pl.BlockSpec((None, None, 1, d), wsel)],
            out_specs=pl.BlockSpec((tm, d), lambda i, te, nu, tok: (i, 0)),
            scratch_shapes=[pltpu.VMEM((2, tm * _row_pitch(nseg), LANES), F32),
                            pltpu.VMEM((d, ff2), F32),
                            pltpu.VMEM((ff, d), F32),
                            pltpu.VMEM((ff2, ff), BF16),
                            pltpu.SemaphoreType.DMA((2,)),
                            pltpu.SemaphoreType.DMA((d // kc + d // nc,))]),
        out_shape=jax.ShapeDtypeStruct((n_tiles * tm, d), F32),
        compiler_params=_params(1),
        name="expert_ffn",
    )(tile_expert, n_used, token_of_row, h_rows, w_gu, b_gu, w_dn, b_dn)


def _combine_kernel(pos_ref, ys_ref, x_ref, g_ref, w_ref, o_ref, buf, sem, *, tm):
    i = pl.program_id(0)
    last = pl.num_programs(0) - 1

    def row_copy(tile, slot, t, k):
        row = pos_ref[(tile * tm + t) * TOP_K + k]
        return pltpu.make_async_copy(ys_ref.at[pl.ds(row, 1), :], buf.at[slot, k, pl.ds(t, 1), :],
                                     sem.at[slot])

    def wait_rows(slot):
        pltpu.make_async_copy(buf.at[slot], buf.at[slot], sem.at[slot]).wait()

    def fetch(tile, slot):
        def body(t, c):
            for k in range(TOP_K):
                row_copy(tile, slot, t, k).start()
            return c
        lax.fori_loop(0, tm, body, 0, unroll=2)

    @pl.when(i == 0)
    def _():
        fetch(0, 0)

    @pl.when(i < last)
    def _():
        fetch(i + 1, (i + 1) % 2)

    slot = i % 2
    wait_rows(slot)
    w = w_ref[...]
    y = w[:, 0:1] * buf[slot, 0]
    for k in range(1, TOP_K):
        y = y + w[:, k:k + 1] * buf[slot, k]
    o_ref[...] = x_ref[...] + g_ref[...] * y


def _combine(ys, pos_flat, x, m4, gate_idx, meta_w, seq, tm):
    t, d = x.shape
    per_b = seq // tm
    return pl.pallas_call(
        functools.partial(_combine_kernel, tm=tm),
        grid_spec=pltpu.PrefetchScalarGridSpec(
            num_scalar_prefetch=1,
            grid=(t // tm,),
            in_specs=[pl.BlockSpec(memory_space=pl.ANY),
                      pl.BlockSpec((tm, d), lambda i, p: (i, 0)),
                      pl.BlockSpec((None, None, 1, d), lambda i, p: (i // per_b, gate_idx, 0, 0)),
                      pl.BlockSpec((tm, LANES), lambda i, p: (i, 0))],
            out_specs=pl.BlockSpec((tm, d), lambda i, p: (i, 0)),
            scratch_shapes=[pltpu.VMEM((2, TOP_K, tm, d), F32),
                            pltpu.SemaphoreType.DMA((2,))]),
        out_shape=jax.ShapeDtypeStruct((t, d), F32),
        compiler_params=_params(1),
        name="moe_combine",
    )(pos_flat, ys, x, m4, meta_w)


def _moe(h_bf, h_rows, x, m4, w_router, b_router, w_gu, b_gu, w_dn, b_dn, layer, seq):
    t, d = x.shape
    n_exp = w_router.shape[1]
    tm = _tile(t, 256)
    wr = jnp.pad(w_router, ((0, 0), (0, LANES - n_exp)))
    br = jnp.pad(b_router, (0, LANES - n_exp))[None, :]
    meta_i, meta_w, counts = _router(h_bf, wr, br, n_exp)

    counts = counts[0, :n_exp].astype(I32)
    padded = ((counts + tm - 1) // tm) * tm
    ends = jnp.cumsum(padded)
    starts = ends - padded
    n_rows = t * TOP_K + n_exp * tm
    n_tiles = n_rows // tm
    experts = meta_i[:, :TOP_K]
    pos = starts[experts] + meta_i[:, TOP_K:2 * TOP_K]
    tile_start = jnp.arange(n_tiles, dtype=I32) * tm
    tile_expert = jnp.minimum(jnp.sum((ends[None, :] <= tile_start[:, None]).astype(I32), axis=1),
                              n_exp - 1)
    n_used = (ends[-1] // tm).astype(I32)[None]
    token_of_row = jnp.zeros((n_rows,), I32).at[pos.reshape(-1)].set(
        jnp.repeat(jnp.arange(t, dtype=I32), TOP_K))

    ys = _expert_ffn(h_rows, tile_expert, n_used, token_of_row, w_gu, b_gu[:, :, None, :], w_dn,
                     b_dn[:, :, None, :], layer, tm)
    return _combine(ys, pos.reshape(-1), x, m4, 5, meta_w, seq, _tile(seq, 128))


def _col_spec(k, tn, off_blocks=0):
    return pl.BlockSpec((k, tn), lambda j, i: (0, j + off_blocks))


def _layer_col_spec(layer, k, tn, off_blocks=0):
    return pl.BlockSpec((None, k, tn), lambda j, i: (layer, 0, j + off_blocks))


def _layer(x, m4, tabs, lam_init, batch, seq, layer, p):
    t, d = x.shape
    bw = p["w_branch"].shape[2]
    heads = bw // HEAD_DIM
    tm = _tile(seq, 1024)
    tn = _tile(bw, 512)
    w_in_t = p["w_in_t"]
    zero = lambda n: jnp.zeros((1, n), F32)

    (h,) = _ada_norm(x, p["norm1_g"], m4, 1, 0, seq, (BF16,))

    tab8, tab16, tab_small = tabs
    tab_spec = lambda w: pl.BlockSpec((tm, w), lambda j, i: (i, 0))
    rope8 = functools.partial(_rope_epilogue, groups=((0, 8),) * (tn // LANES))
    rope16 = functools.partial(_rope_epilogue, groups=((0, 16),) * (tn // LANES))

    def in_proj(row0, n, name, **kw):
        spec = pl.BlockSpec((pl.Element(1), pl.Element(tn), pl.Element(d)),
                            lambda j, i: (layer, pl.multiple_of(row0 + j * tn, 8), 0))
        return _matmul(h, w_in_t, spec, d, n, zero(n), tm=tm, tn=tn, out_dtype=BF16,
                       w_rows_are_outputs=True, name=name, **kw)

    sm_scale = HEAD_DIM ** -0.5
    col_spec = pl.BlockSpec((1, tn), lambda j, i: (0, j))

    def col_factors(*pieces):
        return jnp.concatenate([jnp.full((1, n), v, F32) for v, n in pieces], axis=1)

    c0 = 6 * bw + heads
    qk_a = in_proj(0, 2 * bw, "proj_a_qk", epilogue=lambda acc, tab, f: rope8(acc, tab) * f,
                   extra=(tab8, col_factors(((HEAD_DIM // 2) ** -0.5 * LOG2E, bw), (1.0, bw))),
                   extra_specs=(tab_spec(3 * LANES), col_spec))
    mid = in_proj(2 * bw, 4 * bw, "proj_av_b", epilogue=lambda acc, f: acc * f,
                  extra=(col_factors((1.0, bw), (sm_scale * LOG2E, bw), (1.0, 2 * bw)),),
                  extra_specs=(col_spec,))
    qkv_c = in_proj(c0, 3 * bw, "proj_c", epilogue=lambda acc, f: acc * f,
                    extra=(col_factors((sm_scale, bw), (1.0, 2 * bw)),), extra_specs=(col_spec,))
    q_d = in_proj(c0 + 3 * bw, bw, "proj_d_q",
                  epilogue=lambda acc, tab: rope16(acc, tab) * (sm_scale * LOG2E),
                  extra=(tab16,), extra_specs=(tab_spec(3 * LANES),))
    s0 = c0 + 4 * bw
    n_iq = IDX_HEADS * IDX_DIM
    used = 2 * HEAD_DIM + n_iq + IDX_DIM + IDX_HEADS + heads
    ws = -(-used // LANES) * LANES
    w_small_t = jnp.concatenate(
        [w_in_t[layer, s0:s0 + 2 * HEAD_DIM + n_iq + IDX_DIM + IDX_HEADS],
         w_in_t[layer, 6 * bw:6 * bw + heads], jnp.zeros((ws - used, d), F32)], axis=0)
    ik_col = 2 * HEAD_DIM + n_iq
    iw_col = ik_col + IDX_DIM
    fl_col = iw_col + IDX_HEADS
    groups_small = ((0, 16), None) + ((1, 8),) * (n_iq // LANES) + ((2, 8),)
    tm_s = _tile(seq, 512)
    small = _matmul(h, w_small_t, pl.BlockSpec((ws, d), lambda j, i: (0, 0), pipeline_mode=pl.Buffered(1)),
                    d, ws, zero(ws), tm=tm_s, tn=ws, out_dtype=F32, w_rows_are_outputs=True,
                    epilogue=functools.partial(_rope_epilogue, groups=groups_small), extra=(tab_small,),
                    extra_specs=(pl.BlockSpec((tm_s, 9 * LANES), lambda j, i: (i, 0)),), name="proj_small")

    fl_lane = fl_col % LANES
    bias_row = jnp.zeros((1, LANES), F32).at[0, fl_lane:fl_lane + heads].set(p["b_forget"])
    dcum = _forget_cumsum(small, bias_row, batch, seq, ik_col)[:, fl_lane:fl_lane + heads]
    dcum = dcum.reshape(batch, seq, heads).transpose(0, 2, 1)
    dq_col = dcum[..., None]
    dk_row = dcum[:, :, None, :]

    tq = tk = _tile(seq, 512)
    hb = bw // HEAD_DIM
    o_a = _attention(
        functools.partial(_attn_diff_kernel, tq=tq, tk=tk, seq=seq, lam_init=lam_init),
        qk_a, 0, qk_a, hb, mid, 0, heads, batch, seq,
        pre=(p["diff_lambda"], p["diff_norm_g"][None, :]),
        pre_specs=(pl.BlockSpec(p["diff_lambda"].shape, lambda b, h: (0, 0)),
                   pl.BlockSpec((1, HEAD_DIM), lambda b, h: (0, 0))),
        name="attn_diff")
    o_b = _attention(
        functools.partial(_attn_forget_kernel, tq=tq, tk=tk, seq=seq),
        mid, hb, mid, 2 * hb, mid, 3 * hb, heads, batch, seq,
        pre=(dq_col, dk_row),
        pre_specs=(pl.BlockSpec((None, None, seq, 1), lambda b, h: (b, h, 0, 0)),
                   pl.BlockSpec((None, None, 1, seq), lambda b, h: (b, h, 0, 0))),
        name="attn_forget")
    o_c = _attention(functools.partial(_attn_stick_kernel, tq=tq, tk=tk, seq=seq),
                     qkv_c, 0, qkv_c, hb, qkv_c, 2 * hb, heads, batch, seq, name="attn_stick")
    o_d = _attn_sparse(q_d, small, heads, batch, seq, _tile(seq, 256), min(TOPK_MAX, seq // 4),
                       2 * HEAD_DIM, ik_col, iw_col)

    n_br = p["w_gate"].shape[1]
    nd = d // tn
    gates = _matmul(h, p["w_gate"],
                    pl.BlockSpec((None, None, d, tn), lambda j, i: (layer, j // nd, 0, j % nd)),
                    d, n_br * d, p["b_gate"].reshape(1, n_br * d), tm=tm, tn=tn, out_dtype=F32,
                    epilogue=jax.nn.sigmoid, name="gates")
    merged = _merge((o_a, o_b, o_c, o_d), gates, p["w_branch"], layer, tm, tn)
    per_b = seq // tm
    x = _matmul(merged, p["w_out"], _layer_col_spec(layer, d, tn), d, d, zero(d), tm=tm, tn=tn,
                out_dtype=F32, epilogue=lambda acc, xr, g: xr + g * acc, extra=(x, m4),
                extra_specs=(pl.BlockSpec((tm, tn), lambda j, i: (i, j)),
                             pl.BlockSpec((None, None, 1, tn), lambda j, i: (i // per_b, 2, 0, j))),
                name="out_proj")

    h_bf, h_rows = _ada_norm(x, p["norm2_g"], m4, 4, 3, seq, (BF16, F32), token_rows=(False, True))
    return _moe(h_bf, h_rows, x, m4, p["w_router"], p["b_router"], p["w_gu"], p["b_gu"],
                p["w_dn"], p["b_dn"], layer, seq)


@jax.jit
def _forward(x, c, positions, w_mod, b_mod, ada_table, norm1_g, w_in, b_forget, diff_lambda,
             diff_norm_g, w_branch, w_gate, b_gate, w_out, norm2_g, w_router, b_router,
             w_gu, b_gu, w_dn, b_dn, final_norm_g):
    batch, seq, d = x.shape
    depth = w_in.shape[0]
    t = batch * seq
    xt = x.reshape(t, d)

    c8 = jnp.pad(c, ((0, 8 - batch), (0, 0)))
    n_mod = w_mod.shape[1]
    mod = _matmul(c8, w_mod, _col_spec(d, _tile(n_mod, 512)), d, n_mod, b_mod[None, :], tm=8,
                  tn=_tile(n_mod, 512), out_dtype=F32, prologue=lambda v: v * jax.nn.sigmoid(v),
                  name="mod_proj")[:batch]
    mod = mod.reshape(batch, N_MOD, d)

    pos = positions.reshape(t).astype(F32)
    tabs = (_rope_table(pos, HEAD_DIM // 2, LANES),
            _rope_table(pos, HEAD_DIM, LANES),
            jnp.concatenate([_rope_table(pos, HEAD_DIM, LANES), _rope_table(pos, IDX_DIM, LANES),
                             _rope_table(pos, IDX_DIM, IDX_DIM)], axis=1))

    w_in_t = jnp.swapaxes(w_in, 1, 2)
    for l in range(depth):
        m4 = (mod + ada_table[l][None])[:, :, None, :]
        lam_init = 0.8 - 0.6 * float(np.exp(-0.3 * l))
        p = dict(norm1_g=norm1_g[l][None, :], w_in_t=w_in_t, b_forget=b_forget[l],
                 diff_lambda=diff_lambda[l], diff_norm_g=diff_norm_g[l], w_branch=w_branch,
                 w_gate=w_gate, b_gate=b_gate[l], w_out=w_out, norm2_g=norm2_g[l][None, :],
                 w_router=w_router[l], b_router=b_router[l], w_gu=w_gu, b_gu=b_gu,
                 w_dn=w_dn, b_dn=b_dn)
        xt = _layer(xt, m4, tabs, lam_init, batch, seq, l, p)

    zeros4 = jnp.zeros((batch, 1, 1, d), F32)
    (out,) = _ada_norm(xt, final_norm_g[None, :], zeros4, 0, 0, seq, (F32,))
    return out.reshape(batch, seq, d)


def kernel(x, c, positions, w_mod, b_mod, ada_table, norm1_g, w_in, b_forget, diff_lambda,
           diff_norm_g, w_branch, w_gate, b_gate, w_out, norm2_g, w_router, b_router,
           w_gu, b_gu, w_dn, b_dn, final_norm_g):
    return _forward(x, c, positions, w_mod, b_mod, ada_table, norm1_g, w_in, b_forget, diff_lambda,
                    diff_norm_g, w_branch, w_gate, b_gate, w_out, norm2_g, w_router, b_router,
                    w_gu, b_gu, w_dn, b_dn, final_norm_g)
```

```python
import functools

import numpy as np
import jax
import jax.numpy as jnp
from jax import lax
from jax.experimental import pallas as pl
from jax.experimental.pallas import tpu as pltpu

F32 = jnp.float32
BF16 = jnp.bfloat16
I32 = jnp.int32

LANES = 128
HEAD_DIM = 128
CHUNK = 64
IDX_HEADS = 8
IDX_DIM = 64
TOPK_MAX = 256
ROPE_THETA = 500000.0
ROPE_FRACTION = 4
TOP_K = 4
SWIGLU_LIMIT = 7.0
SWIGLU_ALPHA = 1.702
RMS_EPS = 1e-5
N_MOD = 6
NEG = -1e30
LOG2E = 1.4426950408889634
INT_MIN = -(2 ** 31)
ROW_PITCH_PAD = 8
VMEM_LIMIT = 56 * 1024 * 1024


def _params(n_axes, vmem=VMEM_LIMIT):
    return pltpu.CompilerParams(dimension_semantics=("arbitrary",) * n_axes,
                                vmem_limit_bytes=vmem)


def _tile(n, pref):
    t = min(n, pref)
    while n % t:
        t //= 2
    return t


def _row_pitch(nseg):
    return nseg + ROW_PITCH_PAD


def _store_token_rows(o_ref, val, tm):
    nseg = val.shape[1] // LANES
    pitch = _row_pitch(nseg)
    for s in range(nseg):
        o_ref[pl.ds(s, tm, stride=pitch), :] = val[:, s * LANES:(s + 1) * LANES]
    for s in range(nseg, pitch):
        o_ref[pl.ds(s, tm, stride=pitch), :] = jnp.zeros((tm, LANES), o_ref.dtype)


def _load_token_rows(ref, tm, nseg, lead=()):
    return [ref[(*lead, pl.ds(s, tm, stride=_row_pitch(nseg)), slice(None))] for s in range(nseg)]


def _norm_kernel(x_ref, g_ref, sc_ref, sh_ref, *o_refs, tm, token_rows):
    x = x_ref[...]
    var = jnp.mean(x * x, axis=-1, keepdims=True)
    y = x * lax.rsqrt(var + RMS_EPS) * g_ref[...]
    h = y * (1.0 + sc_ref[...]) + sh_ref[...]
    for o, rows in zip(o_refs, token_rows):
        if rows:
            _store_token_rows(o, h, tm)
        else:
            o[...] = h.astype(o.dtype)


def _ada_norm(x, g, m4, sc_idx, sh_idx, seq, out_dtypes, token_rows=None):
    t, d = x.shape
    tm = _tile(seq, 256)
    per_b = seq // tm
    pitch = _row_pitch(d // LANES)
    token_rows = token_rows or (False,) * len(out_dtypes)
    out_specs, out_shape = [], []
    for dt, rows in zip(out_dtypes, token_rows):
        if rows:
            out_specs.append(pl.BlockSpec((tm * pitch, LANES), lambda i: (i, 0)))
            out_shape.append(jax.ShapeDtypeStruct((t * pitch, LANES), F32))
        else:
            out_specs.append(pl.BlockSpec((tm, d), lambda i: (i, 0)))
            out_shape.append(jax.ShapeDtypeStruct((t, d), dt))
    return pl.pallas_call(
        functools.partial(_norm_kernel, tm=tm, token_rows=token_rows),
        grid=(t // tm,),
        in_specs=[
            pl.BlockSpec((tm, d), lambda i: (i, 0)),
            pl.BlockSpec((1, d), lambda i: (0, 0)),
            pl.BlockSpec((None, None, 1, d), lambda i: (i // per_b, sc_idx, 0, 0)),
            pl.BlockSpec((None, None, 1, d), lambda i: (i // per_b, sh_idx, 0, 0)),
        ],
        out_specs=out_specs,
        out_shape=out_shape,
        compiler_params=_params(1),
        name="ada_norm",
    )(x, g, m4, m4)


def _mm_kernel(*refs, n_extra, epilogue, prologue, cast_rows, w_rows_are_outputs):
    x_ref, w_ref, b_ref = refs[:3]
    extra = refs[3:3 + n_extra]
    o_ref = refs[3 + n_extra]
    wbf_ref = refs[4 + n_extra]
    if len(w_ref.shape) == 3:
        w_ref = w_ref.at[0]

    @pl.when(pl.program_id(1) == 0)
    def _():
        def body(r, c):
            rows = pl.ds(pl.multiple_of(r * cast_rows, cast_rows), cast_rows)
            wbf_ref[rows, :] = w_ref[rows, :].astype(BF16)
            return c

        lax.fori_loop(0, w_ref.shape[0] // cast_rows, body, 0)

    x = x_ref[...]
    if prologue is not None:
        x = prologue(x)
    x = x.astype(BF16)
    if w_rows_are_outputs:
        acc = _qk(x, wbf_ref[...])
    else:
        acc = jnp.dot(x, wbf_ref[...], preferred_element_type=F32)
    o_ref[...] = epilogue(acc + b_ref[...], *[e[...] for e in extra]).astype(o_ref.dtype)


def _matmul(x, w, w_spec, k, n, bias, *, tm, tn, out_dtype, epilogue=None, extra=(),
            extra_specs=(), prologue=None, w_rows_are_outputs=False, name="matmul"):
    m = x.shape[0]
    if epilogue is None:
        epilogue = lambda acc: acc
    w_block = (tn, k) if w_rows_are_outputs else (k, tn)
    kern = functools.partial(_mm_kernel, n_extra=len(extra), epilogue=epilogue, prologue=prologue,
                             cast_rows=_tile(w_block[0], 256), w_rows_are_outputs=w_rows_are_outputs)
    return pl.pallas_call(
        kern,
        grid=(n // tn, m // tm),
        in_specs=[pl.BlockSpec((tm, k), lambda j, i: (i, 0)),
                  w_spec,
                  pl.BlockSpec((1, tn), lambda j, i: (0, j)),
                  *extra_specs],
        out_specs=pl.BlockSpec((tm, tn), lambda j, i: (i, j)),
        out_shape=jax.ShapeDtypeStruct((m, n), out_dtype),
        scratch_shapes=[pltpu.VMEM(w_block, BF16)],
        compiler_params=_params(2),
        name=name,
    )(x, w, bias, *extra)


def _rope_epilogue(acc, tab, *, groups):
    outs = []
    for g, kind in enumerate(groups):
        xg = acc[:, g * LANES:(g + 1) * LANES]
        if kind is not None:
            slot, half = kind
            base = 3 * slot * LANES
            cos = tab[:, base:base + LANES]
            sin_lo = tab[:, base + LANES:base + 2 * LANES]
            sin_hi = tab[:, base + 2 * LANES:base + 3 * LANES]
            xg = (xg * cos + pltpu.roll(xg, half, 1) * sin_lo
                  + pltpu.roll(xg, LANES - half, 1) * sin_hi)
        outs.append(xg)
    return outs[0] if len(outs) == 1 else jnp.concatenate(outs, axis=1)


def _rope_table(pos, period, active):
    rot = period // ROPE_FRACTION
    half = rot // 2
    inv_freq = jnp.float32(ROPE_THETA) ** (-jnp.arange(half, dtype=F32) / half)
    ang = pos[:, None] * inv_freq[None, :]
    cos, sin = jnp.cos(ang), jnp.sin(ang)
    t = pos.shape[0]
    one = jnp.ones((t, period - rot), F32)
    zero_h = jnp.zeros((t, half), F32)
    zero_r = jnp.zeros((t, period - rot), F32)

    def lanes(block, idle):
        reps = [block] * (active // period) + [idle] * ((LANES - active) // period)
        return jnp.concatenate(reps, axis=1)

    c = lanes(jnp.concatenate([cos, cos, one], axis=1), jnp.ones((t, period), F32))
    s_lo = lanes(jnp.concatenate([zero_h, sin, zero_r], axis=1), jnp.zeros((t, period), F32))
    s_hi = lanes(jnp.concatenate([-sin, zero_h, zero_r], axis=1), jnp.zeros((t, period), F32))
    return jnp.concatenate([c, s_lo, s_hi], axis=1)


def _qk(q, k):
    return lax.dot_general(q, k, (((1,), (1,)), ((), ())), preferred_element_type=F32)


def _softmax_tile(carry, s, v):
    m, l, acc = carry
    m_new = jnp.maximum(m, jnp.max(s, axis=-1, keepdims=True))
    alpha = jnp.exp2(m - m_new)
    p = jnp.exp2(s - m_new)
    l = alpha * l + jnp.sum(p, axis=-1, keepdims=True)
    acc = alpha * acc + jnp.dot(p.astype(BF16), v, preferred_element_type=F32)
    return m_new, l, acc


def _softmax_init(rows):
    return (jnp.full((rows, 1), NEG, F32), jnp.zeros((rows, 1), F32),
            jnp.zeros((rows, HEAD_DIM), F32))


def _rows(ref, j, t):
    return ref[pl.ds(pl.multiple_of(j * t, t), t), :]


def _key_chunks(i, tq, tk):
    full = [(s, min(tk, i * tq - s), False) for s in range(0, i * tq, tk)]
    return full + [(i * tq, tq, True)]


def _attn_diff_kernel(dl_ref, g_ref, q_ref, k_ref, v_ref, o_ref, *, tq, tk, seq, lam_init):
    half = HEAD_DIM // 2
    lane = lax.broadcasted_iota(I32, (tq, HEAD_DIM), 1)
    r = lax.broadcasted_iota(I32, (2 * tq, tq), 0)
    r = jnp.where(r >= tq, r - tq, r)
    c = lax.broadcasted_iota(I32, (2 * tq, tq), 1)
    vis = c < (r // CHUNK + 1) * CHUNK
    dl = dl_ref[...]
    lam = (jnp.exp(jnp.sum(dl[0:1] * dl[1:2], axis=-1, keepdims=True))
           - jnp.exp(jnp.sum(dl[2:3] * dl[3:4], axis=-1, keepdims=True)) + lam_init)

    for i in range(seq // tq):
        q = q_ref[i * tq:(i + 1) * tq, :]
        zero = jnp.zeros_like(q)
        qz = jnp.concatenate([jnp.where(lane < half, q, zero), jnp.where(lane >= half, q, zero)], axis=0)
        carry = _softmax_init(2 * tq)
        for k0, kw, diag in _key_chunks(i, tq, tk):
            s = _qk(qz, k_ref[k0:k0 + kw, :])
            if diag:
                s = jnp.where(vis, s, NEG)
            carry = _softmax_tile(carry, s, v_ref[k0:k0 + kw, :])
        _, l, acc = carry
        o = acc / l
        o = o[:tq] - lam * o[tq:]
        var = jnp.mean(o * o, axis=-1, keepdims=True)
        o_ref[i * tq:(i + 1) * tq, :] = (o * lax.rsqrt(var + RMS_EPS) * g_ref[...]
                                         * (1.0 - lam_init)).astype(o_ref.dtype)


def _attn_forget_kernel(dq_ref, dk_ref, q_ref, k_ref, v_ref, o_ref, *, tq, tk, seq):
    r = lax.broadcasted_iota(I32, (tq, tq), 0)
    c = lax.broadcasted_iota(I32, (tq, tq), 1)
    for i in range(seq // tq):
        q = q_ref[i * tq:(i + 1) * tq, :]
        dq2 = dq_ref[i * tq:(i + 1) * tq, :] * LOG2E
        carry = _softmax_init(tq)
        for k0, kw, diag in _key_chunks(i, tq, tk):
            dk2 = dk_ref[:, k0:k0 + kw] * LOG2E
            s = _qk(q, k_ref[k0:k0 + kw, :]) + dq2 - dk2
            if diag:
                s = jnp.where(c <= r, s, NEG)
            carry = _softmax_tile(carry, s, v_ref[k0:k0 + kw, :])
        _, l, acc = carry
        o_ref[i * tq:(i + 1) * tq, :] = (acc / l).astype(o_ref.dtype)


def _log_sigmoid(z):
    return jnp.minimum(z, 0.0) - jnp.log(1.0 + jnp.exp(-jnp.abs(z)))


def _attn_stick_kernel(q_ref, k_ref, v_ref, o_ref, *, tq, tk, seq):
    r = lax.broadcasted_iota(I32, (tq, tq), 0)
    c = lax.broadcasted_iota(I32, (tq, tq), 1)
    cb = min(tq, 256)
    later = jnp.where(lax.broadcasted_iota(I32, (cb, cb), 0) > lax.broadcasted_iota(I32, (cb, cb), 1),
                      1.0, 0.0).astype(BF16)

    def tile(q, k0, kw, carry, mask):
        suffix, acc = carry
        z = _qk(q, k_ref[k0:k0 + kw, :])
        log_beta = _log_sigmoid(z)
        log_keep = log_beta - z
        if mask is not None:
            log_keep = jnp.where(mask, log_keep, 0.0)
        parts = [None] * (kw // cb)
        for blk in reversed(range(kw // cb)):
            lk = log_keep[:, blk * cb:(blk + 1) * cb]
            hi = lk.astype(BF16)
            lo = (lk - hi.astype(F32)).astype(BF16)
            parts[blk] = (jnp.dot(hi, later, preferred_element_type=F32)
                          + jnp.dot(lo, later, preferred_element_type=F32) + suffix)
            suffix = suffix + jnp.sum(lk, axis=-1, keepdims=True)
        after = parts[0] if len(parts) == 1 else jnp.concatenate(parts, axis=1)
        w = jnp.exp(log_beta + after)
        if mask is not None:
            w = jnp.where(mask, w, 0.0)
        acc = acc + jnp.dot(w.astype(BF16), v_ref[k0:k0 + kw, :], preferred_element_type=F32)
        return suffix, acc

    for i in range(seq // tq):
        q = q_ref[i * tq:(i + 1) * tq, :]
        carry = (jnp.zeros((tq, 1), F32), jnp.zeros((tq, HEAD_DIM), F32))
        for k0, kw, diag in reversed(_key_chunks(i, tq, tk)):
            carry = tile(q, k0, kw, carry, (c < r) if diag else None)
        o_ref[i * tq:(i + 1) * tq, :] = carry[1].astype(o_ref.dtype)


def _attention(kernel, q_arr, q_col, k_arr, k_col, v_arr, v_col, heads, batch, seq,
               pre=(), pre_specs=(), name="attention"):
    t = batch * seq
    return pl.pallas_call(
        kernel,
        grid=(batch, heads),
        in_specs=[*pre_specs,
                  pl.BlockSpec((seq, HEAD_DIM), lambda b, h: (b, q_col + h)),
                  pl.BlockSpec((seq, HEAD_DIM), lambda b, h: (b, k_col + h)),
                  pl.BlockSpec((seq, HEAD_DIM), lambda b, h: (b, v_col + h))],
        out_specs=pl.BlockSpec((seq, HEAD_DIM), lambda b, h: (b, h)),
        out_shape=jax.ShapeDtypeStruct((t, heads * HEAD_DIM), BF16),
        compiler_params=_params(2),
        name=name,
    )(*pre, q_arr, k_arr, v_arr)


def _sortable(x):
    bits = pltpu.bitcast(x + 0.0, I32)
    return jnp.where(bits < 0, bits ^ jnp.int32(0x7FFFFFFF), bits)


def _attn_sparse_kernel(q_ref, sq_ref, kv_ref, ik_ref, o_ref, key_ref, thr_ref, *, tq, seq, heads,
                        topk, iq_col, iw_col):
    i = pl.program_id(1)
    n_tiles = seq // tq
    q0 = i * tq

    sq = sq_ref[...]
    lane_k = lax.broadcasted_iota(I32, (tq, LANES), 1)
    iq_groups = [sq[:, iq_col + g * LANES: iq_col + (g + 1) * LANES].astype(BF16)
                 for g in range(IDX_HEADS * IDX_DIM // LANES)]
    iw = [jnp.broadcast_to(sq[:, iw_col + h: iw_col + h + 1], (tq, tq)) for h in range(IDX_HEADS)]
    row = lax.broadcasted_iota(I32, (tq, tq), 0)
    col = lax.broadcasted_iota(I32, (tq, tq), 1)
    chunk_end = ((q0 + row) // CHUNK + 1) * CHUNK

    key_ref[...] = jnp.full((tq, seq), INT_MIN, I32)

    def score_tile(j, c):
        ikt = _rows(ik_ref, j, tq)
        ik_lo = jnp.where(lane_k < IDX_DIM, ikt, 0.0)
        ik_hi = pltpu.roll(ik_lo, IDX_DIM, 1)
        ik_lo = ik_lo.astype(BF16)
        ik_hi = ik_hi.astype(BF16)
        score = jnp.zeros((tq, tq), F32)
        for h in range(IDX_HEADS):
            s = _qk(iq_groups[h // 2], ik_hi if h % 2 else ik_lo)
            score = score + iw[h] * jnp.maximum(s, 0.0)
        vis = (j * tq + col) < chunk_end
        key_ref[:, pl.ds(pl.multiple_of(j * tq, tq), tq)] = jnp.where(vis, _sortable(score), INT_MIN)
        return c

    lax.fori_loop(0, i + 1, score_tile, 0)

    def count_ge(cand, ncols=seq):
        return jnp.sum(jnp.where(key_ref[:, :ncols] >= cand, 1.0, 0.0), axis=-1, keepdims=True)

    kf = jnp.float32(topk)

    def find_threshold(ncols):
        thr = jnp.where(count_ge(jnp.zeros((tq, 1), I32), ncols) >= kf, 0, INT_MIN).astype(I32)

        def bisect(b, thr):
            cand = thr | lax.shift_left(jnp.int32(1), 30 - b)
            return jnp.where(count_ge(cand, ncols) >= kf, cand, thr)

        thr = lax.fori_loop(0, jnp.where((i + 1) * tq <= topk, 0, 31), bisect, thr)
        thr_ref[...] = jnp.maximum(thr, INT_MIN + 1)

    lo = 0
    for hi in sorted({min(n_tiles, v) for v in range(2, n_tiles + 2, 2)}):
        pl.when((i + 1 > lo) & (i + 1 <= hi))(functools.partial(find_threshold, hi * tq))
        lo = hi
    thr = thr_ref[...]
    n_ge = count_ge(thr)

    @pl.when(jnp.max(n_ge) > kf)
    def _():
        need = kf - jnp.sum(jnp.where(key_ref[...] > thr, 1.0, 0.0), axis=-1, keepdims=True)
        upto = jnp.where(row <= col, 1.0, 0.0).astype(BF16)

        def fix(j, seen):
            cols = pl.ds(pl.multiple_of(j * tq, tq), tq)
            kt = key_ref[:, cols]
            eq = kt == thr
            rank = jnp.dot(jnp.where(eq, 1.0, 0.0).astype(BF16), upto, preferred_element_type=F32) + seen
            key_ref[:, cols] = jnp.where(eq & (rank > need), INT_MIN, kt)
            return seen + jnp.sum(jnp.where(eq, 1.0, 0.0), axis=-1, keepdims=True)

        lax.fori_loop(0, n_tiles, fix, jnp.zeros((tq, 1), F32))

    qs = jnp.concatenate([q_ref[:, h * HEAD_DIM:(h + 1) * HEAD_DIM] for h in range(heads)], axis=0)

    def attn_tile(j, carry):
        kvt = _rows(kv_ref, j, tq)
        kt = kvt[:, :HEAD_DIM].astype(BF16)
        vt = kvt[:, HEAD_DIM:].astype(BF16)
        bias = jnp.where(key_ref[:, pl.ds(pl.multiple_of(j * tq, tq), tq)] >= thr, 0.0, NEG)
        s = _qk(qs, kt).reshape(heads, tq, tq) + bias[None]
        return _softmax_tile(carry, s.reshape(heads * tq, tq), vt)

    _, l, acc = lax.fori_loop(0, i + 1, attn_tile, _softmax_init(heads * tq))
    o = acc / l
    for h in range(heads):
        o_ref[:, h * HEAD_DIM:(h + 1) * HEAD_DIM] = o[h * tq:(h + 1) * tq].astype(o_ref.dtype)


def _attn_sparse(dq, small, heads, batch, seq, tq, topk, iq_col, ik_col, iw_col):
    nq = seq // tq
    t = batch * seq
    bw = heads * HEAD_DIM
    w = small.shape[1]
    kern = functools.partial(_attn_sparse_kernel, tq=tq, seq=seq, heads=heads, topk=topk,
                             iq_col=iq_col, iw_col=iw_col)
    return pl.pallas_call(
        kern,
        grid=(batch, nq),
        in_specs=[pl.BlockSpec((tq, bw), lambda b, i: (b * nq + i, 0)),
                  pl.BlockSpec((tq, w), lambda b, i: (b * nq + i, 0)),
                  pl.BlockSpec((seq, 2 * HEAD_DIM), lambda b, i: (b, 0)),
                  pl.BlockSpec((seq, LANES), lambda b, i: (b, ik_col // LANES))],
        out_specs=pl.BlockSpec((tq, bw), lambda b, i: (b * nq + i, 0)),
        out_shape=jax.ShapeDtypeStruct((t, bw), BF16),
        scratch_shapes=[pltpu.VMEM((tq, seq), I32), pltpu.VMEM((tq, 1), I32)],
        compiler_params=_params(2),
        name="attn_sparse",
    )(dq, small, small, small)


def _forget_cumsum_kernel(x_ref, b_ref, o_ref, *, seq, blk):
    r = lax.broadcasted_iota(I32, (blk, blk), 0)
    c = lax.broadcasted_iota(I32, (blk, blk), 1)
    tri = jnp.where(c <= r, 1.0, 0.0).astype(BF16)
    carry = jnp.zeros((1, LANES), F32)
    for s in range(seq // blk):
        logf = _log_sigmoid(x_ref[s * blk:(s + 1) * blk, :] + b_ref[...])
        hi = logf.astype(BF16)
        rem = logf - hi.astype(F32)
        mid = rem.astype(BF16)
        lo = (rem - mid.astype(F32)).astype(BF16)
        local = (jnp.dot(tri, hi, preferred_element_type=F32)
                 + jnp.dot(tri, mid, preferred_element_type=F32)
                 + jnp.dot(tri, lo, preferred_element_type=F32)) + carry
        o_ref[s * blk:(s + 1) * blk, :] = local
        carry = local[blk - 1:blk, :]


def _forget_cumsum(small, bias_row, batch, seq, col):
    return pl.pallas_call(
        functools.partial(_forget_cumsum_kernel, seq=seq, blk=_tile(seq, 256)),
        grid=(batch,),
        in_specs=[pl.BlockSpec((seq, LANES), lambda b: (b, col // LANES)),
                  pl.BlockSpec((1, LANES), lambda b: (0, 0))],
        out_specs=pl.BlockSpec((seq, LANES), lambda b: (b, 0)),
        out_shape=jax.ShapeDtypeStruct((batch * seq, LANES), F32),
        compiler_params=_params(1),
        name="forget_cumsum",
    )(small, bias_row)


def _merge_kernel(*refs, n_br, cast_rows):
    o_refs = refs[:n_br]
    g_refs = refs[n_br:2 * n_br]
    w_ref = refs[2 * n_br]
    out_ref = refs[2 * n_br + 1]
    wbf_ref = refs[2 * n_br + 2]

    @pl.when(pl.program_id(1) == 0)
    def _():
        for b in range(n_br):
            def body(r, c, b=b):
                rows = pl.ds(pl.multiple_of(r * cast_rows, cast_rows), cast_rows)
                wbf_ref[b, rows, :] = w_ref[b, rows, :].astype(BF16)
                return c
            lax.fori_loop(0, w_ref.shape[1] // cast_rows, body, 0)

    acc = None
    for b in range(n_br):
        term = g_refs[b][...] * jnp.dot(o_refs[b][...], wbf_ref[b], preferred_element_type=F32)
        acc = term if acc is None else acc + term
    out_ref[...] = acc.astype(out_ref.dtype)


def _merge(outs, gates, w_branch, layer, tm, tn):
    _, n_br, bw, d = w_branch.shape
    t = outs[0].shape[0]
    nd = d // tn
    kern = functools.partial(_merge_kernel, n_br=n_br, cast_rows=_tile(bw, 256))
    return pl.pallas_call(
        kern,
        grid=(nd, t // tm),
        in_specs=[*[pl.BlockSpec((tm, bw), lambda j, i: (i, 0)) for _ in range(n_br)],
                  *[pl.BlockSpec((tm, tn), lambda j, i, b=b: (i, b * nd + j)) for b in range(n_br)],
                  pl.BlockSpec((None, n_br, bw, tn), lambda j, i: (layer, 0, 0, j),
                               pipeline_mode=pl.Buffered(1))],
        out_specs=pl.BlockSpec((tm, tn), lambda j, i: (i, j)),
        out_shape=jax.ShapeDtypeStruct((t, d), BF16),
        scratch_shapes=[pltpu.VMEM((n_br, bw, tn), BF16)],
        compiler_params=_params(2),
        name="gated_merge",
    )(*outs, *([gates] * n_br), w_branch)


def _router_kernel(h_ref, w_ref, b_ref, meta_i_ref, meta_w_ref, cnt_ref, run_ref, *, tm, n_exp):
    @pl.when(pl.program_id(0) == 0)
    def _():
        run_ref[...] = jnp.zeros_like(run_ref)

    lane = lax.broadcasted_iota(I32, (tm, LANES), 1)
    lane_f = lane.astype(F32)
    logits = jnp.dot(h_ref[...], w_ref[...].astype(BF16), preferred_element_type=F32) + b_ref[...]
    logits = jnp.where(lane < n_exp, logits, -jnp.inf)

    vals, idxs = [], []
    work = logits
    for _ in range(TOP_K):
        v = jnp.max(work, axis=-1, keepdims=True)
        ix = jnp.min(jnp.where(work == v, lane_f, float(LANES)), axis=-1, keepdims=True).astype(I32)
        vals.append(v)
        idxs.append(ix)
        work = jnp.where(lane == ix, -jnp.inf, work)
    es = [jnp.exp(v - vals[0]) for v in vals]
    denom = es[0] + es[1] + es[2] + es[3]

    onehots = [jnp.where(lane == ix, 1.0, 0.0) for ix in idxs]
    sel = onehots[0] + onehots[1] + onehots[2] + onehots[3]
    r = lax.broadcasted_iota(I32, (tm, tm), 0)
    c = lax.broadcasted_iota(I32, (tm, tm), 1)
    before = jnp.where(c < r, 1.0, 0.0).astype(BF16)
    rank = jnp.dot(before, sel.astype(BF16), preferred_element_type=F32) + run_ref[...]
    run_ref[...] = run_ref[...] + jnp.sum(sel, axis=0, keepdims=True)

    meta_i = jnp.zeros((tm, LANES), I32)
    meta_w = jnp.zeros((tm, LANES), F32)
    for k in range(TOP_K):
        rk = jnp.sum(onehots[k] * rank, axis=-1, keepdims=True).astype(I32)
        meta_i = jnp.where(lane == k, idxs[k], meta_i)
        meta_i = jnp.where(lane == TOP_K + k, rk, meta_i)
        meta_w = jnp.where(lane == k, es[k] / denom, meta_w)
    meta_i_ref[...] = meta_i
    meta_w_ref[...] = meta_w
    cnt_ref[...] = run_ref[...]


def _router(h, w_router_pad, b_router_pad, n_exp):
    t, d = h.shape
    tm = _tile(t, 256)
    kern = functools.partial(_router_kernel, tm=tm, n_exp=n_exp)
    return pl.pallas_call(
        kern,
        grid=(t // tm,),
        in_specs=[pl.BlockSpec((tm, d), lambda i: (i, 0)),
                  pl.BlockSpec((d, LANES), lambda i: (0, 0)),
                  pl.BlockSpec((1, LANES), lambda i: (0, 0))],
        out_specs=[pl.BlockSpec((tm, LANES), lambda i: (i, 0)),
                   pl.BlockSpec((tm, LANES), lambda i: (i, 0)),
                   pl.BlockSpec((1, LANES), lambda i: (0, 0))],
        out_shape=[jax.ShapeDtypeStruct((t, LANES), I32),
                   jax.ShapeDtypeStruct((t, LANES), F32),
                   jax.ShapeDtypeStruct((1, LANES), F32)],
        scratch_shapes=[pltpu.VMEM((1, LANES), F32)],
        compiler_params=_params(1),
        name="router",
    )(h, w_router_pad, b_router_pad)


def _ffn_kernel(te_ref, nu_ref, tok_ref, h_ref, wgu_hbm, bgu_ref, wdn_hbm, bdn_ref, o_ref,
                xbuf, wgu_buf, wdn_buf, pick_ref, sem, wsem, *, tm, nseg, kc, nc, layer):
    i = pl.program_id(0)
    n_used = nu_ref[0]
    d, ff = wgu_buf.shape[0], wdn_buf.shape[0]
    pitch = _row_pitch(nseg)
    n_kc, n_nc = d // kc, d // nc

    def row_copy(tile, slot, r):
        tok = tok_ref[tile * tm + r]
        return pltpu.make_async_copy(h_ref.at[pl.ds(tok * pitch, nseg), :],
                                     xbuf.at[slot, pl.ds(r * pitch, nseg), :], sem.at[slot])

    def fetch(tile, slot):
        def body(r, c):
            row_copy(tile, slot, r).start()
            return c
        lax.fori_loop(0, tm, body, 0, unroll=4)

    def wait_rows(slot):
        done = xbuf.at[slot, pl.ds(0, tm * nseg), :]
        pltpu.make_async_copy(done, done, sem.at[slot]).wait()

    def gu_copy(e, c):
        rows = pl.ds(c * kc, kc)
        return pltpu.make_async_copy(wgu_hbm.at[layer, e, rows, :], wgu_buf.at[rows, :], wsem.at[c])

    def dn_copy(e, c):
        cols = pl.ds(c * nc, nc)
        return pltpu.make_async_copy(wdn_hbm.at[layer, e, :, cols], wdn_buf.at[:, cols],
                                     wsem.at[n_kc + c])

    active = i < n_used
    new_expert = (i == 0) | (te_ref[i] != te_ref[jnp.maximum(i - 1, 0)])

    @pl.when(active & new_expert)
    def _():
        for c in range(n_kc):
            gu_copy(te_ref[i], c).start()
        for c in range(n_nc):
            dn_copy(te_ref[i], c).start()

    @pl.when(i == 0)
    def _():
        r = lax.broadcasted_iota(I32, (2 * ff, ff), 0)
        c = lax.broadcasted_iota(I32, (2 * ff, ff), 1)
        pick_ref[...] = jnp.where(r == 2 * c, 1.0, 0.0).astype(BF16)
        fetch(0, 0)

    @pl.when(i + 1 < n_used)
    def _():
        fetch(i + 1, (i + 1) % 2)

    def compute(arriving):
        slot = i % 2
        wait_rows(slot)
        parts = _load_token_rows(xbuf, tm, nseg, lead=(slot,))
        per = kc // LANES
        gu = bgu_ref[...]
        for c in range(n_kc):
            if arriving is not None:
                gu_copy(arriving, c).wait()
            xc = jnp.concatenate(parts[c * per:(c + 1) * per], axis=1).astype(BF16)
            gu = gu + jnp.dot(xc, wgu_buf[c * kc:(c + 1) * kc, :].astype(BF16),
                              preferred_element_type=F32)
        lin = pltpu.roll(gu, 2 * ff - 1, 1)
        glu = jnp.minimum(gu, SWIGLU_LIMIT)
        lin = jnp.clip(lin, -SWIGLU_LIMIT, SWIGLU_LIMIT)
        act = glu * jax.nn.sigmoid(SWIGLU_ALPHA * glu) * (lin + 1.0)
        act = jnp.dot(act.astype(BF16), pick_ref[...], preferred_element_type=F32).astype(BF16)
        for c in range(n_nc):
            if arriving is not None:
                dn_copy(arriving, c).wait()
            cols = slice(c * nc, (c + 1) * nc)
            o_ref[:, cols] = (jnp.dot(act, wdn_buf[:, cols].astype(BF16), preferred_element_type=F32)
                              + bdn_ref[:, cols])

    pl.when(active & new_expert)(lambda: compute(te_ref[i]))
    pl.when(active & jnp.logical_not(new_expert))(lambda: compute(None))

    @pl.when(i >= n_used)
    def _():
        o_ref[...] = jnp.zeros_like(o_ref)


def _expert_ffn(h_rows, tile_expert, n_used, token_of_row, w_gu, b_gu, w_dn, b_dn, layer, tm):
    _, n_exp, d, ff2 = w_gu.shape
    ff = ff2 // 2
    nseg = d // LANES
    n_tiles = token_of_row.shape[0] // tm
    kc, nc = _tile(d, 512), _tile(d, 1024)

    def wsel(i, te, nu, tok):
        return (layer, te[i], 0, 0)

    kern = functools.partial(_ffn_kernel, tm=tm, nseg=nseg, kc=kc, nc=nc, layer=layer)
    return pl.pallas_call(
        kern,
        grid_spec=pltpu.PrefetchScalarGridSpec(
            num_scalar_prefetch=3,
            grid=(n_tiles,),
            in_specs=[pl.BlockSpec(memory_space=pl.ANY),
                      pl.BlockSpec(memory_space=pl.ANY),
                      pl.BlockSpec((None, None, 1, ff2), wsel),
                      pl.BlockSpec(memory_space=pl.ANY),
                      pl.BlockSpec((None, None, 1, d), wsel)],
            out_specs=pl.BlockSpec((tm, d), lambda i, te, nu, tok: (i, 0)),
            scratch_shapes=[pltpu.VMEM((2, tm * _row_pitch(nseg), LANES), F32),
                            pltpu.VMEM((d, ff2), F32),
                            pltpu.VMEM((ff, d), F32),
                            pltpu.VMEM((ff2, ff), BF16),
                            pltpu.SemaphoreType.DMA((2,)),
                            pltpu.SemaphoreType.DMA((d // kc + d // nc,))]),
        out_shape=jax.ShapeDtypeStruct((n_tiles * tm, d), F32),
        compiler_params=_params(1),
        name="expert_ffn",
    )(tile_expert, n_used, token_of_row, h_rows, w_gu, b_gu, w_dn, b_dn)


def _combine_kernel(pos_ref, ys_ref, x_ref, g_ref, w_ref, o_ref, buf, sem, *, tm):
    i = pl.program_id(0)
    last = pl.num_programs(0) - 1

    def row_copy(tile, slot, t, k):
        row = pos_ref[(tile * tm + t) * TOP_K + k]
        return pltpu.make_async_copy(ys_ref.at[pl.ds(row, 1), :], buf.at[slot, k, pl.ds(t, 1), :],
                                     sem.at[slot])

    def wait_rows(slot):
        pltpu.make_async_copy(buf.at[slot], buf.at[slot], sem.at[slot]).wait()

    def fetch(tile, slot):
        def body(t, c):
            for k in range(TOP_K):
                row_copy(tile, slot, t, k).start()
            return c
        lax.fori_loop(0, tm, body, 0, unroll=2)

    @pl.when(i == 0)
    def _():
        fetch(0, 0)

    @pl.when(i < last)
    def _():
        fetch(i + 1, (i + 1) % 2)

    slot = i % 2
    wait_rows(slot)
    w = w_ref[...]
    y = w[:, 0:1] * buf[slot, 0]
    for k in range(1, TOP_K):
        y = y + w[:, k:k + 1] * buf[slot, k]
    o_ref[...] = x_ref[...] + g_ref[...] * y


def _combine(ys, pos_flat, x, m4, gate_idx, meta_w, seq, tm):
    t, d = x.shape
    per_b = seq // tm
    return pl.pallas_call(
        functools.partial(_combine_kernel, tm=tm),
        grid_spec=pltpu.PrefetchScalarGridSpec(
            num_scalar_prefetch=1,
            grid=(t // tm,),
            in_specs=[pl.BlockSpec(memory_space=pl.ANY),
                      pl.BlockSpec((tm, d), lambda i, p: (i, 0)),
                      pl.BlockSpec((None, None, 1, d), lambda i, p: (i // per_b, gate_idx, 0, 0)),
                      pl.BlockSpec((tm, LANES), lambda i, p: (i, 0))],
            out_specs=pl.BlockSpec((tm, d), lambda i, p: (i, 0)),
            scratch_shapes=[pltpu.VMEM((2, TOP_K, tm, d), F32),
                            pltpu.SemaphoreType.DMA((2,))]),
        out_shape=jax.ShapeDtypeStruct((t, d), F32),
        compiler_params=_params(1),
        name="moe_combine",
    )(pos_flat, ys, x, m4, meta_w)


def _moe(h_bf, h_rows, x, m4, w_router, b_router, w_gu, b_gu, w_dn, b_dn, layer, seq):
    t, d = x.shape
    n_exp = w_router.shape[1]
    tm = _tile(t, 256)
    wr = jnp.pad(w_router, ((0, 0), (0, LANES - n_exp)))
    br = jnp.pad(b_router, (0, LANES - n_exp))[None, :]
    meta_i, meta_w, counts = _router(h_bf, wr, br, n_exp)

    counts = counts[0, :n_exp].astype(I32)
    padded = ((counts + tm - 1) // tm) * tm
    ends = jnp.cumsum(padded)
    starts = ends - padded
    n_rows = t * TOP_K + n_exp * tm
    n_tiles = n_rows // tm
    experts = meta_i[:, :TOP_K]
    pos = starts[experts] + meta_i[:, TOP_K:2 * TOP_K]
    tile_start = jnp.arange(n_tiles, dtype=I32) * tm
    tile_expert = jnp.minimum(jnp.sum((ends[None, :] <= tile_start[:, None]).astype(I32), axis=1),
                              n_exp - 1)
    n_used = (ends[-1] // tm).astype(I32)[None]
    token_of_row = jnp.zeros((n_rows,), I32).at[pos.reshape(-1)].set(
        jnp.repeat(jnp.arange(t, dtype=I32), TOP_K), unique_indices=True, mode="promise_in_bounds")

    ys = _expert_ffn(h_rows, tile_expert, n_used, token_of_row, w_gu, b_gu[:, :, None, :], w_dn,
                     b_dn[:, :, None, :], layer, tm)
    return _combine(ys, pos.reshape(-1), x, m4, 5, meta_w, seq, _tile(seq, 128))


def _col_spec(k, tn, off_blocks=0):
    return pl.BlockSpec((k, tn), lambda j, i: (0, j + off_blocks))


def _layer_col_spec(layer, k, tn, off_blocks=0):
    return pl.BlockSpec((None, k, tn), lambda j, i: (layer, 0, j + off_blocks))


def _layer(x, m4, tabs, lam_init, batch, seq, layer, p):
    t, d = x.shape
    bw = p["w_branch"].shape[2]
    heads = bw // HEAD_DIM
    tm = _tile(seq, 1024)
    tn = _tile(bw, 512)
    w_in_t = p["w_in_t"]
    zero = lambda n: jnp.zeros((1, n), F32)

    (h,) = _ada_norm(x, p["norm1_g"], m4, 1, 0, seq, (BF16,))

    tab8, tab16, tab_small = tabs
    tab_spec = lambda w: pl.BlockSpec((tm, w), lambda j, i: (i, 0))
    rope8 = functools.partial(_rope_epilogue, groups=((0, 8),) * (tn // LANES))
    rope16 = functools.partial(_rope_epilogue, groups=((0, 16),) * (tn // LANES))

    def in_proj(row0, n, name, **kw):
        spec = pl.BlockSpec((pl.Element(1), pl.Element(tn), pl.Element(d)),
                            lambda j, i: (layer, pl.multiple_of(row0 + j * tn, 8), 0))
        return _matmul(h, w_in_t, spec, d, n, zero(n), tm=tm, tn=tn, out_dtype=BF16,
                       w_rows_are_outputs=True, name=name, **kw)

    sm_scale = HEAD_DIM ** -0.5
    col_spec = pl.BlockSpec((1, tn), lambda j, i: (0, j))

    def col_factors(*pieces):
        return jnp.concatenate([jnp.full((1, n), v, F32) for v, n in pieces], axis=1)

    c0 = 6 * bw + heads
    qk_a = in_proj(0, 2 * bw, "proj_a_qk", epilogue=lambda acc, tab, f: rope8(acc, tab) * f,
                   extra=(tab8, col_factors(((HEAD_DIM // 2) ** -0.5 * LOG2E, bw), (1.0, bw))),
                   extra_specs=(tab_spec(3 * LANES), col_spec))
    mid = in_proj(2 * bw, 4 * bw, "proj_av_b", epilogue=lambda acc, f: acc * f,
                  extra=(col_factors((1.0, bw), (sm_scale * LOG2E, bw), (1.0, 2 * bw)),),
                  extra_specs=(col_spec,))
    qkv_c = in_proj(c0, 3 * bw, "proj_c", epilogue=lambda acc, f: acc * f,
                    extra=(col_factors((sm_scale, bw), (1.0, 2 * bw)),), extra_specs=(col_spec,))
    q_d = in_proj(c0 + 3 * bw, bw, "proj_d_q",
                  epilogue=lambda acc, tab: rope16(acc, tab) * (sm_scale * LOG2E),
                  extra=(tab16,), extra_specs=(tab_spec(3 * LANES),))
    s0 = c0 + 4 * bw
    n_iq = IDX_HEADS * IDX_DIM
    used = 2 * HEAD_DIM + n_iq + IDX_DIM + IDX_HEADS + heads
    ws = -(-used // LANES) * LANES
    w_small_t = jnp.concatenate(
        [w_in_t[layer, s0:s0 + 2 * HEAD_DIM + n_iq + IDX_DIM + IDX_HEADS],
         w_in_t[layer, 6 * bw:6 * bw + heads], jnp.zeros((ws - used, d), F32)], axis=0)
    ik_col = 2 * HEAD_DIM + n_iq
    iw_col = ik_col + IDX_DIM
    fl_col = iw_col + IDX_HEADS
    groups_small = ((0, 16), None) + ((1, 8),) * (n_iq // LANES) + ((2, 8),)
    tm_s = _tile(seq, 512)
    small = _matmul(h, w_small_t, pl.BlockSpec((ws, d), lambda j, i: (0, 0), pipeline_mode=pl.Buffered(1)),
                    d, ws, zero(ws), tm=tm_s, tn=ws, out_dtype=F32, w_rows_are_outputs=True,
                    epilogue=functools.partial(_rope_epilogue, groups=groups_small), extra=(tab_small,),
                    extra_specs=(pl.BlockSpec((tm_s, 9 * LANES), lambda j, i: (i, 0)),), name="proj_small")

    fl_lane = fl_col % LANES
    bias_row = jnp.zeros((1, LANES), F32).at[0, fl_lane:fl_lane + heads].set(p["b_forget"])
    dcum = _forget_cumsum(small, bias_row, batch, seq, ik_col)[:, fl_lane:fl_lane + heads]
    dcum = dcum.reshape(batch, seq, heads).transpose(0, 2, 1)
    dq_col = dcum[..., None]
    dk_row = dcum[:, :, None, :]

    tq = tk = _tile(seq, 512)
    hb = bw // HEAD_DIM
    o_a = _attention(
        functools.partial(_attn_diff_kernel, tq=tq, tk=tk, seq=seq, lam_init=lam_init),
        qk_a, 0, qk_a, hb, mid, 0, heads, batch, seq,
        pre=(p["diff_lambda"], p["diff_norm_g"][None, :]),
        pre_specs=(pl.BlockSpec(p["diff_lambda"].shape, lambda b, h: (0, 0)),
                   pl.BlockSpec((1, HEAD_DIM), lambda b, h: (0, 0))),
        name="attn_diff")
    o_b = _attention(
        functools.partial(_attn_forget_kernel, tq=tq, tk=tk, seq=seq),
        mid, hb, mid, 2 * hb, mid, 3 * hb, heads, batch, seq,
        pre=(dq_col, dk_row),
        pre_specs=(pl.BlockSpec((None, None, seq, 1), lambda b, h: (b, h, 0, 0)),
                   pl.BlockSpec((None, None, 1, seq), lambda b, h: (b, h, 0, 0))),
        name="attn_forget")
    o_c = _attention(functools.partial(_attn_stick_kernel, tq=tq, tk=tk, seq=seq),
                     qkv_c, 0, qkv_c, hb, qkv_c, 2 * hb, heads, batch, seq, name="attn_stick")
    o_d = _attn_sparse(q_d, small, heads, batch, seq, _tile(seq, 256), min(TOPK_MAX, seq // 4),
                       2 * HEAD_DIM, ik_col, iw_col)

    n_br = p["w_gate"].shape[1]
    nd = d // tn
    gates = _matmul(h, p["w_gate"],
                    pl.BlockSpec((None, None, d, tn), lambda j, i: (layer, j // nd, 0, j % nd)),
                    d, n_br * d, p["b_gate"].reshape(1, n_br * d), tm=tm, tn=tn, out_dtype=F32,
                    epilogue=jax.nn.sigmoid, name="gates")
    merged = _merge((o_a, o_b, o_c, o_d), gates, p["w_branch"], layer, tm, tn)
    per_b = seq // tm
    x = _matmul(merged, p["w_out"], _layer_col_spec(layer, d, tn), d, d, zero(d), tm=tm, tn=tn,
                out_dtype=F32, epilogue=lambda acc, xr, g: xr + g * acc, extra=(x, m4),
                extra_specs=(pl.BlockSpec((tm, tn), lambda j, i: (i, j)),
                             pl.BlockSpec((None, None, 1, tn), lambda j, i: (i // per_b, 2, 0, j))),
                name="out_proj")

    h_bf, h_rows = _ada_norm(x, p["norm2_g"], m4, 4, 3, seq, (BF16, F32), token_rows=(False, True))
    return _moe(h_bf, h_rows, x, m4, p["w_router"], p["b_router"], p["w_gu"], p["b_gu"],
                p["w_dn"], p["b_dn"], layer, seq)


@jax.jit
def _forward(x, c, positions, w_mod, b_mod, ada_table, norm1_g, w_in, b_forget, diff_lambda,
             diff_norm_g, w_branch, w_gate, b_gate, w_out, norm2_g, w_router, b_router,
             w_gu, b_gu, w_dn, b_dn, final_norm_g):
    batch, seq, d = x.shape
    depth = w_in.shape[0]
    t = batch * seq
    xt = x.reshape(t, d)

    c8 = jnp.pad(c, ((0, 8 - batch), (0, 0)))
    n_mod = w_mod.shape[1]
    mod = _matmul(c8, w_mod, _col_spec(d, _tile(n_mod, 512)), d, n_mod, b_mod[None, :], tm=8,
                  tn=_tile(n_mod, 512), out_dtype=F32, prologue=lambda v: v * jax.nn.sigmoid(v),
                  name="mod_proj")[:batch]
    mod = mod.reshape(batch, N_MOD, d)

    pos = positions.reshape(t).astype(F32)
    tabs = (_rope_table(pos, HEAD_DIM // 2, LANES),
            _rope_table(pos, HEAD_DIM, LANES),
            jnp.concatenate([_rope_table(pos, HEAD_DIM, LANES), _rope_table(pos, IDX_DIM, LANES),
                             _rope_table(pos, IDX_DIM, IDX_DIM)], axis=1))

    w_in_t = jnp.swapaxes(w_in, 1, 2)
    for l in range(depth):
        m4 = (mod + ada_table[l][None])[:, :, None, :]
        lam_init = 0.8 - 0.6 * float(np.exp(-0.3 * l))
        p = dict(norm1_g=norm1_g[l][None, :], w_in_t=w_in_t, b_forget=b_forget[l],
                 diff_lambda=diff_lambda[l], diff_norm_g=diff_norm_g[l], w_branch=w_branch,
                 w_gate=w_gate, b_gate=b_gate[l], w_out=w_out, norm2_g=norm2_g[l][None, :],
                 w_router=w_router[l], b_router=b_router[l], w_gu=w_gu, b_gu=b_gu,
                 w_dn=w_dn, b_dn=b_dn)
        xt = _layer(xt, m4, tabs, lam_init, batch, seq, l, p)

    zeros4 = jnp.zeros((batch, 1, 1, d), F32)
    (out,) = _ada_norm(xt, final_norm_g[None, :], zeros4, 0, 0, seq, (F32,))
    return out.reshape(batch, seq, d)


def kernel(x, c, positions, w_mod, b_mod, ada_table, norm1_g, w_in, b_forget, diff_lambda,
           diff_norm_g, w_branch, w_gate, b_gate, w_out, norm2_g, w_router, b_router,
           w_gu, b_gu, w_dn, b_dn, final_norm_g):
    return _forward(x, c, positions, w_mod, b_mod, ada_table, norm1_g, w_in, b_forget, diff_lambda,
                    diff_norm_g, w_branch, w_gate, b_gate, w_out, norm2_g, w_router, b_router,
                    w_gu, b_gu, w_dn, b_dn, final_norm_g)
```
